```python
import math
import jax
import jax.numpy as jnp
from jax import lax
import numpy as np

D_MODEL = 1024
BATCH = 8
SEQ = 4096
DEPTH = 2

GRID_W = 64
CTX_LEN = 256
N_EVEN = (DEPTH + 1) // 2
N_ODD = DEPTH // 2
EPS = 1e-6

CONV_CH = D_MODEL // 2
CONV_W = 3
NA_HEADS = 8
NA_HEAD_DIM = D_MODEL // 2 // NA_HEADS
NA_DIM = NA_HEADS * NA_HEAD_DIM
WIN_H = 8
WIN_W = 16
EVEN_IN = 3 * CONV_CH + 3 * NA_DIM
EVEN_OUT = CONV_CH + NA_DIM

RET_HEADS = 4
RET_QK_DIM = D_MODEL // RET_HEADS
RET_V_DIM = 2 * RET_QK_DIM
RET_CHUNK = 128
RET_QK = RET_HEADS * RET_QK_DIM
RET_V = RET_HEADS * RET_V_DIM
ODD_IN = 2 * RET_QK + 2 * RET_V
ROPE_BASE = 10000.0

N_GROUPS = 4
EXPERTS_PER_GROUP = 8
EXPERT_HIDDEN = D_MODEL // 4
TOP_K_IN_GROUP = 2
GROUP_HIDDEN = EXPERTS_PER_GROUP * EXPERT_HIDDEN

kernel_name = 'hybrid_dit_shortconv_natten_retnet_hmoe'


def rmsnorm(x, g):
    xf = x.astype(jnp.float32)
    y = xf * lax.rsqrt(jnp.mean(xf * xf, axis=-1, keepdims=True) + EPS)
    return (y * g.astype(jnp.float32)).astype(x.dtype)


def modulate(h, shift, scale):
    return h * (1 + scale) + shift


def short_conv(u, w):
    up = jnp.pad(u, ((0, 0), (1, 1), (0, 0)))
    return up[:, :-2] * w[0] + up[:, 1:-1] * w[1] + up[:, 2:] * w[2]


def dense_attention(q, k, v):
    s = jnp.einsum('bqhd,bkhd->bhqk', q, k).astype(jnp.float32)
    p = jax.nn.softmax(s, axis=-1).astype(v.dtype)
    o = jnp.einsum('bhqk,bkhd->bqhd', p, v)
    return o.reshape(o.shape[0], o.shape[1], -1)


def neighborhood_attention(q, k, v, k_ctx, v_ctx, rpb):
    B, L, H, Dh = q.shape
    rows = L // GRID_W
    kh = min(WIN_H, rows)
    kw = min(WIN_W, GRID_W)
    qg = q.reshape(B, rows, GRID_W, H, Dh)
    kg = k.reshape(B, rows, GRID_W, H, Dh)
    vg = v.reshape(B, rows, GRID_W, H, Dh)
    cols = jnp.arange(GRID_W)
    col_idx = jnp.clip(cols - kw // 2, 0, GRID_W - kw)[:, None] + jnp.arange(kw)[None, :]
    rpb_cols = rpb.astype(jnp.float32)[:, :, col_idx - cols[:, None] + WIN_W - 1]

    def row_block(r):
        rs = jnp.clip(r - kh // 2, 0, rows - kh)
        q_r = lax.dynamic_index_in_dim(qg, r, axis=1, keepdims=False)
        k_win = lax.dynamic_slice_in_dim(kg, rs, kh, axis=1)[:, :, col_idx]
        v_win = lax.dynamic_slice_in_dim(vg, rs, kh, axis=1)[:, :, col_idx]
        dy = rs + jnp.arange(kh) - r + WIN_H - 1
        bias = jnp.take(rpb_cols, dy, axis=1).transpose(0, 2, 1, 3)
        s_loc = jnp.einsum('bchd,bicjhd->bhcij', q_r, k_win).astype(jnp.float32) + bias
        s_ctx = jnp.einsum('bchd,bnhd->bhcn', q_r, k_ctx).astype(jnp.float32)
        logits = jnp.concatenate([s_loc.reshape(B, H, GRID_W, kh * kw), s_ctx], axis=-1)
        p = jax.nn.softmax(logits, axis=-1).astype(v.dtype)
        p_loc = p[..., :kh * kw].reshape(B, H, GRID_W, kh, kw)
        return (jnp.einsum('bhcij,bicjhd->bchd', p_loc, v_win)
                + jnp.einsum('bhcn,bnhd->bchd', p[..., kh * kw:], v_ctx))

    out = lax.map(row_block, jnp.arange(rows))
    return jnp.moveaxis(out, 0, 1).reshape(B, L, H * Dh)


def rope_1d(t, pos):
    half = t.shape[-1] // 2
    freqs = ROPE_BASE ** (-jnp.arange(half, dtype=jnp.float32) / half)
    ang = pos.astype(jnp.float32)[:, None] * freqs[None, :]
    cos, sin = jnp.cos(ang), jnp.sin(ang)
    t1, t2 = t[..., :half], t[..., half:]
    return jnp.concatenate([t1 * cos - t2 * sin, t2 * cos + t1 * sin], axis=-1)


def axial_rope(t, n_tokens):
    pos = jnp.arange(n_tokens)
    half = t.shape[-1] // 2
    return jnp.concatenate([rope_1d(t[..., :half], pos // GRID_W),
                            rope_1d(t[..., half:], pos % GRID_W)], axis=-1)


def retention_scan(q, k, v, log_gamma, s0):
    B, H, L, _ = q.shape
    Dv = v.shape[-1]
    n = L // RET_CHUNK
    pos = jnp.arange(RET_CHUNK, dtype=jnp.float32)
    rel = pos[:, None] - pos[None, :]
    d_intra = jnp.where(rel >= 0, jnp.exp(log_gamma[:, None, None] * jnp.maximum(rel, 0.0)), 0.0)
    d_query = jnp.exp(log_gamma[:, None] * (pos + 1.0))[..., None]
    d_key = jnp.exp(log_gamma[:, None] * (RET_CHUNK - 1.0 - pos))[..., None]
    d_chunk = jnp.exp(log_gamma * RET_CHUNK)[:, None, None]

    def chunks(t):
        return jnp.moveaxis(t.reshape(B, H, n, RET_CHUNK, t.shape[-1]), 2, 0)

    def step(state, qkv):
        qc, kc, vc = qkv
        scores = jnp.einsum('bhid,bhjd->bhij', qc, kc) * d_intra
        out = (jnp.einsum('bhij,bhje->bhie', scores, vc)
               + jnp.einsum('bhid,bhde->bhie', qc, state) * d_query)
        state = state * d_chunk + jnp.einsum('bhjd,bhje->bhde', kc * d_key, vc)
        return state, out

    state, out = lax.scan(step, s0, (chunks(q), chunks(k), chunks(v)))
    return jnp.moveaxis(out, 0, 2).reshape(B, H, L, Dv), state


def even_mixer(h, hc, w_in, conv_w, rpb, w_out, with_ctx):
    cuts = [CONV_CH, 2 * CONV_CH, 3 * CONV_CH, 3 * CONV_CH + NA_DIM, 3 * CONV_CH + 2 * NA_DIM]
    a_b, a_c, a_x, q, k, v = jnp.split(h @ w_in, cuts, axis=-1)
    ca_b, ca_c, ca_x, cq, ck, cv = jnp.split(hc @ w_in, cuts, axis=-1)

    def heads(t):
        return t.reshape(t.shape[0], t.shape[1], NA_HEADS, NA_HEAD_DIM)

    scale = NA_HEAD_DIM ** -0.5
    ck, cv = heads(ck), heads(cv)
    y_conv = a_b * short_conv(a_c * a_x, conv_w)
    y_attn = neighborhood_attention(heads(q) * scale, heads(k), heads(v), ck, cv, rpb)
    y = jnp.concatenate([y_conv, y_attn], axis=-1) @ w_out
    if not with_ctx:
        return y, None
    yc_conv = ca_b * short_conv(ca_c * ca_x, conv_w)
    yc_attn = dense_attention(heads(cq) * scale, ck, cv)
    return y, jnp.concatenate([yc_conv, yc_attn], axis=-1) @ w_out


def odd_mixer(h, hc, w_in, decay_f, decay_b, w_out, with_ctx):
    B, L, _ = h.shape
    cuts = [RET_QK, 2 * RET_QK, 2 * RET_QK + RET_V]
    q, k, v, g = jnp.split(h @ w_in, cuts, axis=-1)
    qc, kc, vc, gc = jnp.split(hc @ w_in, cuts, axis=-1)

    def heads(t, d):
        return jnp.swapaxes(t.reshape(t.shape[0], t.shape[1], RET_HEADS, d), 1, 2).astype(jnp.float32)

    scale = RET_QK_DIM ** -0.5
    q = axial_rope(heads(q, RET_QK_DIM), L)
    k = axial_rope(heads(k, RET_QK_DIM), L) * scale
    v = heads(v, RET_V_DIM)
    qc = heads(qc, RET_QK_DIM)
    kc = heads(kc, RET_QK_DIM) * scale
    vc = heads(vc, RET_V_DIM)
    lg_f = jnp.log1p(-jnp.exp(decay_f.astype(jnp.float32)))
    lg_b = jnp.log1p(-jnp.exp(decay_b.astype(jnp.float32)))
    zero = jnp.zeros((B, RET_HEADS, RET_QK_DIM, RET_V_DIM), jnp.float32)
    flip = lambda t: jnp.flip(t, axis=2)
    yc_f, s_f = retention_scan(qc, kc, vc, lg_f, zero)
    yc_b, s_b = retention_scan(flip(qc), flip(kc), flip(vc), lg_b, zero)
    y_f, _ = retention_scan(q, k, v, lg_f, s_f)
    y_b, _ = retention_scan(flip(q), flip(k), flip(v), lg_b, s_b)

    def head_out(y, gate):
        y = y * lax.rsqrt(jnp.mean(y * y, axis=-1, keepdims=True) + EPS)
        y = jnp.swapaxes(y, 1, 2).reshape(gate.shape).astype(gate.dtype)
        return (jax.nn.silu(gate) * y) @ w_out

    y = head_out(y_f + flip(y_b), g)
    yc = head_out(yc_f + flip(yc_b), gc) if with_ctx else None
    return y, yc


def hier_moe(h, gw, gb, ew, eb, w1, w3, w2):
    B, L, _ = h.shape
    gp = jax.nn.softmax((h @ gw + gb).astype(jnp.float32), axis=-1)
    g_val, g_idx = lax.top_k(gp, 1)
    g_onehot = jax.nn.one_hot(g_idx[..., 0], N_GROUPS, dtype=jnp.float32)
    el = (h @ ew + eb).astype(jnp.float32).reshape(B, L, N_GROUPS, EXPERTS_PER_GROUP)
    ep = jax.nn.softmax(jnp.sum(el * g_onehot[..., None], axis=-2), axis=-1)
    e_val, e_idx = lax.top_k(ep, TOP_K_IN_GROUP)
    e_w = e_val / jnp.sum(e_val, axis=-1, keepdims=True) * g_val
    w_in_group = jnp.sum(jax.nn.one_hot(e_idx, EXPERTS_PER_GROUP, dtype=jnp.float32) * e_w[..., None], axis=-2)
    gates = (g_onehot[..., None] * w_in_group[..., None, :]).astype(h.dtype)
    out = jnp.zeros_like(h)
    for gi in range(N_GROUPS):
        hid = jax.nn.silu(h @ w1[gi]) * (h @ w3[gi])
        hid = hid.reshape(B, L, EXPERTS_PER_GROUP, EXPERT_HIDDEN) * gates[:, :, gi, :, None]
        out = out + hid.reshape(B, L, GROUP_HIDDEN) @ w2[gi]
    return out


def setup_inputs(seed: int = 0) -> dict:
    key = jax.random.key(seed)
    ks = jax.random.split(key, 24)
    f32 = jnp.float32
    D = D_MODEL

    def nrm(k, shape, s):
        return jax.random.normal(k, shape, f32) * s

    decay_base = -(5.0 + jnp.arange(RET_HEADS, dtype=f32)) * math.log(2.0)
    return {
        'x': nrm(ks[0], (BATCH, SEQ, D), 1.0),
        'c': nrm(ks[1], (BATCH, D), 1.0),
        'ctx': nrm(ks[2], (BATCH, CTX_LEN, D), 1.0),
        'c_ctx': nrm(ks[3], (D,), 1.0),
        'ada_w': nrm(ks[4], (DEPTH, D, 6 * D), 0.5 * D ** -0.5),
        'ada_b': nrm(ks[5], (DEPTH, 6 * D), 0.02),
        'norm1_g': 1.0 + nrm(ks[6], (DEPTH, D), 0.02),
        'norm2_g': 1.0 + nrm(ks[7], (DEPTH, D), 0.02),
        'w_in_e': nrm(ks[8], (N_EVEN, D, EVEN_IN), D ** -0.5),
        'conv_w': nrm(ks[9], (N_EVEN, CONV_W, CONV_CH), CONV_W ** -0.5),
        'na_rpb': nrm(ks[10], (N_EVEN, NA_HEADS, 2 * WIN_H - 1, 2 * WIN_W - 1), 0.1),
        'w_out_e': nrm(ks[11], (N_EVEN, EVEN_OUT, D), EVEN_OUT ** -0.5),
        'w_in_o': nrm(ks[12], (N_ODD, D, ODD_IN), D ** -0.5),
        'ret_decay_f': decay_base + nrm(ks[13], (N_ODD, RET_HEADS), 0.05),
        'ret_decay_b': decay_base + nrm(ks[14], (N_ODD, RET_HEADS), 0.05),
        'w_out_o': nrm(ks[15], (N_ODD, RET_V, D), RET_V ** -0.5),
        'router_gw': nrm(ks[16], (DEPTH, D, N_GROUPS), D ** -0.5),
        'router_gb': nrm(ks[17], (DEPTH, N_GROUPS), 0.01),
        'router_ew': nrm(ks[18], (DEPTH, D, N_GROUPS * EXPERTS_PER_GROUP), D ** -0.5),
        'router_eb': nrm(ks[19], (DEPTH, N_GROUPS * EXPERTS_PER_GROUP), 0.01),
        'moe_w1': nrm(ks[20], (DEPTH, N_GROUPS, D, GROUP_HIDDEN), D ** -0.5),
        'moe_w3': nrm(ks[21], (DEPTH, N_GROUPS, D, GROUP_HIDDEN), D ** -0.5),
        'moe_w2': nrm(ks[22], (DEPTH, N_GROUPS, GROUP_HIDDEN, D), EXPERT_HIDDEN ** -0.5),
        'final_g': 1.0 + nrm(ks[23], (D,), 0.02),
    }


def reference(x, c, ctx, c_ctx, ada_w, ada_b, norm1_g, norm2_g, w_in_e, conv_w, na_rpb, w_out_e,
              w_in_o, ret_decay_f, ret_decay_b, w_out_o, router_gw, router_gb, router_ew, router_eb,
              moe_w1, moe_w3, moe_w2, final_g):
    s_ctx = ctx
    for layer in range(DEPTH):
        with_ctx = layer < DEPTH - 1
        mod_x = (jax.nn.silu(c) @ ada_w[layer] + ada_b[layer])[:, None, :]
        mod_c = (jax.nn.silu(c_ctx) @ ada_w[layer] + ada_b[layer])[None, None, :]
        sh1, sc1, g1, sh2, sc2, g2 = jnp.split(mod_x, 6, axis=-1)
        csh1, csc1, cg1, csh2, csc2, cg2 = jnp.split(mod_c, 6, axis=-1)
        hx = modulate(rmsnorm(x, norm1_g[layer]), sh1, sc1)
        hc = modulate(rmsnorm(s_ctx, norm1_g[layer]), csh1, csc1)
        i = layer // 2
        if layer % 2 == 0:
            y, yc = even_mixer(hx, hc, w_in_e[i], conv_w[i], na_rpb[i], w_out_e[i], with_ctx)
        else:
            y, yc = odd_mixer(hx, hc, w_in_o[i], ret_decay_f[i], ret_decay_b[i], w_out_o[i], with_ctx)
        moe_w = (router_gw[layer], router_gb[layer], router_ew[layer], router_eb[layer],
                 moe_w1[layer], moe_w3[layer], moe_w2[layer])
        x = x + g1 * y
        x = x + g2 * hier_moe(modulate(rmsnorm(x, norm2_g[layer]), sh2, sc2), *moe_w)
        if with_ctx:
            s_ctx = s_ctx + cg1 * yc
            s_ctx = s_ctx + cg2 * hier_moe(modulate(rmsnorm(s_ctx, norm2_g[layer]), csh2, csc2), *moe_w)
    return rmsnorm(x, final_g)
```

```python
import functools
import math

import numpy as np
import jax
import jax.numpy as jnp
from jax import lax
from jax.experimental import pallas as pl
from jax.experimental.pallas import tpu as pltpu

F32 = jnp.float32
BF16 = jnp.bfloat16
HIGHEST = lax.Precision.HIGHEST

D = 1024
L = 4096
GRID_W = 64
GRID_H = L // GRID_W
CTX = 256
EPS = 1e-6

CONV_CH = 512
NA_HEADS = 8
NA_DH = 64
WIN_H = 8
WIN_W = 16
EVEN_IN = 3072

RET_HEADS = 4
RET_DK = 256
RET_DV = 512
RET_C = 128
ODD_IN = 6144
ROPE_BASE = 10000.0

N_GROUPS = 4
EPG = 8
EH = 256
GH = EPG * EH

NEG = -1e30

NA_QR = 4
NA_KR = NA_QR + WIN_H - 1
NA_NQ = NA_QR * GRID_W
NA_NK = NA_KR * GRID_W
NA_WS_MAX = GRID_H - NA_KR

TS = 512
TD = 512

VMEM_LIMIT = 56 * 1024 * 1024


def _cp(sem, vmem=VMEM_LIMIT):
    return pltpu.CompilerParams(dimension_semantics=sem, vmem_limit_bytes=vmem)


def _silu(v):
    return v * (1.0 / (1.0 + jnp.exp(-v)))


def _ada_kernel(c_ref, w_ref, b_ref, o_ref):
    s = _silu(c_ref[...])
    o_ref[...] = jnp.dot(s, w_ref[...], precision=HIGHEST, preferred_element_type=F32) + b_ref[...]


def _ada(c16, ada_w, ada_b):
    depth = ada_w.shape[0]
    tn = 1536
    return pl.pallas_call(
        _ada_kernel,
        grid=(depth, 6 * D // tn),
        in_specs=[
            pl.BlockSpec((16, D), lambda l, j: (0, 0)),
            pl.BlockSpec((None, D, tn), lambda l, j: (l, 0, j)),
            pl.BlockSpec((None, 1, tn), lambda l, j: (l, 0, j)),
        ],
        out_specs=pl.BlockSpec((None, 16, tn), lambda l, j: (l, 0, j)),
        out_shape=jax.ShapeDtypeStruct((depth, 16, 6 * D), F32),
        compiler_params=_cp(("arbitrary", "arbitrary")),
        name="ada_mod",
    )(c16, ada_w, ada_b.reshape(depth, 1, 6 * D))


def _inproj_kernel(*refs, rope, tn):
    if rope:
        x_ref, g_ref, sh_ref, sc_ref, w_ref, cos_ref, sin_ref, o_ref, h_scr = refs
    else:
        x_ref, g_ref, sh_ref, sc_ref, w_ref, o_ref, h_scr = refs
    j = pl.program_id(1)

    @pl.when(j == 0)
    def _():
        x = x_ref[...]
        ms = jnp.mean(x * x, axis=-1, keepdims=True)
        y = x * lax.rsqrt(ms + EPS) * g_ref[...]
        h_scr[...] = (y * (1.0 + sc_ref[...]) + sh_ref[...]).astype(BF16)

    acc = jnp.dot(h_scr[...], w_ref[...], preferred_element_type=F32)
    if not rope:
        o_ref[...] = acc.astype(o_ref.dtype)
        return

    @pl.when(j < 2)
    def _():
        cos = cos_ref[...]
        sin = sin_ref[...]
        scale = jnp.where(j == 1, RET_DK ** -0.5, 1.0).astype(F32)
        pieces = []
        for cblk in range(tn // 128):
            t = acc[:, cblk * 128:(cblk + 1) * 128]
            half = (cblk % 2) * 128
            r = pltpu.roll(t, 64, axis=1)
            pieces.append(t * cos[:, half:half + 128] + r * sin[:, half:half + 128])
        o_ref[...] = (jnp.concatenate(pieces, axis=1) * scale).astype(o_ref.dtype)

    @pl.when(j >= 2)
    def _():
        o_ref[...] = acc.astype(o_ref.dtype)


def _inproj(xa, gain, sh, sc, w, n_lat_rows, n_mod, tm, rope_tables=None):
    rows = xa.shape[0]
    n = w.shape[1]
    tn = 1024
    mod_idx = lambda i, j: (jnp.minimum((i * tm) // L, n_mod), 0, 0)
    in_specs = [
        pl.BlockSpec((tm, D), lambda i, j: (i, 0)),
        pl.BlockSpec((1, D), lambda i, j: (0, 0)),
        pl.BlockSpec((None, 1, D), mod_idx),
        pl.BlockSpec((None, 1, D), mod_idx),
        pl.BlockSpec((D, tn), lambda i, j: (0, j)),
    ]
    args = [xa, gain.reshape(1, D), sh, sc, w]
    rope = rope_tables is not None
    if rope:
        lat_tiles = n_lat_rows // tm
        per_seq = L // tm
        tab_idx = lambda i, j: (jnp.where(i < lat_tiles, i % per_seq, per_seq), 0)
        in_specs += [pl.BlockSpec((tm, RET_DK), tab_idx), pl.BlockSpec((tm, RET_DK), tab_idx)]
        args += list(rope_tables)
    return pl.pallas_call(
        functools.partial(_inproj_kernel, rope=rope, tn=tn),
        grid=(rows // tm, n // tn),
        in_specs=in_specs,
        out_specs=pl.BlockSpec((tm, tn), lambda i, j: (i, j)),
        out_shape=jax.ShapeDtypeStruct((rows, n), BF16),
        scratch_shapes=[pltpu.VMEM((tm, D), BF16)],
        compiler_params=_cp(("arbitrary", "arbitrary")),
        name="inproj_rope" if rope else "inproj",
    )(*args)


def _rope_tables(tm):
    pos = np.arange(L)
    half = RET_DK // 4
    freqs = ROPE_BASE ** (-np.arange(half, dtype=np.float64) / half)
    ang_r = (pos // GRID_W)[:, None] * freqs[None, :]
    ang_c = (pos % GRID_W)[:, None] * freqs[None, :]
    cos = np.concatenate([np.cos(ang_r), np.cos(ang_r), np.cos(ang_c), np.cos(ang_c)], axis=1)
    sin = np.concatenate([-np.sin(ang_r), np.sin(ang_r), -np.sin(ang_c), np.sin(ang_c)], axis=1)
    cos = np.concatenate([cos, np.ones((tm, RET_DK))], axis=0)
    sin = np.concatenate([sin, np.zeros((tm, RET_DK))], axis=0)
    return jnp.asarray(cos, F32), jnp.asarray(sin, F32)


def _conv_kernel(ab_ref, ac_ref, ax_ref, w_ref, o_ref, *, seq):
    u = ac_ref[...].astype(F32) * ax_ref[...].astype(F32)
    row = lax.broadcasted_iota(jnp.int32, u.shape, 0)
    up = jnp.where(row == 0, 0.0, pltpu.roll(u, 1, axis=0))
    un = jnp.where(row == seq - 1, 0.0, pltpu.roll(u, seq - 1, axis=0))
    w = w_ref[...]
    y = ab_ref[...].astype(F32) * (up * w[0:1, :] + u * w[1:2, :] + un * w[2:3, :])
    o_ref[...] = y.astype(o_ref.dtype)


def _conv(qkv, conv_w, seq, n_seq, row_blk0):
    nb = CONV_CH // 128
    return pl.pallas_call(
        functools.partial(_conv_kernel, seq=seq),
        grid=(n_seq, nb),
        in_specs=[
            pl.BlockSpec((seq, 128), lambda b, j: (row_blk0 + b, j)),
            pl.BlockSpec((seq, 128), lambda b, j: (row_blk0 + b, nb + j)),
            pl.BlockSpec((seq, 128), lambda b, j: (row_blk0 + b, 2 * nb + j)),
            pl.BlockSpec((3, 128), lambda b, j: (0, j)),
        ],
        out_specs=pl.BlockSpec((seq, 128), lambda b, j: (b, j)),
        out_shape=jax.ShapeDtypeStruct((n_seq * seq, CONV_CH), BF16),
        compiler_params=_cp(("arbitrary", "arbitrary")),
        name="short_conv",
    )(qkv, qkv, qkv, conv_w)


def _natten_bias(rpb):
    qc = np.arange(GRID_W)
    kc = np.arange(GRID_W)
    cs = np.clip(qc - WIN_W // 2, 0, GRID_W - WIN_W)
    col_ok = (kc[None, :] >= cs[:, None]) & (kc[None, :] < cs[:, None] + WIN_W)
    dx = np.clip(kc[None, :] - qc[:, None] + WIN_W - 1, 0, 2 * WIN_W - 2)
    tcol = jnp.where(col_ok[None, None], rpb.astype(F32)[:, :, dx], NEG)
    dys, oks = [], []
    for r0 in (0, 2 * NA_QR, GRID_H - NA_QR):
        ws = int(np.clip(r0 - WIN_H // 2, 0, NA_WS_MAX))
        qr = r0 + np.arange(NA_QR)
        kr = ws + np.arange(NA_KR)
        rs = np.clip(qr - WIN_H // 2, 0, GRID_H - WIN_H)
        oks.append((kr[None, :] >= rs[:, None]) & (kr[None, :] < rs[:, None] + WIN_H))
        dys.append(np.clip(kr[None, :] - qr[:, None] + WIN_H - 1, 0, 2 * WIN_H - 2))
    dy = np.stack(dys)
    ok = np.stack(oks)
    b = jnp.take(tcol, jnp.asarray(dy.reshape(-1)), axis=1)
    b = b.reshape(NA_HEADS, 3, NA_QR, NA_KR, GRID_W, GRID_W)
    b = jnp.where(ok[None, :, :, :, None, None], b, NEG)
    b = b.transpose(0, 1, 2, 4, 3, 5).reshape(NA_HEADS // 2, 2, 3, NA_NQ, NA_NK)
    return b.transpose(0, 2, 1, 3, 4)


def _attend(qa, kw, vw, kc, vc, bias):
    dn = (((1,), (1,)), ((), ()))
    s2 = lax.dot_general(qa, kc, dn, preferred_element_type=F32)
    m = jnp.max(s2, axis=1, keepdims=True)
    if kw is not None:
        s1 = lax.dot_general(qa, kw, dn, preferred_element_type=F32) + bias
        m = jnp.maximum(m, jnp.max(s1, axis=1, keepdims=True))
        p1 = jnp.exp(s1 - m)
    p2 = jnp.exp(s2 - m)
    den = jnp.sum(p2, axis=1, keepdims=True)
    o = jnp.dot(p2.astype(BF16), vc, preferred_element_type=F32)
    if kw is not None:
        den = den + jnp.sum(p1, axis=1, keepdims=True)
        o = o + jnp.dot(p1.astype(BF16), vw, preferred_element_type=F32)
    return o / den


def _natten_kernel(q_ref, k_ref, v_ref, kc_ref, vc_ref, bias_ref, o_ref):
    rb = pl.program_id(2)
    ws = jnp.clip(rb * NA_QR - WIN_H // 2, 0, NA_WS_MAX)
    start = pl.multiple_of(ws * GRID_W, GRID_W)
    kw = k_ref[pl.ds(start, NA_NK), :]
    vw = v_ref[pl.ds(start, NA_NK), :]
    q = q_ref[...] * (NA_DH ** -0.5)
    lane = lax.broadcasted_iota(jnp.int32, q.shape, 1)
    zero = jnp.zeros_like(q)
    o0 = _attend(jnp.where(lane < NA_DH, q, zero), kw, vw, kc_ref[...], vc_ref[...], bias_ref[0])
    o1 = _attend(jnp.where(lane >= NA_DH, q, zero), kw, vw, kc_ref[...], vc_ref[...], bias_ref[1])
    o_ref[...] = jnp.where(lane < NA_DH, o0, o1).astype(o_ref.dtype)


def _natten(qkv, bias, nb):
    n_rb = GRID_H // NA_QR
    qcol, kcol, vcol = 3 * CONV_CH // 128, 3 * CONV_CH // 128 + 4, 3 * CONV_CH // 128 + 8
    ctx_blk0 = nb * L // CTX
    cls = lambda r: jnp.where(r == 0, 0, jnp.where(r == n_rb - 1, 2, 1))
    return pl.pallas_call(
        _natten_kernel,
        grid=(nb, NA_HEADS // 2, n_rb),
        in_specs=[
            pl.BlockSpec((NA_NQ, 128), lambda b, h, r: (b * n_rb + r, qcol + h)),
            pl.BlockSpec((L, 128), lambda b, h, r: (b, kcol + h)),
            pl.BlockSpec((L, 128), lambda b, h, r: (b, vcol + h)),
            pl.BlockSpec((CTX, 128), lambda b, h, r: (ctx_blk0 + b, kcol + h)),
            pl.BlockSpec((CTX, 128), lambda b, h, r: (ctx_blk0 + b, vcol + h)),
            pl.BlockSpec((None, None, 2, NA_NQ, NA_NK), lambda b, h, r: (h, cls(r), 0, 0, 0)),
        ],
        out_specs=pl.BlockSpec((NA_NQ, 128), lambda b, h, r: (b * n_rb + r, h)),
        out_shape=jax.ShapeDtypeStruct((nb * L, NA_HEADS * NA_DH), BF16),
        compiler_params=_cp(("arbitrary", "arbitrary", "arbitrary")),
        name="natten",
    )(qkv, qkv, qkv, qkv, qkv, bias)


def _ctx_attn_kernel(q_ref, kc_ref, vc_ref, o_ref):
    q = q_ref[...] * (NA_DH ** -0.5)
    lane = lax.broadcasted_iota(jnp.int32, q.shape, 1)
    zero = jnp.zeros_like(q)
    o0 = _attend(jnp.where(lane < NA_DH, q, zero), None, None, kc_ref[...], vc_ref[...], None)
    o1 = _attend(jnp.where(lane >= NA_DH, q, zero), None, None, kc_ref[...], vc_ref[...], None)
    o_ref[...] = jnp.where(lane < NA_DH, o0, o1).astype(o_ref.dtype)


def _ctx_attn(qkv, nb):
    qcol, kcol, vcol = 3 * CONV_CH // 128, 3 * CONV_CH // 128 + 4, 3 * CONV_CH // 128 + 8
    ctx_blk0 = nb * L // CTX
    return pl.pallas_call(
        _ctx_attn_kernel,
        grid=(nb, NA_HEADS // 2),
        in_specs=[
            pl.BlockSpec((CTX, 128), lambda b, h: (ctx_blk0 + b, qcol + h)),
            pl.BlockSpec((CTX, 128), lambda b, h: (ctx_blk0 + b, kcol + h)),
            pl.BlockSpec((CTX, 128), lambda b, h: (ctx_blk0 + b, vcol + h)),
        ],
        out_specs=pl.BlockSpec((CTX, 128), lambda b, h: (b, h)),
        out_shape=jax.ShapeDtypeStruct((nb * CTX, NA_HEADS * NA_DH), BF16),
        compiler_params=_cp(("arbitrary", "arbitrary")),
        name="ctx_attn",
    )(qkv, qkv, qkv)


def _retention_kernel(lg_ref, q_ref, k_ref, v_ref, g_ref, kc_ref, vc_ref, o_ref, y_scr, st_scr):
    h = pl.program_id(1)
    n_chunks = L // RET_C
    pos_r = lax.broadcasted_iota(jnp.int32, (RET_C, 1), 0).astype(F32)
    rel = pos_r - lax.broadcasted_iota(jnp.int32, (1, RET_C), 1).astype(F32)
    dn_t = (((1,), (1,)), ((), ()))
    dn_k = (((0,), (0,)), ((), ()))

    def decays(lg, fwd):
        if fwd:
            d_intra = jnp.where(rel >= 0, jnp.exp(lg * jnp.maximum(rel, 0.0)), 0.0)
            d_q = jnp.exp(lg * (pos_r + 1.0))
            d_k = jnp.exp(lg * (RET_C - 1.0 - pos_r))
        else:
            d_intra = jnp.where(rel <= 0, jnp.exp(lg * jnp.maximum(-rel, 0.0)), 0.0)
            d_q = jnp.exp(lg * (RET_C - pos_r))
            d_k = jnp.exp(lg * pos_r)
        d_c = jnp.exp(jnp.full((1, 1), RET_C, F32) * lg)
        return d_intra, d_q, d_k, d_c

    def ctx_state(d_k, d_c, order):
        st_scr[...] = jnp.zeros_like(st_scr)
        for jc in order:
            kd = (kc_ref[jc * RET_C:(jc + 1) * RET_C, :].astype(F32) * d_k).astype(BF16)
            vc = vc_ref[jc * RET_C:(jc + 1) * RET_C, :]
            st_scr[...] = st_scr[...] * d_c + lax.dot_general(kd, vc, dn_k, preferred_element_type=F32)

    def chunk_out(c, d_intra, d_q, d_k, d_c):
        r0 = pl.multiple_of(c * RET_C, RET_C)
        q = q_ref[pl.ds(r0, RET_C), :]
        k = k_ref[pl.ds(r0, RET_C), :]
        v = v_ref[pl.ds(r0, RET_C), :]
        s = lax.dot_general(q, k, dn_t, preferred_element_type=F32) * d_intra
        st = st_scr[...]
        o = (jnp.dot(s.astype(BF16), v, preferred_element_type=F32)
             + jnp.dot(q, st.astype(BF16), preferred_element_type=F32) * d_q)
        kd = (k.astype(F32) * d_k).astype(BF16)
        st_scr[...] = st * d_c + lax.dot_general(kd, v, dn_k, preferred_element_type=F32)
        return r0, o

    dec_f = decays(lg_ref[0, h], True)
    ctx_state(dec_f[2], dec_f[3], range(CTX // RET_C))

    def fwd_step(c, carry):
        r0, o = chunk_out(c, *dec_f)
        y_scr[pl.ds(r0, RET_C), :] = o
        return carry

    lax.fori_loop(0, n_chunks, fwd_step, 0)

    dec_b = decays(lg_ref[1, h], False)
    ctx_state(dec_b[2], dec_b[3], range(CTX // RET_C - 1, -1, -1))

    def bwd_step(i, carry):
        r0, o = chunk_out(n_chunks - 1 - i, *dec_b)
        y = y_scr[pl.ds(r0, RET_C), :] + o
        y = y * lax.rsqrt(jnp.mean(y * y, axis=-1, keepdims=True) + EPS)
        g = g_ref[pl.ds(r0, RET_C), :].astype(F32)
        o_ref[pl.ds(r0, RET_C), :] = (_silu(g) * y).astype(o_ref.dtype)
        return carry

    lax.fori_loop(0, n_chunks, bwd_step, 0)


def _retention(qkvg, lg, nb):
    ctx_blk0 = nb * L // CTX
    kq = RET_HEADS
    vq = 2 * RET_HEADS * RET_DK // RET_DV
    gq = vq + RET_HEADS
    return pl.pallas_call(
        _retention_kernel,
        grid=(nb, RET_HEADS),
        in_specs=[
            pl.BlockSpec(memory_space=pltpu.SMEM),
            pl.BlockSpec((L, RET_DK), lambda b, h: (b, h)),
            pl.BlockSpec((L, RET_DK), lambda b, h: (b, kq + h)),
            pl.BlockSpec((L, RET_DV), lambda b, h: (b, vq + h)),
            pl.BlockSpec((L, RET_DV), lambda b, h: (b, gq + h)),
            pl.BlockSpec((CTX, RET_DK), lambda b, h: (ctx_blk0 + b, kq + h)),
            pl.BlockSpec((CTX, RET_DV), lambda b, h: (ctx_blk0 + b, vq + h)),
        ],
        out_specs=pl.BlockSpec((L, RET_DV), lambda b, h: (b, h)),
        out_shape=jax.ShapeDtypeStruct((nb * L, RET_HEADS * RET_DV), BF16),
        scratch_shapes=[pltpu.VMEM((L, RET_DV), F32), pltpu.VMEM((RET_DK, RET_DV), F32)],
        compiler_params=_cp(("arbitrary", "arbitrary")),
        name="retention",
    )(lg, qkvg, qkvg, qkvg, qkvg, qkvg, qkvg)


def _outproj_kernel(*refs, n_act):
    a_refs = refs[:n_act]
    w_refs = refs[n_act:2 * n_act]
    x_ref, g1_ref, sh2_ref, sc2_ref, n2_ref, wr_ref, br_ref, x1_ref, h2_ref, lg_ref = refs[2 * n_act:]
    y = jnp.dot(a_refs[0][...], w_refs[0][...], preferred_element_type=F32)
    for a_ref, w_ref in zip(a_refs[1:], w_refs[1:]):
        y = y + jnp.dot(a_ref[...], w_ref[...], preferred_element_type=F32)
    x1 = x_ref[...] + g1_ref[...] * y
    x1_ref[...] = x1
    ms = jnp.mean(x1 * x1, axis=-1, keepdims=True)
    h2 = x1 * lax.rsqrt(ms + EPS) * n2_ref[...] * (1.0 + sc2_ref[...]) + sh2_ref[...]
    h2_ref[...] = h2
    lg_ref[...] = jnp.dot(h2, wr_ref[...], precision=HIGHEST, preferred_element_type=F32) + br_ref[...]


def _outproj(acts, ws, xa, g1, sh2, sc2, n2g, wr, br, rows, n_mod, tm):
    n_act = len(acts)
    mod_idx = lambda i: (jnp.minimum((i * tm) // L, n_mod), 0, 0)
    in_specs = ([pl.BlockSpec((tm, a.shape[1]), lambda i: (i, 0)) for a in acts]
                + [pl.BlockSpec(w.shape, lambda i: (0, 0)) for w in ws]
                + [pl.BlockSpec((tm, D), lambda i: (i, 0)),
                   pl.BlockSpec((None, 1, D), mod_idx),
                   pl.BlockSpec((None, 1, D), mod_idx),
                   pl.BlockSpec((None, 1, D), mod_idx),
                   pl.BlockSpec((1, D), lambda i: (0, 0)),
                   pl.BlockSpec((D, 128), lambda i: (0, 0)),
                   pl.BlockSpec((1, 128), lambda i: (0, 0))])
    return pl.pallas_call(
        functools.partial(_outproj_kernel, n_act=n_act),
        grid=(rows // tm,),
        in_specs=in_specs,
        out_specs=[pl.BlockSpec((tm, D), lambda i: (i, 0)),
                   pl.BlockSpec((tm, D), lambda i: (i, 0)),
                   pl.BlockSpec((tm, 128), lambda i: (i, 0))],
        out_shape=[jax.ShapeDtypeStruct((rows, D), F32),
                   jax.ShapeDtypeStruct((rows, D), F32),
                   jax.ShapeDtypeStruct((rows, 128), F32)],
        compiler_params=_cp(("arbitrary",)),
        name="outproj_router",
    )(*acts, *ws, xa, g1, sh2, sc2, n2g.reshape(1, D), wr, br)


def _router_kernel(lg_ref, tri_ref, o_ref, cnt_ref, carry_scr):
    i = pl.program_id(0)

    @pl.when(i == 0)
    def _():
        carry_scr[...] = jnp.zeros_like(carry_scr)

    lg = lg_ref[...]
    lane = lax.broadcasted_iota(jnp.int32, lg.shape, 1)
    lane_f = lane.astype(F32)
    big = jnp.float32(1e9)

    gmask = lane < N_GROUPS
    gl = jnp.where(gmask, lg, NEG)
    gm = jnp.max(gl, axis=1, keepdims=True)
    gidx = jnp.min(jnp.where(gl == gm, lane_f, big), axis=1, keepdims=True)
    g_val = 1.0 / jnp.sum(jnp.where(gmask, jnp.exp(gl - gm), 0.0), axis=1, keepdims=True)

    base = 8.0 + 8.0 * gidx
    emask = (lane_f >= base) & (lane_f < base + EPG)
    el = jnp.where(emask, lg, NEG)
    m1 = jnp.max(el, axis=1, keepdims=True)
    i1 = jnp.min(jnp.where(el == m1, lane_f, big), axis=1, keepdims=True)
    el2 = jnp.where(lane_f == i1, NEG, el)
    m2 = jnp.max(el2, axis=1, keepdims=True)
    i2 = jnp.min(jnp.where(el2 == m2, lane_f, big), axis=1, keepdims=True)
    r = jnp.exp(m2 - m1)
    w1 = g_val / (1.0 + r)
    w2 = w1 * r
    gates = jnp.where(lane_f == i1 - base, w1, jnp.where(lane_f == i2 - base, w2, 0.0))

    onehot = jnp.where(lane_f == gidx, 1.0, 0.0)
    before = jnp.dot(tri_ref[...], onehot.astype(BF16), preferred_element_type=F32)
    carry = carry_scr[0:1, :]
    rank = jnp.sum(jnp.where(lane_f == gidx, before + carry, 0.0), axis=1, keepdims=True)
    carry_scr[0:1, :] = carry + jnp.sum(onehot, axis=0, keepdims=True)

    o_ref[...] = jnp.where(lane < EPG, gates, jnp.where(lane == 8, gidx, jnp.where(lane == 9, rank, 0.0)))
    cnt_ref[...] = carry_scr[...]


def _router(logits, tm):
    rows = logits.shape[0]
    tri = jnp.asarray(np.tril(np.ones((tm, tm), np.float32), -1), BF16)
    return pl.pallas_call(
        _router_kernel,
        grid=(rows // tm,),
        in_specs=[pl.BlockSpec((tm, 128), lambda i: (i, 0)),
                  pl.BlockSpec((tm, tm), lambda i: (0, 0))],
        out_specs=[pl.BlockSpec((tm, 128), lambda i: (i, 0)),
                   pl.BlockSpec((8, 128), lambda i: (0, 0))],
        out_shape=[jax.ShapeDtypeStruct((rows, 128), F32),
                   jax.ShapeDtypeStruct((8, 128), F32)],
        scratch_shapes=[pltpu.VMEM((8, 128), F32)],
        compiler_params=_cp(("arbitrary",)),
        name="router",
    )(logits, tri)


def _dispatch_kernel(slot_ref, pad_ref, h2_ref, gt_ref, xs_ref, gs_ref, zx_scr, zg_scr, sem):
    i = pl.program_id(0)
    n_pad = pad_ref.shape[1]

    @pl.when(i == 0)
    def _():
        zx_scr[...] = jnp.zeros_like(zx_scr)
        zg_scr[...] = jnp.zeros_like(zg_scr)

        def zbody(r, carry):
            s = pad_ref[0, r]
            pltpu.make_async_copy(zx_scr.at[pl.ds(0, 1)], xs_ref.at[pl.ds(s, 1)], sem.at[0]).start()
            pltpu.make_async_copy(zg_scr.at[pl.ds(0, 1)], gs_ref.at[pl.ds(s, 1)], sem.at[1]).start()
            return carry

        lax.fori_loop(0, n_pad, zbody, 0)

        def wbody(r, carry):
            pltpu.make_async_copy(zx_scr.at[pl.ds(0, 1)], xs_ref.at[pl.ds(0, 1)], sem.at[0]).wait()
            pltpu.make_async_copy(zg_scr.at[pl.ds(0, 1)], gs_ref.at[pl.ds(0, 1)], sem.at[1]).wait()
            return carry

        lax.fori_loop(0, n_pad, wbody, 0)

    def body(r, carry):
        s = slot_ref[0, r]
        pltpu.make_async_copy(h2_ref.at[pl.ds(r, 1)], xs_ref.at[pl.ds(s, 1)], sem.at[0]).start()
        pltpu.make_async_copy(gt_ref.at[pl.ds(r, 1)], gs_ref.at[pl.ds(s, 1)], sem.at[1]).start()
        return carry

    lax.fori_loop(0, TD, body, 0)

    def wait_body(r, carry):
        pltpu.make_async_copy(h2_ref.at[pl.ds(0, 1)], xs_ref.at[pl.ds(0, 1)], sem.at[0]).wait()
        pltpu.make_async_copy(gt_ref.at[pl.ds(0, 1)], gs_ref.at[pl.ds(0, 1)], sem.at[1]).wait()
        return carry

    lax.fori_loop(0, TD, wait_body, 0)


def _dispatch(slot, pad_slots, h2, gates, n_sorted):
    rows = h2.shape[0]
    n_pad = pad_slots.shape[0]
    return pl.pallas_call(
        _dispatch_kernel,
        grid=(rows // TD,),
        in_specs=[pl.BlockSpec((None, 1, TD), lambda i: (i, 0, 0), memory_space=pltpu.SMEM),
                  pl.BlockSpec((1, n_pad), lambda i: (0, 0), memory_space=pltpu.SMEM),
                  pl.BlockSpec((TD, D), lambda i: (i, 0)),
                  pl.BlockSpec((TD, 128), lambda i: (i, 0))],
        out_specs=[pl.BlockSpec(memory_space=pl.ANY), pl.BlockSpec(memory_space=pl.ANY)],
        out_shape=[jax.ShapeDtypeStruct((n_sorted, D), F32),
                   jax.ShapeDtypeStruct((n_sorted, 128), F32)],
        scratch_shapes=[pltpu.VMEM((8, D), F32), pltpu.VMEM((8, 128), F32),
                        pltpu.SemaphoreType.DMA((2,))],
        compiler_params=_cp(("arbitrary",)),
        name="moe_dispatch",
    )(slot.reshape(rows // TD, 1, TD), pad_slots.reshape(1, n_pad), h2, gates)


def _moe_kernel(tg_ref, na_ref, x_ref, g_ref, w1_ref, w3_ref, w2_ref, o_ref, hid_scr):
    i = pl.program_id(0)

    @pl.when(i < na_ref[0])
    def _():
        x = x_ref[...].astype(BF16)
        gt = g_ref[...]
        for e in range(EPG):
            cols = slice(e * EH, (e + 1) * EH)
            h1 = jnp.dot(x, w1_ref[:, cols], preferred_element_type=F32)
            h3 = jnp.dot(x, w3_ref[:, cols], preferred_element_type=F32)
            hid_scr[:, cols] = (_silu(h1) * h3 * gt[:, e:e + 1]).astype(BF16)
        o_ref[...] = jnp.dot(hid_scr[...], w2_ref[...], preferred_element_type=F32)

    @pl.when(i >= na_ref[0])
    def _():
        o_ref[...] = jnp.zeros_like(o_ref)


def _moe(tile_group, n_active, xs, gs, w1, w3, w2, n_tiles):
    row_idx = lambda i, tg, na: (jnp.minimum(i, na[0] - 1), 0)
    out_idx = lambda i, tg, na: (i, 0)
    w_idx = lambda i, tg, na: (tg[i], 0, 0)
    grid_spec = pltpu.PrefetchScalarGridSpec(
        num_scalar_prefetch=2,
        grid=(n_tiles,),
        in_specs=[pl.BlockSpec((TS, D), row_idx),
                  pl.BlockSpec((TS, 128), row_idx),
                  pl.BlockSpec((None, D, GH), w_idx),
                  pl.BlockSpec((None, D, GH), w_idx),
                  pl.BlockSpec((None, GH, D), w_idx)],
        out_specs=pl.BlockSpec((TS, D), out_idx),
        scratch_shapes=[pltpu.VMEM((TS, GH), BF16)],
    )
    return pl.pallas_call(
        _moe_kernel,
        grid_spec=grid_spec,
        out_shape=jax.ShapeDtypeStruct((xs.shape[0], D), F32),
        compiler_params=_cp(("arbitrary",)),
        name="moe_experts",
    )(tile_group, n_active, xs, gs, w1, w3, w2)


def _combine_kernel(*refs, final):
    if final:
        slot_ref, x1_ref, g2_ref, fg_ref, ys_ref, o_ref, buf, sem = refs
    else:
        slot_ref, x1_ref, g2_ref, ys_ref, o_ref, buf, sem = refs

    def body(r, carry):
        s = slot_ref[0, r]
        pltpu.make_async_copy(ys_ref.at[pl.ds(s, 1)], buf.at[pl.ds(r, 1)], sem.at[0]).start()
        return carry

    lax.fori_loop(0, TD, body, 0)

    def wait_body(r, carry):
        pltpu.make_async_copy(ys_ref.at[pl.ds(0, 1)], buf.at[pl.ds(0, 1)], sem.at[0]).wait()
        return carry

    lax.fori_loop(0, TD, wait_body, 0)

    x2 = x1_ref[...] + g2_ref[...] * buf[...]
    if final:
        ms = jnp.mean(x2 * x2, axis=-1, keepdims=True)
        x2 = x2 * lax.rsqrt(ms + EPS) * fg_ref[...]
    o_ref[...] = x2


def _combine(slot, x1, g2, ys, n_mod, final_g=None):
    rows = x1.shape[0]
    final = final_g is not None
    mod_idx = lambda i: (jnp.minimum((i * TD) // L, n_mod), 0, 0)
    in_specs = [pl.BlockSpec((None, 1, TD), lambda i: (i, 0, 0), memory_space=pltpu.SMEM),
                pl.BlockSpec((TD, D), lambda i: (i, 0)),
                pl.BlockSpec((None, 1, D), mod_idx)]
    args = [slot.reshape(rows // TD, 1, TD), x1, g2]
    if final:
        in_specs.append(pl.BlockSpec((1, D), lambda i: (0, 0)))
        args.append(final_g.reshape(1, D))
    in_specs.append(pl.BlockSpec(memory_space=pl.ANY))
    args.append(ys)
    return pl.pallas_call(
        functools.partial(_combine_kernel, final=final),
        grid=(rows // TD,),
        in_specs=in_specs,
        out_specs=pl.BlockSpec((TD, D), lambda i: (i, 0)),
        out_shape=jax.ShapeDtypeStruct((rows, D), F32),
        scratch_shapes=[pltpu.VMEM((TD, D), F32), pltpu.SemaphoreType.DMA((1,))],
        compiler_params=_cp(("arbitrary",)),
        name="moe_combine_final" if final else "moe_combine",
    )(*args)


def _sort_plan(route, counts_row, rows):
    gidx = route[:, 8].astype(jnp.int32)
    rank = route[:, 9].astype(jnp.int32)
    counts = counts_row[:N_GROUPS].astype(jnp.int32)
    padded = ((counts + TS - 1) // TS) * TS
    ends = jnp.cumsum(padded)
    starts = ends - padded
    slot = rank
    for g in range(N_GROUPS):
        slot = slot + jnp.where(gidx == g, starts[g], 0)
    n_tiles = rows // TS + N_GROUPS
    n_active = (ends[-1] // TS).reshape(1)
    tile_start = jnp.arange(n_tiles, dtype=jnp.int32) * TS
    tile_group = jnp.minimum(jnp.sum(tile_start[:, None] >= ends[None, :], axis=1), N_GROUPS - 1).astype(jnp.int32)
    r = jnp.arange(TS, dtype=jnp.int32)
    pad_real = ((starts + counts)[:, None] + r[None, :]).reshape(-1)
    is_pad = pad_real < jnp.repeat(ends, TS)
    tail = ends[-1] + jnp.cumsum(jnp.logical_not(is_pad).astype(jnp.int32)) - 1
    pad_slots = jnp.where(is_pad, pad_real, tail)
    return slot, pad_slots, tile_group, n_active, n_tiles, n_tiles * TS


def _moe_block(x1, h2, logits, g2, w1, w3, w2, n_mod, tm, final_g=None):
    rows = x1.shape[0]
    route, cnt = _router(logits, tm)
    slot, pad_slots, tile_group, n_active, n_tiles, n_sorted = _sort_plan(route, cnt[0], rows)
    xs, gs = _dispatch(slot, pad_slots, h2, route, n_sorted)
    ys = _moe(tile_group, n_active, xs, gs, w1, w3, w2, n_tiles)
    return _combine(slot, x1, g2, ys, n_mod, final_g)


def _router_weights(gw, gb, ew, eb):
    wr = jnp.zeros((D, 128), F32).at[:, :N_GROUPS].set(gw).at[:, 8:8 + N_GROUPS * EPG].set(ew)
    br = jnp.zeros((1, 128), F32).at[0, :N_GROUPS].set(gb).at[0, 8:8 + N_GROUPS * EPG].set(eb)
    return wr, br


def kernel(x, c, ctx, c_ctx, ada_w, ada_b, norm1_g, norm2_g, w_in_e, conv_w, na_rpb, w_out_e, w_in_o,
           ret_decay_f, ret_decay_b, w_out_o, router_gw, router_gb, router_ew, router_eb,
           moe_w1, moe_w3, moe_w2, final_g):
    nb = x.shape[0]
    assert x.shape[1:] == (L, D) and ctx.shape[1:] == (CTX, D) and nb < 16 and ada_w.shape[0] == 2
    t_lat = nb * L
    t_all = t_lat + nb * CTX
    tm = 1024 if t_all % 1024 == 0 else 512

    xa = jnp.concatenate([x.reshape(t_lat, D), ctx.reshape(nb * CTX, D)], axis=0)
    c16 = jnp.zeros((16, D), F32).at[:nb].set(c).at[nb].set(c_ctx)
    mod = _ada(c16, ada_w, ada_b).reshape(2, 16, 6, 1, D)
    mods = [[mod[l, :, k] for k in range(6)] for l in range(2)]

    sh1, sc1, g1, sh2, sc2, g2 = mods[0]
    qkv = _inproj(xa, norm1_g[0], sh1, sc1, w_in_e[0].astype(BF16), t_lat, nb, tm)
    y_conv = jnp.concatenate([_conv(qkv, conv_w[0], L, nb, 0),
                              _conv(qkv, conv_w[0], CTX, nb, t_lat // CTX)], axis=0)
    y_attn = jnp.concatenate([_natten(qkv, _natten_bias(na_rpb[0]), nb), _ctx_attn(qkv, nb)], axis=0)
    w_out = w_out_e[0].astype(BF16)
    wr, br = _router_weights(router_gw[0], router_gb[0], router_ew[0], router_eb[0])
    x1, h2, logits = _outproj([y_conv, y_attn], [w_out[:CONV_CH], w_out[CONV_CH:]], xa, g1, sh2, sc2,
                              norm2_g[0], wr, br, t_all, nb, tm)
    xa = _moe_block(x1, h2, logits, g2, moe_w1[0].astype(BF16), moe_w3[0].astype(BF16),
                    moe_w2[0].astype(BF16), nb, tm)

    sh1, sc1, g1, sh2, sc2, g2 = mods[1]
    qkvg = _inproj(xa, norm1_g[1], sh1, sc1, w_in_o[0].astype(BF16), t_lat, nb, tm, _rope_tables(tm))
    lg = jnp.stack([jnp.log1p(-jnp.exp(ret_decay_f[0].astype(F32))),
                    jnp.log1p(-jnp.exp(ret_decay_b[0].astype(F32)))])
    y_ret = _retention(qkvg, lg, nb)
    wr, br = _router_weights(router_gw[1], router_gb[1], router_ew[1], router_eb[1])
    tm1 = 1024 if t_lat % 1024 == 0 else 512
    x1, h2, logits = _outproj([y_ret], [w_out_o[0].astype(BF16)], xa, g1, sh2, sc2, norm2_g[1], wr, br,
                              t_lat, nb, tm1)
    out = _moe_block(x1, h2, logits, g2, moe_w1[1].astype(BF16), moe_w3[1].astype(BF16),
                     moe_w2[1].astype(BF16), nb, tm1, final_g)
    return out.reshape(nb, L, D)
```

```python
import functools
import math

import numpy as np
import jax
import jax.numpy as jnp
from jax import lax
from jax.experimental import pallas as pl
from jax.experimental.pallas import tpu as pltpu

F32 = jnp.float32
BF16 = jnp.bfloat16
HIGHEST = lax.Precision.HIGHEST

D = 1024
L = 4096
GRID_W = 64
GRID_H = L // GRID_W
CTX = 256
EPS = 1e-6

CONV_CH = 512
NA_HEADS = 8
NA_DH = 64
WIN_H = 8
WIN_W = 16
EVEN_IN = 3072

RET_HEADS = 4
RET_DK = 256
RET_DV = 512
RET_C = 128
ODD_IN = 6144
ROPE_BASE = 10000.0

N_GROUPS = 4
EPG = 8
EH = 256
GH = EPG * EH

NEG = -1e30

NA_QR = 4
NA_KR = NA_QR + WIN_H - 1
NA_NQ = NA_QR * GRID_W
NA_NK = NA_KR * GRID_W
NA_WS_MAX = GRID_H - NA_KR

TS = 512
TD = 512

VMEM_LIMIT = 56 * 1024 * 1024


def _cp(sem, vmem=VMEM_LIMIT):
    return pltpu.CompilerParams(dimension_semantics=sem, vmem_limit_bytes=vmem)


def _silu(v):
    return v * (1.0 / (1.0 + jnp.exp(-v)))


def _src_specs(src, tm, lat_tiles):
    tail = tuple(src[0].shape[1:])
    zeros = (0,) * len(tail)
    if len(src) == 1:
        return [pl.BlockSpec((tm,) + tail, lambda i, *_: (i,) + zeros)]
    return [pl.BlockSpec((tm,) + tail, lambda i, *_: (jnp.minimum(i, lat_tiles - 1),) + zeros),
            pl.BlockSpec((tm,) + tail, lambda i, *_: (jnp.maximum(i - lat_tiles, 0),) + zeros)]


def _src_load(refs, i, lat_tiles):
    if len(refs) == 1:
        return refs[0][...]
    return jnp.where(i < lat_tiles, refs[0][...], refs[1][...])


def _ada_kernel(c_ref, w_ref, b_ref, o_ref):
    s = _silu(c_ref[...])
    o_ref[...] = jnp.dot(s, w_ref[...], precision=HIGHEST, preferred_element_type=F32) + b_ref[...]


def _ada(c16, ada_w, ada_b):
    depth = ada_w.shape[0]
    tn = 1536
    return pl.pallas_call(
        _ada_kernel,
        grid=(depth, 6 * D // tn),
        in_specs=[
            pl.BlockSpec((16, D), lambda l, j: (0, 0)),
            pl.BlockSpec((None, D, tn), lambda l, j: (l, 0, j)),
            pl.BlockSpec((None, 1, tn), lambda l, j: (l, 0, j)),
        ],
        out_specs=pl.BlockSpec((None, 16, tn), lambda l, j: (l, 0, j)),
        out_shape=jax.ShapeDtypeStruct((depth, 16, 6 * D), F32),
        compiler_params=_cp(("arbitrary", "arbitrary")),
        name="ada_mod",
    )(c16, ada_w, ada_b.reshape(depth, 1, 6 * D))


def _inproj_kernel(*refs, rope, tn, n_src, lat_tiles):
    x_refs, refs = refs[:n_src], refs[n_src:]
    if rope:
        g_ref, sh_ref, sc_ref, w_ref, cos_ref, sin_ref, o_ref, h_scr = refs
    else:
        g_ref, sh_ref, sc_ref, w_ref, o_ref, h_scr = refs
    i = pl.program_id(0)
    j = pl.program_id(1)

    @pl.when(j == 0)
    def _():
        x = _src_load(x_refs, i, lat_tiles)
        ms = jnp.mean(x * x, axis=-1, keepdims=True)
        y = x * lax.rsqrt(ms + EPS) * g_ref[...]
        h_scr[...] = (y * (1.0 + sc_ref[...]) + sh_ref[...]).astype(BF16)

    acc = jnp.dot(h_scr[...], w_ref[...], preferred_element_type=F32)
    if not rope:
        o_ref[...] = acc.astype(o_ref.dtype)
        return

    @pl.when(j < 2)
    def _():
        cos = cos_ref[...]
        sin = sin_ref[...]
        scale = jnp.where(j == 1, RET_DK ** -0.5, 1.0).astype(F32)
        pieces = []
        for cblk in range(tn // 128):
            t = acc[:, cblk * 128:(cblk + 1) * 128]
            half = (cblk % 2) * 128
            r = pltpu.roll(t, 64, axis=1)
            pieces.append(t * cos[:, half:half + 128] + r * sin[:, half:half + 128])
        o_ref[...] = (jnp.concatenate(pieces, axis=1) * scale).astype(o_ref.dtype)

    @pl.when(j >= 2)
    def _():
        o_ref[...] = acc.astype(o_ref.dtype)


def _inproj(xsrc, gain, sh, sc, w, n_lat_rows, n_mod, tm, rope_tables=None):
    rows = sum(a.shape[0] for a in xsrc)
    n = w.shape[1]
    tn = 1024
    lat_tiles = n_lat_rows // tm
    mod_idx = lambda i, j: (jnp.minimum((i * tm) // L, n_mod), 0, 0)
    in_specs = _src_specs(xsrc, tm, lat_tiles) + [
        pl.BlockSpec((1, D), lambda i, j: (0, 0)),
        pl.BlockSpec((None, 1, D), mod_idx),
        pl.BlockSpec((None, 1, D), mod_idx),
        pl.BlockSpec((D, tn), lambda i, j: (0, j)),
    ]
    args = list(xsrc) + [gain.reshape(1, D), sh, sc, w]
    rope = rope_tables is not None
    if rope:
        per_seq = L // tm
        tab_idx = lambda i, j: (jnp.where(i < lat_tiles, i % per_seq, per_seq), 0)
        in_specs += [pl.BlockSpec((tm, RET_DK), tab_idx), pl.BlockSpec((tm, RET_DK), tab_idx)]
        args += list(rope_tables)
    return pl.pallas_call(
        functools.partial(_inproj_kernel, rope=rope, tn=tn, n_src=len(xsrc), lat_tiles=lat_tiles),
        grid=(rows // tm, n // tn),
        in_specs=in_specs,
        out_specs=pl.BlockSpec((tm, tn), lambda i, j: (i, j)),
        out_shape=jax.ShapeDtypeStruct((rows, n), BF16),
        scratch_shapes=[pltpu.VMEM((tm, D), BF16)],
        compiler_params=_cp(("arbitrary", "arbitrary")),
        name="inproj_rope" if rope else "inproj",
    )(*args)


def _rope_tables(tm):
    pos = np.arange(L)
    half = RET_DK // 4
    freqs = ROPE_BASE ** (-np.arange(half, dtype=np.float64) / half)
    ang_r = (pos // GRID_W)[:, None] * freqs[None, :]
    ang_c = (pos % GRID_W)[:, None] * freqs[None, :]
    cos = np.concatenate([np.cos(ang_r), np.cos(ang_r), np.cos(ang_c), np.cos(ang_c)], axis=1)
    sin = np.concatenate([-np.sin(ang_r), np.sin(ang_r), -np.sin(ang_c), np.sin(ang_c)], axis=1)
    cos = np.concatenate([cos, np.ones((tm, RET_DK))], axis=0)
    sin = np.concatenate([sin, np.zeros((tm, RET_DK))], axis=0)
    return jnp.asarray(cos, F32), jnp.asarray(sin, F32)


def _conv_kernel(ab_ref, ac_ref, ax_ref, w_ref, o_ref, *, seq):
    u = ac_ref[...].astype(F32) * ax_ref[...].astype(F32)
    row = lax.broadcasted_iota(jnp.int32, u.shape, 0)
    up = jnp.where(row == 0, 0.0, pltpu.roll(u, 1, axis=0))
    un = jnp.where(row == seq - 1, 0.0, pltpu.roll(u, seq - 1, axis=0))
    w = w_ref[...]
    y = ab_ref[...].astype(F32) * (up * w[0:1, :] + u * w[1:2, :] + un * w[2:3, :])
    o_ref[...] = y.astype(o_ref.dtype)


def _conv(qkv, conv_w, seq, n_seq, row_blk0):
    nb = CONV_CH // 128
    return pl.pallas_call(
        functools.partial(_conv_kernel, seq=seq),
        grid=(n_seq, nb),
        in_specs=[
            pl.BlockSpec((seq, 128), lambda b, j: (row_blk0 + b, j)),
            pl.BlockSpec((seq, 128), lambda b, j: (row_blk0 + b, nb + j)),
            pl.BlockSpec((seq, 128), lambda b, j: (row_blk0 + b, 2 * nb + j)),
            pl.BlockSpec((3, 128), lambda b, j: (0, j)),
        ],
        out_specs=pl.BlockSpec((seq, 128), lambda b, j: (b, j)),
        out_shape=jax.ShapeDtypeStruct((n_seq * seq, CONV_CH), BF16),
        compiler_params=_cp(("arbitrary", "arbitrary")),
        name="short_conv",
    )(qkv, qkv, qkv, conv_w)


def _natten_bias(rpb):
    qc = np.arange(GRID_W)
    kc = np.arange(GRID_W)
    cs = np.clip(qc - WIN_W // 2, 0, GRID_W - WIN_W)
    col_ok = (kc[None, :] >= cs[:, None]) & (kc[None, :] < cs[:, None] + WIN_W)
    dx = np.clip(kc[None, :] - qc[:, None] + WIN_W - 1, 0, 2 * WIN_W - 2)
    tcol = jnp.where(col_ok[None, None], rpb.astype(F32)[:, :, dx], NEG)
    dys, oks = [], []
    for r0 in (0, 2 * NA_QR, GRID_H - NA_QR):
        ws = int(np.clip(r0 - WIN_H // 2, 0, NA_WS_MAX))
        qr = r0 + np.arange(NA_QR)
        kr = ws + np.arange(NA_KR)
        rs = np.clip(qr - WIN_H // 2, 0, GRID_H - WIN_H)
        oks.append((kr[None, :] >= rs[:, None]) & (kr[None, :] < rs[:, None] + WIN_H))
        dys.append(np.clip(kr[None, :] - qr[:, None] + WIN_H - 1, 0, 2 * WIN_H - 2))
    dy = np.stack(dys)
    ok = np.stack(oks)
    b = jnp.take(tcol, jnp.asarray(dy.reshape(-1)), axis=1)
    b = b.reshape(NA_HEADS, 3, NA_QR, NA_KR, GRID_W, GRID_W)
    b = jnp.where(ok[None, :, :, :, None, None], b, NEG)
    b = b.transpose(0, 1, 2, 4, 3, 5).reshape(NA_HEADS // 2, 2, 3, NA_NQ, NA_NK)
    return b.transpose(0, 2, 1, 3, 4)


def _attend(qa, kw, vw, kc, vc, bias):
    dn = (((1,), (1,)), ((), ()))
    s2 = lax.dot_general(qa, kc, dn, preferred_element_type=F32)
    m = jnp.max(s2, axis=1, keepdims=True)
    if kw is not None:
        s1 = lax.dot_general(qa, kw, dn, preferred_element_type=F32) + bias
        m = jnp.maximum(m, jnp.max(s1, axis=1, keepdims=True))
        p1 = jnp.exp(s1 - m)
    p2 = jnp.exp(s2 - m)
    den = jnp.sum(p2, axis=1, keepdims=True)
    o = jnp.dot(p2.astype(BF16), vc, preferred_element_type=F32)
    if kw is not None:
        den = den + jnp.sum(p1, axis=1, keepdims=True)
        o = o + jnp.dot(p1.astype(BF16), vw, preferred_element_type=F32)
    return o / den


def _natten_kernel(q_ref, k_ref, v_ref, kc_ref, vc_ref, bias_ref, o_ref):
    rb = pl.program_id(2)
    ws = jnp.clip(rb * NA_QR - WIN_H // 2, 0, NA_WS_MAX)
    start = pl.multiple_of(ws * GRID_W, GRID_W)
    kw = k_ref[pl.ds(start, NA_NK), :]
    vw = v_ref[pl.ds(start, NA_NK), :]
    q = q_ref[...] * (NA_DH ** -0.5)
    lane = lax.broadcasted_iota(jnp.int32, q.shape, 1)
    zero = jnp.zeros_like(q)
    o0 = _attend(jnp.where(lane < NA_DH, q, zero), kw, vw, kc_ref[...], vc_ref[...], bias_ref[0])
    o1 = _attend(jnp.where(lane >= NA_DH, q, zero), kw, vw, kc_ref[...], vc_ref[...], bias_ref[1])
    o_ref[...] = jnp.where(lane < NA_DH, o0, o1).astype(o_ref.dtype)


def _natten(qkv, bias, nb):
    n_rb = GRID_H // NA_QR
    qcol, kcol, vcol = 3 * CONV_CH // 128, 3 * CONV_CH // 128 + 4, 3 * CONV_CH // 128 + 8
    ctx_blk0 = nb * L // CTX
    cls = lambda r: jnp.where(r == 0, 0, jnp.where(r == n_rb - 1, 2, 1))
    return pl.pallas_call(
        _natten_kernel,
        grid=(nb, NA_HEADS // 2, n_rb),
        in_specs=[
            pl.BlockSpec((NA_NQ, 128), lambda b, h, r: (b * n_rb + r, qcol + h)),
            pl.BlockSpec((L, 128), lambda b, h, r: (b, kcol + h)),
            pl.BlockSpec((L, 128), lambda b, h, r: (b, vcol + h)),
            pl.BlockSpec((CTX, 128), lambda b, h, r: (ctx_blk0 + b, kcol + h)),
            pl.BlockSpec((CTX, 128), lambda b, h, r: (ctx_blk0 + b, vcol + h)),
            pl.BlockSpec((None, None, 2, NA_NQ, NA_NK), lambda b, h, r: (h, cls(r), 0, 0, 0)),
        ],
        out_specs=pl.BlockSpec((NA_NQ, 128), lambda b, h, r: (b * n_rb + r, h)),
        out_shape=jax.ShapeDtypeStruct((nb * L, NA_HEADS * NA_DH), BF16),
        compiler_params=_cp(("arbitrary", "arbitrary", "arbitrary")),
        name="natten",
    )(qkv, qkv, qkv, qkv, qkv, bias)


def _ctx_attn_kernel(q_ref, kc_ref, vc_ref, o_ref):
    q = q_ref[...] * (NA_DH ** -0.5)
    lane = lax.broadcasted_iota(jnp.int32, q.shape, 1)
    zero = jnp.zeros_like(q)
    o0 = _attend(jnp.where(lane < NA_DH, q, zero), None, None, kc_ref[...], vc_ref[...], None)
    o1 = _attend(jnp.where(lane >= NA_DH, q, zero), None, None, kc_ref[...], vc_ref[...], None)
    o_ref[...] = jnp.where(lane < NA_DH, o0, o1).astype(o_ref.dtype)


def _ctx_attn(qkv, nb):
    qcol, kcol, vcol = 3 * CONV_CH // 128, 3 * CONV_CH // 128 + 4, 3 * CONV_CH // 128 + 8
    ctx_blk0 = nb * L // CTX
    return pl.pallas_call(
        _ctx_attn_kernel,
        grid=(nb, NA_HEADS // 2),
        in_specs=[
            pl.BlockSpec((CTX, 128), lambda b, h: (ctx_blk0 + b, qcol + h)),
            pl.BlockSpec((CTX, 128), lambda b, h: (ctx_blk0 + b, kcol + h)),
            pl.BlockSpec((CTX, 128), lambda b, h: (ctx_blk0 + b, vcol + h)),
        ],
        out_specs=pl.BlockSpec((CTX, 128), lambda b, h: (b, h)),
        out_shape=jax.ShapeDtypeStruct((nb * CTX, NA_HEADS * NA_DH), BF16),
        compiler_params=_cp(("arbitrary", "arbitrary")),
        name="ctx_attn",
    )(qkv, qkv, qkv)


def _retention_kernel(lg_ref, q_ref, k_ref, v_ref, g_ref, kc_ref, vc_ref, o_ref, y_scr, stf_scr, stb_scr):
    h = pl.program_id(1)
    n_chunks = L // RET_C
    pos_r = lax.broadcasted_iota(jnp.int32, (RET_C, 1), 0).astype(F32)
    rel = pos_r - lax.broadcasted_iota(jnp.int32, (1, RET_C), 1).astype(F32)
    dn_t = (((1,), (1,)), ((), ()))
    dn_k = (((0,), (0,)), ((), ()))

    def decays(lg, fwd):
        if fwd:
            d_intra = jnp.where(rel >= 0, jnp.exp(lg * jnp.maximum(rel, 0.0)), 0.0)
            d_q = jnp.exp(lg * (pos_r + 1.0))
            d_k = jnp.exp(lg * (RET_C - 1.0 - pos_r))
        else:
            d_intra = jnp.where(rel <= 0, jnp.exp(lg * jnp.maximum(-rel, 0.0)), 0.0)
            d_q = jnp.exp(lg * (RET_C - pos_r))
            d_k = jnp.exp(lg * pos_r)
        d_c = jnp.exp(jnp.full((1, 1), RET_C, F32) * lg)
        return d_intra, d_q, d_k, d_c

    def ctx_state(st_scr, d_k, d_c, order):
        st_scr[...] = jnp.zeros_like(st_scr)
        for jc in order:
            kd = (kc_ref[jc * RET_C:(jc + 1) * RET_C, :].astype(F32) * d_k).astype(BF16)
            vc = vc_ref[jc * RET_C:(jc + 1) * RET_C, :]
            st_scr[...] = st_scr[...] * d_c + lax.dot_general(kd, vc, dn_k, preferred_element_type=F32)

    def chunk_out(st_scr, c, d_intra, d_q, d_k, d_c):
        r0 = pl.multiple_of(c * RET_C, RET_C)
        q = q_ref[pl.ds(r0, RET_C), :]
        k = k_ref[pl.ds(r0, RET_C), :]
        v = v_ref[pl.ds(r0, RET_C), :]
        s = lax.dot_general(q, k, dn_t, preferred_element_type=F32) * d_intra
        st = st_scr[...]
        o = (jnp.dot(s.astype(BF16), v, preferred_element_type=F32)
             + jnp.dot(q, st.astype(BF16), preferred_element_type=F32) * d_q)
        kd = (k.astype(F32) * d_k).astype(BF16)
        st_scr[...] = st * d_c + lax.dot_general(kd, v, dn_k, preferred_element_type=F32)
        return r0, o

    def finish(r0, y):
        y = y * lax.rsqrt(jnp.mean(y * y, axis=-1, keepdims=True) + EPS)
        g = g_ref[pl.ds(r0, RET_C), :].astype(F32)
        o_ref[pl.ds(r0, RET_C), :] = (_silu(g) * y).astype(o_ref.dtype)

    dec_f = decays(lg_ref[0, h], True)
    dec_b = decays(lg_ref[1, h], False)
    ctx_state(stf_scr, dec_f[2], dec_f[3], range(CTX // RET_C))
    ctx_state(stb_scr, dec_b[2], dec_b[3], range(CTX // RET_C - 1, -1, -1))

    def first_half(i, carry):
        r0, o = chunk_out(stf_scr, i, *dec_f)
        y_scr[pl.ds(r0, RET_C), :] = o
        r1, o = chunk_out(stb_scr, n_chunks - 1 - i, *dec_b)
        y_scr[pl.ds(r1, RET_C), :] = o
        return carry

    lax.fori_loop(0, n_chunks // 2, first_half, 0)

    def second_half(i, carry):
        r0, o = chunk_out(stf_scr, i, *dec_f)
        finish(r0, y_scr[pl.ds(r0, RET_C), :] + o)
        r1, o = chunk_out(stb_scr, n_chunks - 1 - i, *dec_b)
        finish(r1, y_scr[pl.ds(r1, RET_C), :] + o)
        return carry

    lax.fori_loop(n_chunks // 2, n_chunks, second_half, 0)


def _retention(qkvg, lg, nb):
    ctx_blk0 = nb * L // CTX
    kq = RET_HEADS
    vq = 2 * RET_HEADS * RET_DK // RET_DV
    gq = vq + RET_HEADS
    return pl.pallas_call(
        _retention_kernel,
        grid=(nb, RET_HEADS),
        in_specs=[
            pl.BlockSpec(memory_space=pltpu.SMEM),
            pl.BlockSpec((L, RET_DK), lambda b, h: (b, h)),
            pl.BlockSpec((L, RET_DK), lambda b, h: (b, kq + h)),
            pl.BlockSpec((L, RET_DV), lambda b, h: (b, vq + h)),
            pl.BlockSpec((L, RET_DV), lambda b, h: (b, gq + h)),
            pl.BlockSpec((CTX, RET_DK), lambda b, h: (ctx_blk0 + b, kq + h)),
            pl.BlockSpec((CTX, RET_DV), lambda b, h: (ctx_blk0 + b, vq + h)),
        ],
        out_specs=pl.BlockSpec((L, RET_DV), lambda b, h: (b, h)),
        out_shape=jax.ShapeDtypeStruct((nb * L, RET_HEADS * RET_DV), BF16),
        scratch_shapes=[pltpu.VMEM((L, RET_DV), F32), pltpu.VMEM((RET_DK, RET_DV), F32),
                        pltpu.VMEM((RET_DK, RET_DV), F32)],
        compiler_params=_cp(("arbitrary", "arbitrary")),
        name="retention",
    )(lg, qkvg, qkvg, qkvg, qkvg, qkvg, qkvg)


def _outproj_kernel(*refs, src_lens, lat_tiles):
    i = pl.program_id(0)
    srcs = []
    for n in src_lens:
        srcs.append(refs[:n])
        refs = refs[n:]
    n_act = len(src_lens) - 1
    w_refs, refs = refs[:n_act], refs[n_act:]
    g1_ref, sh2_ref, sc2_ref, n2_ref, wr_ref, br_ref, x1_ref, h2_ref, lg_ref = refs
    y = None
    for a_src, w_ref in zip(srcs[:n_act], w_refs):
        t = jnp.dot(_src_load(a_src, i, lat_tiles), w_ref[...], preferred_element_type=F32)
        y = t if y is None else y + t
    x1 = _src_load(srcs[n_act], i, lat_tiles) + g1_ref[...] * y
    x1_ref[...] = x1
    ms = jnp.mean(x1 * x1, axis=-1, keepdims=True)
    h2 = x1 * lax.rsqrt(ms + EPS) * n2_ref[...] * (1.0 + sc2_ref[...]) + sh2_ref[...]
    for k in range(D // 128):
        h2_ref[:, k, :] = h2[:, k * 128:(k + 1) * 128]
    h_hi = h2.astype(BF16)
    h_lo = (h2 - h_hi.astype(F32)).astype(BF16)
    p = jnp.dot(h_hi, wr_ref[...], preferred_element_type=F32)
    q = jnp.dot(h_lo, wr_ref[:, :128], preferred_element_type=F32)
    lg_ref[...] = p[:, :128] + p[:, 128:] + q + br_ref[...]


def _outproj(acts, ws, xsrc, g1, sh2, sc2, n2g, wr2, br, rows, n_lat_rows, n_mod, tm):
    lat_tiles = n_lat_rows // tm
    mod_idx = lambda i: (jnp.minimum((i * tm) // L, n_mod), 0, 0)
    in_specs, args = [], []
    for src in list(acts) + [xsrc]:
        in_specs += _src_specs(src, tm, lat_tiles)
        args += list(src)
    in_specs += [pl.BlockSpec(w.shape, lambda i: (0, 0)) for w in ws]
    in_specs += [pl.BlockSpec((None, 1, D), mod_idx),
                 pl.BlockSpec((None, 1, D), mod_idx),
                 pl.BlockSpec((None, 1, D), mod_idx),
                 pl.BlockSpec((1, D), lambda i: (0, 0)),
                 pl.BlockSpec((D, 256), lambda i: (0, 0)),
                 pl.BlockSpec((1, 128), lambda i: (0, 0))]
    args += list(ws) + [g1, sh2, sc2, n2g.reshape(1, D), wr2, br]
    return pl.pallas_call(
        functools.partial(_outproj_kernel, src_lens=tuple(len(s) for s in list(acts) + [xsrc]),
                          lat_tiles=lat_tiles),
        grid=(rows // tm,),
        in_specs=in_specs,
        out_specs=[pl.BlockSpec((tm, D), lambda i: (i, 0)),
                   pl.BlockSpec((tm, D // 128, 128), lambda i: (i, 0, 0)),
                   pl.BlockSpec((tm, 128), lambda i: (i, 0))],
        out_shape=[jax.ShapeDtypeStruct((rows, D), F32),
                   jax.ShapeDtypeStruct((rows, D // 128, 128), F32),
                   jax.ShapeDtypeStruct((rows, 128), F32)],
        compiler_params=_cp(("arbitrary",)),
        name="outproj_router",
    )(*args)


def _router_kernel(lg_ref, tri_ref, o_ref, cnt_ref, carry_scr):
    i = pl.program_id(0)

    @pl.when(i == 0)
    def _():
        carry_scr[...] = jnp.zeros_like(carry_scr)

    lg = lg_ref[...]
    lane = lax.broadcasted_iota(jnp.int32, lg.shape, 1)
    lane_f = lane.astype(F32)
    big = jnp.float32(1e9)

    gmask = lane < N_GROUPS
    gl = jnp.where(gmask, lg, NEG)
    gm = jnp.max(gl, axis=1, keepdims=True)
    gidx = jnp.min(jnp.where(gl == gm, lane_f, big), axis=1, keepdims=True)
    g_val = 1.0 / jnp.sum(jnp.where(gmask, jnp.exp(gl - gm), 0.0), axis=1, keepdims=True)

    base = 8.0 + 8.0 * gidx
    emask = (lane_f >= base) & (lane_f < base + EPG)
    el = jnp.where(emask, lg, NEG)
    m1 = jnp.max(el, axis=1, keepdims=True)
    i1 = jnp.min(jnp.where(el == m1, lane_f, big), axis=1, keepdims=True)
    el2 = jnp.where(lane_f == i1, NEG, el)
    m2 = jnp.max(el2, axis=1, keepdims=True)
    i2 = jnp.min(jnp.where(el2 == m2, lane_f, big), axis=1, keepdims=True)
    r = jnp.exp(m2 - m1)
    w1 = g_val / (1.0 + r)
    w2 = w1 * r
    gates = jnp.where(lane_f == i1 - base, w1, jnp.where(lane_f == i2 - base, w2, 0.0))

    onehot = jnp.where(lane_f == gidx, 1.0, 0.0)
    before = jnp.dot(tri_ref[...], onehot.astype(BF16), preferred_element_type=F32)
    carry = carry_scr[0:1, :]
    rank = jnp.sum(jnp.where(lane_f == gidx, before + carry, 0.0), axis=1, keepdims=True)
    carry_scr[0:1, :] = carry + jnp.sum(onehot, axis=0, keepdims=True)

    o_ref[...] = jnp.where(lane < EPG, gates, jnp.where(lane == 8, gidx, jnp.where(lane == 9, rank, 0.0)))
    cnt_ref[...] = carry_scr[...]


def _router(logits, tm):
    rows = logits.shape[0]
    tri = jnp.asarray(np.tril(np.ones((tm, tm), np.float32), -1), BF16)
    return pl.pallas_call(
        _router_kernel,
        grid=(rows // tm,),
        in_specs=[pl.BlockSpec((tm, 128), lambda i: (i, 0)),
                  pl.BlockSpec((tm, tm), lambda i: (0, 0))],
        out_specs=[pl.BlockSpec((tm, 128), lambda i: (i, 0)),
                   pl.BlockSpec((8, 128), lambda i: (0, 0))],
        out_shape=[jax.ShapeDtypeStruct((rows, 128), F32),
                   jax.ShapeDtypeStruct((8, 128), F32)],
        scratch_shapes=[pltpu.VMEM((8, 128), F32)],
        compiler_params=_cp(("arbitrary",)),
        name="router",
    )(logits, tri)


def _slot_kernel(route_ref, st_ref, o_ref):
    rt = route_ref[...].T
    g = rt[8:9, :]
    slot = rt[9:10, :]
    st = st_ref[...]
    for k in range(N_GROUPS):
        slot = slot + jnp.where(g == k, st[:, k:k + 1], 0.0)
    o_ref[...] = slot.astype(jnp.int32)


def _slots(route, starts_row):
    rows = route.shape[0]
    return pl.pallas_call(
        _slot_kernel,
        grid=(rows // TD,),
        in_specs=[pl.BlockSpec((TD, 128), lambda i: (i, 0)),
                  pl.BlockSpec((1, 128), lambda i: (0, 0))],
        out_specs=pl.BlockSpec((None, 1, TD), lambda i: (i, 0, 0)),
        out_shape=jax.ShapeDtypeStruct((rows // TD, 1, TD), jnp.int32),
        compiler_params=_cp(("arbitrary",)),
        name="moe_slots",
    )(route, starts_row)


def _dispatch_kernel(slot_ref, pad_ref, h2_ref, gt_ref, xs_ref, gs_ref, zx_scr, zg_scr, sem):
    i = pl.program_id(0)
    n_pad = pad_ref.shape[1]

    @pl.when(i == 0)
    def _():
        zx_scr[...] = jnp.zeros_like(zx_scr)
        zg_scr[...] = jnp.zeros_like(zg_scr)

        def zbody(r, carry):
            s = pad_ref[0, r]
            pltpu.make_async_copy(zx_scr, xs_ref.at[s], sem.at[0]).start()
            pltpu.make_async_copy(zg_scr.at[pl.ds(0, 1)], gs_ref.at[pl.ds(s, 1)], sem.at[1]).start()
            return carry

        lax.fori_loop(0, n_pad, zbody, 0)

        def wbody(r, carry):
            pltpu.make_async_copy(zx_scr, xs_ref.at[0], sem.at[0]).wait()
            pltpu.make_async_copy(zg_scr.at[pl.ds(0, 1)], gs_ref.at[pl.ds(0, 1)], sem.at[1]).wait()
            return carry

        lax.fori_loop(0, n_pad, wbody, 0)

    def body(j, carry):
        for u in range(2):
            r = 2 * j + u
            s = slot_ref[0, r]
            pltpu.make_async_copy(h2_ref.at[r], xs_ref.at[s], sem.at[0]).start(priority=u)
            pltpu.make_async_copy(gt_ref.at[pl.ds(r, 1)], gs_ref.at[pl.ds(s, 1)], sem.at[1]).start(priority=1 - u)
        return carry

    lax.fori_loop(0, TD // 2, body, 0)

    def wait_body(r, carry):
        pltpu.make_async_copy(h2_ref.at[0], xs_ref.at[0], sem.at[0]).wait()
        pltpu.make_async_copy(gt_ref.at[pl.ds(0, 1)], gs_ref.at[pl.ds(0, 1)], sem.at[1]).wait()
        return carry

    lax.fori_loop(0, TD, wait_body, 0)


def _dispatch(slot3, pad_slots, h2, gates, n_sorted):
    rows = h2.shape[0]
    n_pad = pad_slots.shape[0]
    return pl.pallas_call(
        _dispatch_kernel,
        grid=(rows // TD,),
        in_specs=[pl.BlockSpec((None, 1, TD), lambda i: (i, 0, 0), memory_space=pltpu.SMEM),
                  pl.BlockSpec((1, n_pad), lambda i: (0, 0), memory_space=pltpu.SMEM),
                  pl.BlockSpec((TD, D // 128, 128), lambda i: (i, 0, 0)),
                  pl.BlockSpec((TD, 128), lambda i: (i, 0))],
        out_specs=[pl.BlockSpec(memory_space=pl.ANY), pl.BlockSpec(memory_space=pl.ANY)],
        out_shape=[jax.ShapeDtypeStruct((n_sorted, D // 128, 128), F32),
                   jax.ShapeDtypeStruct((n_sorted, 128), F32)],
        scratch_shapes=[pltpu.VMEM((D // 128, 128), F32), pltpu.VMEM((8, 128), F32),
                        pltpu.SemaphoreType.DMA((2,))],
        compiler_params=_cp(("arbitrary",)),
        name="moe_dispatch",
    )(slot3, pad_slots.reshape(1, n_pad), h2, gates)


def _moe_kernel(tg_ref, na_ref, x_ref, g_ref, w1_ref, w3_ref, w2_ref, o_ref, hid_scr, w1b_scr, w3b_scr, w2b_scr):
    i = pl.program_id(0)

    @pl.when((i == 0) | (tg_ref[i] != tg_ref[jnp.maximum(i - 1, 0)]))
    def _():
        for e in range(EPG):
            cols = slice(e * EH, (e + 1) * EH)
            w1b_scr[:, cols] = w1_ref[:, cols].astype(BF16)
            w3b_scr[:, cols] = w3_ref[:, cols].astype(BF16)
            w2b_scr[cols, :] = w2_ref[cols, :].astype(BF16)

    @pl.when(i < na_ref[0])
    def _():
        x = jnp.concatenate([x_ref[:, k, :] for k in range(D // 128)], axis=1).astype(BF16)
        gt = g_ref[...]
        for e in range(EPG):
            cols = slice(e * EH, (e + 1) * EH)
            h1 = jnp.dot(x, w1b_scr[:, cols], preferred_element_type=F32)
            h3 = jnp.dot(x, w3b_scr[:, cols], preferred_element_type=F32)
            hid_scr[:, cols] = (_silu(h1) * h3 * gt[:, e:e + 1]).astype(BF16)
        y = jnp.dot(hid_scr[...], w2b_scr[...], preferred_element_type=F32)
        for k in range(D // 128):
            o_ref[:, k, :] = y[:, k * 128:(k + 1) * 128]

    @pl.when(i >= na_ref[0])
    def _():
        o_ref[...] = jnp.zeros_like(o_ref)


def _moe(tile_group, n_active, xs, gs, w1, w3, w2, n_tiles):
    x_idx = lambda i, tg, na: (jnp.minimum(i, na[0] - 1), 0, 0)
    g_idx = lambda i, tg, na: (jnp.minimum(i, na[0] - 1), 0)
    out_idx = lambda i, tg, na: (i, 0, 0)
    w_idx = lambda i, tg, na: (tg[i], 0, 0)
    once = pl.Buffered(1)
    grid_spec = pltpu.PrefetchScalarGridSpec(
        num_scalar_prefetch=2,
        grid=(n_tiles,),
        in_specs=[pl.BlockSpec((TS, D // 128, 128), x_idx),
                  pl.BlockSpec((TS, 128), g_idx),
                  pl.BlockSpec((None, D, GH), w_idx, pipeline_mode=once),
                  pl.BlockSpec((None, D, GH), w_idx, pipeline_mode=once),
                  pl.BlockSpec((None, GH, D), w_idx, pipeline_mode=once)],
        out_specs=pl.BlockSpec((TS, D // 128, 128), out_idx),
        scratch_shapes=[pltpu.VMEM((TS, GH), BF16), pltpu.VMEM((D, GH), BF16), pltpu.VMEM((D, GH), BF16),
                        pltpu.VMEM((GH, D), BF16)],
    )
    return pl.pallas_call(
        _moe_kernel,
        grid_spec=grid_spec,
        out_shape=jax.ShapeDtypeStruct(xs.shape, F32),
        compiler_params=_cp(("arbitrary",)),
        name="moe_experts",
    )(tile_group, n_active, xs, gs, w1, w3, w2)


def _combine_kernel(*refs, final, n_steps):
    if final:
        slot_ref, slot_next_ref, x1_ref, g2_ref, fg_ref, ys_ref, o_ref, buf, sem = refs
    else:
        slot_ref, slot_next_ref, x1_ref, g2_ref, ys_ref, o_ref, buf, sem = refs
    i = pl.program_id(0)

    def issue(sref, b):
        def body(j, carry):
            for u in range(2):
                r = 2 * j + u
                pltpu.make_async_copy(ys_ref.at[sref[0, r]], buf.at[b, r], sem.at[b]).start(priority=u)
            return carry

        lax.fori_loop(0, TD // 2, body, 0)

    @pl.when(i == 0)
    def _():
        issue(slot_ref, 0)

    @pl.when(i + 1 < n_steps)
    def _():
        issue(slot_next_ref, (i + 1) % 2)

    b = i % 2

    def wait_body(r, carry):
        pltpu.make_async_copy(ys_ref.at[0], buf.at[b, 0], sem.at[b]).wait()
        return carry

    lax.fori_loop(0, TD, wait_body, 0)

    y = jnp.concatenate([buf[b, :, k, :] for k in range(D // 128)], axis=1)
    x2 = x1_ref[...] + g2_ref[...] * y
    if final:
        ms = jnp.mean(x2 * x2, axis=-1, keepdims=True)
        x2 = x2 * lax.rsqrt(ms + EPS) * fg_ref[...]
    o_ref[...] = x2


def _combine(slot3, x1, g2, ys, n_mod, final_g=None):
    rows = x1.shape[0]
    n_steps = rows // TD
    final = final_g is not None
    mod_idx = lambda i: (jnp.minimum((i * TD) // L, n_mod), 0, 0)
    in_specs = [pl.BlockSpec((None, 1, TD), lambda i: (i, 0, 0), memory_space=pltpu.SMEM),
                pl.BlockSpec((None, 1, TD), lambda i: (jnp.minimum(i + 1, n_steps - 1), 0, 0),
                             memory_space=pltpu.SMEM),
                pl.BlockSpec((TD, D), lambda i: (i, 0)),
                pl.BlockSpec((None, 1, D), mod_idx)]
    args = [slot3, slot3, x1, g2]
    if final:
        in_specs.append(pl.BlockSpec((1, D), lambda i: (0, 0)))
        args.append(final_g.reshape(1, D))
    in_specs.append(pl.BlockSpec(memory_space=pl.ANY))
    args.append(ys)
    return pl.pallas_call(
        functools.partial(_combine_kernel, final=final, n_steps=n_steps),
        grid=(n_steps,),
        in_specs=in_specs,
        out_specs=pl.BlockSpec((TD, D), lambda i: (i, 0)),
        out_shape=jax.ShapeDtypeStruct((rows, D), F32),
        scratch_shapes=[pltpu.VMEM((2, TD, D // 128, 128), F32), pltpu.SemaphoreType.DMA((2,))],
        compiler_params=_cp(("arbitrary",)),
        name="moe_combine_final" if final else "moe_combine",
    )(*args)


def _sort_plan(counts_row, rows):
    counts = counts_row[:N_GROUPS].astype(jnp.int32)
    padded = ((counts + TS - 1) // TS) * TS
    ends = jnp.cumsum(padded)
    starts = ends - padded
    starts_row = jnp.zeros((1, 128), F32).at[0, :N_GROUPS].set(starts.astype(F32))
    n_tiles = rows // TS + N_GROUPS
    n_active = (ends[-1] // TS).reshape(1)
    tile_start = jnp.arange(n_tiles, dtype=jnp.int32) * TS
    tile_group = jnp.minimum(jnp.sum(tile_start[:, None] >= ends[None, :], axis=1), N_GROUPS - 1).astype(jnp.int32)
    r = jnp.arange(TS, dtype=jnp.int32)
    pad_real = ((starts + counts)[:, None] + r[None, :]).reshape(-1)
    is_pad = pad_real < jnp.repeat(ends, TS)
    tail = ends[-1] + jnp.cumsum(jnp.logical_not(is_pad).astype(jnp.int32)) - 1
    pad_slots = jnp.where(is_pad, pad_real, tail)
    return starts_row, pad_slots, tile_group, n_active, n_tiles, n_tiles * TS


def _moe_block(x1, h2, logits, g2, w1, w3, w2, n_mod, tm, final_g=None):
    rows = x1.shape[0]
    route, cnt = _router(logits, tm)
    starts_row, pad_slots, tile_group, n_active, n_tiles, n_sorted = _sort_plan(cnt[0], rows)
    slot3 = _slots(route, starts_row)
    xs, gs = _dispatch(slot3, pad_slots, h2, route, n_sorted)
    ys = _moe(tile_group, n_active, xs, gs, w1, w3, w2, n_tiles)
    return _combine(slot3, x1, g2, ys, n_mod, final_g)


def _router_weights(gw, gb, ew, eb):
    wr = jnp.zeros((D, 128), F32).at[:, :N_GROUPS].set(gw).at[:, 8:8 + N_GROUPS * EPG].set(ew)
    br = jnp.zeros((1, 128), F32).at[0, :N_GROUPS].set(gb).at[0, 8:8 + N_GROUPS * EPG].set(eb)
    hi = wr.astype(BF16)
    lo = (wr - hi.astype(F32)).astype(BF16)
    return jnp.concatenate([hi, lo], axis=1), br


def kernel(x, c, ctx, c_ctx, ada_w, ada_b, norm1_g, norm2_g, w_in_e, conv_w, na_rpb, w_out_e, w_in_o,
           ret_decay_f, ret_decay_b, w_out_o, router_gw, router_gb, router_ew, router_eb,
           moe_w1, moe_w3, moe_w2, final_g):
    nb = x.shape[0]
    assert x.shape[1:] == (L, D) and ctx.shape[1:] == (CTX, D) and nb < 16 and ada_w.shape[0] == 2
    t_lat = nb * L
    t_ctx = nb * CTX
    t_all = t_lat + t_ctx
    tm = 1024 if t_ctx % 1024 == 0 else 512

    xsrc = (x.reshape(t_lat, D), ctx.reshape(t_ctx, D))
    c16 = jnp.zeros((16, D), F32).at[:nb].set(c).at[nb].set(c_ctx)
    mod = _ada(c16, ada_w, ada_b).reshape(2, 16, 6, 1, D)
    mods = [[mod[l, :, k] for k in range(6)] for l in range(2)]

    sh1, sc1, g1, sh2, sc2, g2 = mods[0]
    qkv = _inproj(xsrc, norm1_g[0], sh1, sc1, w_in_e[0].astype(BF16), t_lat, nb, tm)
    y_conv = (_conv(qkv, conv_w[0], L, nb, 0), _conv(qkv, conv_w[0], CTX, nb, t_lat // CTX))
    y_attn = (_natten(qkv, _natten_bias(na_rpb[0]), nb), _ctx_attn(qkv, nb))
    w_out = w_out_e[0].astype(BF16)
    wr2, br = _router_weights(router_gw[0], router_gb[0], router_ew[0], router_eb[0])
    x1, h2, logits = _outproj([y_conv, y_attn], [w_out[:CONV_CH], w_out[CONV_CH:]], xsrc, g1, sh2, sc2,
                              norm2_g[0], wr2, br, t_all, t_lat, nb, tm)
    xa = _moe_block(x1, h2, logits, g2, moe_w1[0], moe_w3[0], moe_w2[0], nb, tm)

    sh1, sc1, g1, sh2, sc2, g2 = mods[1]
    qkvg = _inproj((xa,), norm1_g[1], sh1, sc1, w_in_o[0].astype(BF16), t_lat, nb, tm, _rope_tables(tm))
    lg = jnp.stack([jnp.log1p(-jnp.exp(ret_decay_f[0].astype(F32))),
                    jnp.log1p(-jnp.exp(ret_decay_b[0].astype(F32)))])
    y_ret = _retention(qkvg, lg, nb)
    wr2, br = _router_weights(router_gw[1], router_gb[1], router_ew[1], router_eb[1])
    x1, h2, logits = _outproj([(y_ret,)], [w_out_o[0].astype(BF16)], (xa,), g1, sh2, sc2, norm2_g[1], wr2, br,
                              t_lat, t_lat, nb, tm)
    out = _moe_block(x1, h2, logits, g2, moe_w1[1], moe_w3[1], moe_w2[1], nb, tm, final_g)
    return out.reshape(nb, L, D)
```

```python
import functools
import math

import numpy as np
import jax
import jax.numpy as jnp
from jax import lax
from jax.experimental import pallas as pl
from jax.experimental.pallas import tpu as pltpu

F32 = jnp.float32
BF16 = jnp.bfloat16
HIGHEST = lax.Precision.HIGHEST

D = 1024
L = 4096
GRID_W = 64
GRID_H = L // GRID_W
CTX = 256
EPS = 1e-6

CONV_CH = 512
NA_HEADS = 8
NA_DH = 64
WIN_H = 8
WIN_W = 16
EVEN_IN = 3072

RET_HEADS = 4
RET_DK = 256
RET_DV = 512
RET_C = 256
ODD_IN = 6144
ROPE_BASE = 10000.0

N_GROUPS = 4
EPG = 8
EH = 256
GH = EPG * EH

NEG = -1e30

NA_QR = 4
NA_KR = NA_QR + WIN_H - 1
NA_NQ = NA_QR * GRID_W
NA_NK = NA_KR * GRID_W
NA_WS_MAX = GRID_H - NA_KR

TS = 512
TD = 512

VMEM_LIMIT = 56 * 1024 * 1024


def _cp(sem, vmem=VMEM_LIMIT):
    return pltpu.CompilerParams(dimension_semantics=sem, vmem_limit_bytes=vmem)


def _silu(v):
    return v * (1.0 / (1.0 + jnp.exp(-v)))


def _src_specs(src, tm, lat_tiles):
    tail = tuple(src[0].shape[1:])
    zeros = (0,) * len(tail)
    if len(src) == 1:
        return [pl.BlockSpec((tm,) + tail, lambda i, *_: (i,) + zeros)]
    return [pl.BlockSpec((tm,) + tail, lambda i, *_: (jnp.minimum(i, lat_tiles - 1),) + zeros),
            pl.BlockSpec((tm,) + tail, lambda i, *_: (jnp.maximum(i - lat_tiles, 0),) + zeros)]


def _src_load(refs, i, lat_tiles):
    if len(refs) == 1:
        return refs[0][...]
    return jnp.where(i < lat_tiles, refs[0][...], refs[1][...])


def _ada_kernel(c_ref, w_ref, b_ref, o_ref):
    s = _silu(c_ref[...])
    o_ref[...] = jnp.dot(s, w_ref[...], precision=HIGHEST, preferred_element_type=F32) + b_ref[...]


def _ada(c16, ada_w, ada_b):
    depth = ada_w.shape[0]
    tn = 1536
    return pl.pallas_call(
        _ada_kernel,
        grid=(depth, 6 * D // tn),
        in_specs=[
            pl.BlockSpec((16, D), lambda l, j: (0, 0)),
            pl.BlockSpec((None, D, tn), lambda l, j: (l, 0, j)),
            pl.BlockSpec((None, 1, tn), lambda l, j: (l, 0, j)),
        ],
        out_specs=pl.BlockSpec((None, 16, tn), lambda l, j: (l, 0, j)),
        out_shape=jax.ShapeDtypeStruct((depth, 16, 6 * D), F32),
        compiler_params=_cp(("arbitrary", "arbitrary")),
        name="ada_mod",
    )(c16, ada_w, ada_b.reshape(depth, 1, 6 * D))


def _inproj_kernel(*refs, rope, tn, n_src, lat_tiles):
    x_refs, refs = refs[:n_src], refs[n_src:]
    if rope:
        g_ref, sh_ref, sc_ref, w_ref, cos_ref, sin_ref, o_ref, h_scr = refs
    else:
        g_ref, sh_ref, sc_ref, w_ref, o_ref, h_scr = refs
    i = pl.program_id(0)
    j = pl.program_id(1)

    @pl.when(j == 0)
    def _():
        x = _src_load(x_refs, i, lat_tiles)
        ms = jnp.mean(x * x, axis=-1, keepdims=True)
        y = x * lax.rsqrt(ms + EPS) * g_ref[...]
        h_scr[...] = (y * (1.0 + sc_ref[...]) + sh_ref[...]).astype(BF16)

    acc = jnp.dot(h_scr[...], w_ref[...], preferred_element_type=F32)
    if not rope:
        o_ref[...] = acc.astype(o_ref.dtype)
        return

    @pl.when(j < 2)
    def _():
        cos = cos_ref[...]
        sin = sin_ref[...]
        scale = jnp.where(j == 1, RET_DK ** -0.5, 1.0).astype(F32)
        pieces = []
        for cblk in range(tn // 128):
            t = acc[:, cblk * 128:(cblk + 1) * 128]
            half = (cblk % 2) * 128
            r = pltpu.roll(t, 64, axis=1)
            pieces.append(t * cos[:, half:half + 128] + r * sin[:, half:half + 128])
        o_ref[...] = (jnp.concatenate(pieces, axis=1) * scale).astype(o_ref.dtype)

    @pl.when(j >= 2)
    def _():
        o_ref[...] = acc.astype(o_ref.dtype)


def _inproj(xsrc, gain, sh, sc, w, n_lat_rows, n_mod, tm, rope_tables=None):
    rows = sum(a.shape[0] for a in xsrc)
    n = w.shape[1]
    tn = 1024
    lat_tiles = n_lat_rows // tm
    mod_idx = lambda i, j: (jnp.minimum((i * tm) // L, n_mod), 0, 0)
    in_specs = _src_specs(xsrc, tm, lat_tiles) + [
        pl.BlockSpec((1, D), lambda i, j: (0, 0)),
        pl.BlockSpec((None, 1, D), mod_idx),
        pl.BlockSpec((None, 1, D), mod_idx),
        pl.BlockSpec((D, tn), lambda i, j: (0, j)),
    ]
    args = list(xsrc) + [gain.reshape(1, D), sh, sc, w]
    rope = rope_tables is not None
    if rope:
        per_seq = L // tm
        tab_idx = lambda i, j: (jnp.where(i < lat_tiles, i % per_seq, per_seq), 0)
        in_specs += [pl.BlockSpec((tm, RET_DK), tab_idx), pl.BlockSpec((tm, RET_DK), tab_idx)]
        args += list(rope_tables)
    return pl.pallas_call(
        functools.partial(_inproj_kernel, rope=rope, tn=tn, n_src=len(xsrc), lat_tiles=lat_tiles),
        grid=(rows // tm, n // tn),
        in_specs=in_specs,
        out_specs=pl.BlockSpec((tm, tn), lambda i, j: (i, j)),
        out_shape=jax.ShapeDtypeStruct((rows, n), BF16),
        scratch_shapes=[pltpu.VMEM((tm, D), BF16)],
        compiler_params=_cp(("arbitrary", "arbitrary")),
        name="inproj_rope" if rope else "inproj",
    )(*args)


def _rope_tables(tm):
    pos = np.arange(L)
    half = RET_DK // 4
    freqs = ROPE_BASE ** (-np.arange(half, dtype=np.float64) / half)
    ang_r = (pos // GRID_W)[:, None] * freqs[None, :]
    ang_c = (pos % GRID_W)[:, None] * freqs[None, :]
    cos = np.concatenate([np.cos(ang_r), np.cos(ang_r), np.cos(ang_c), np.cos(ang_c)], axis=1)
    sin = np.concatenate([-np.sin(ang_r), np.sin(ang_r), -np.sin(ang_c), np.sin(ang_c)], axis=1)
    cos = np.concatenate([cos, np.ones((tm, RET_DK))], axis=0)
    sin = np.concatenate([sin, np.zeros((tm, RET_DK))], axis=0)
    return jnp.asarray(cos, F32), jnp.asarray(sin, F32)


def _conv_kernel(ab_ref, ac_ref, ax_ref, w_ref, o_ref, *, seq):
    u = ac_ref[...].astype(F32) * ax_ref[...].astype(F32)
    row = lax.broadcasted_iota(jnp.int32, u.shape, 0)
    up = jnp.where(row == 0, 0.0, pltpu.roll(u, 1, axis=0))
    un = jnp.where(row == seq - 1, 0.0, pltpu.roll(u, seq - 1, axis=0))
    w = w_ref[...]
    y = ab_ref[...].astype(F32) * (up * w[0:1, :] + u * w[1:2, :] + un * w[2:3, :])
    o_ref[...] = y.astype(o_ref.dtype)


def _conv(qkv, conv_w, seq, n_seq, row_blk0):
    nb = CONV_CH // 128
    return pl.pallas_call(
        functools.partial(_conv_kernel, seq=seq),
        grid=(n_seq, nb),
        in_specs=[
            pl.BlockSpec((seq, 128), lambda b, j: (row_blk0 + b, j)),
            pl.BlockSpec((seq, 128), lambda b, j: (row_blk0 + b, nb + j)),
            pl.BlockSpec((seq, 128), lambda b, j: (row_blk0 + b, 2 * nb + j)),
            pl.BlockSpec((3, 128), lambda b, j: (0, j)),
        ],
        out_specs=pl.BlockSpec((seq, 128), lambda b, j: (b, j)),
        out_shape=jax.ShapeDtypeStruct((n_seq * seq, CONV_CH), BF16),
        compiler_params=_cp(("arbitrary", "arbitrary")),
        name="short_conv",
    )(qkv, qkv, qkv, conv_w)


def _natten_bias(rpb):
    qc = np.arange(GRID_W)
    kc = np.arange(GRID_W)
    cs = np.clip(qc - WIN_W // 2, 0, GRID_W - WIN_W)
    col_ok = (kc[None, :] >= cs[:, None]) & (kc[None, :] < cs[:, None] + WIN_W)
    dx = np.clip(kc[None, :] - qc[:, None] + WIN_W - 1, 0, 2 * WIN_W - 2)
    tcol = jnp.where(col_ok[None, None], rpb.astype(F32)[:, :, dx], NEG)
    dys, oks = [], []
    for r0 in (0, 2 * NA_QR, GRID_H - NA_QR):
        ws = int(np.clip(r0 - WIN_H // 2, 0, NA_WS_MAX))
        qr = r0 + np.arange(NA_QR)
        kr = ws + np.arange(NA_KR)
        rs = np.clip(qr - WIN_H // 2, 0, GRID_H - WIN_H)
        oks.append((kr[None, :] >= rs[:, None]) & (kr[None, :] < rs[:, None] + WIN_H))
        dys.append(np.clip(kr[None, :] - qr[:, None] + WIN_H - 1, 0, 2 * WIN_H - 2))
    dy = np.stack(dys)
    ok = np.stack(oks)
    b = jnp.take(tcol, jnp.asarray(dy.reshape(-1)), axis=1)
    b = b.reshape(NA_HEADS, 3, NA_QR, NA_KR, GRID_W, GRID_W)
    b = jnp.where(ok[None, :, :, :, None, None], b, NEG)
    b = b.transpose(0, 1, 2, 4, 3, 5).reshape(NA_HEADS // 2, 2, 3, NA_NQ, NA_NK)
    return b.transpose(0, 2, 1, 3, 4)


def _attend(qa, kw, vw, kc, vc, bias):
    dn = (((1,), (1,)), ((), ()))
    s2 = lax.dot_general(qa, kc, dn, preferred_element_type=F32)
    m = jnp.max(s2, axis=1, keepdims=True)
    if kw is not None:
        s1 = lax.dot_general(qa, kw, dn, preferred_element_type=F32) + bias
        m = jnp.maximum(m, jnp.max(s1, axis=1, keepdims=True))
        p1 = jnp.exp(s1 - m)
    p2 = jnp.exp(s2 - m)
    den = jnp.sum(p2, axis=1, keepdims=True)
    o = jnp.dot(p2.astype(BF16), vc, preferred_element_type=F32)
    if kw is not None:
        den = den + jnp.sum(p1, axis=1, keepdims=True)
        o = o + jnp.dot(p1.astype(BF16), vw, preferred_element_type=F32)
    return o / den


def _natten_kernel(q_ref, k_ref, v_ref, kc_ref, vc_ref, bias_ref, o_ref):
    rb = pl.program_id(2)
    ws = jnp.clip(rb * NA_QR - WIN_H // 2, 0, NA_WS_MAX)
    start = pl.multiple_of(ws * GRID_W, GRID_W)
    kw = k_ref[pl.ds(start, NA_NK), :]
    vw = v_ref[pl.ds(start, NA_NK), :]
    q = q_ref[...] * (NA_DH ** -0.5)
    lane = lax.broadcasted_iota(jnp.int32, q.shape, 1)
    zero = jnp.zeros_like(q)
    o0 = _attend(jnp.where(lane < NA_DH, q, zero), kw, vw, kc_ref[...], vc_ref[...], bias_ref[0])
    o1 = _attend(jnp.where(lane >= NA_DH, q, zero), kw, vw, kc_ref[...], vc_ref[...], bias_ref[1])
    o_ref[...] = jnp.where(lane < NA_DH, o0, o1).astype(o_ref.dtype)


def _natten(qkv, bias, nb):
    n_rb = GRID_H // NA_QR
    qcol, kcol, vcol = 3 * CONV_CH // 128, 3 * CONV_CH // 128 + 4, 3 * CONV_CH // 128 + 8
    ctx_blk0 = nb * L // CTX
    cls = lambda r: jnp.where(r == 0, 0, jnp.where(r == n_rb - 1, 2, 1))
    return pl.pallas_call(
        _natten_kernel,
        grid=(nb, NA_HEADS // 2, n_rb),
        in_specs=[
            pl.BlockSpec((NA_NQ, 128), lambda b, h, r: (b * n_rb + r, qcol + h)),
            pl.BlockSpec((L, 128), lambda b, h, r: (b, kcol + h)),
            pl.BlockSpec((L, 128), lambda b, h, r: (b, vcol + h)),
            pl.BlockSpec((CTX, 128), lambda b, h, r: (ctx_blk0 + b, kcol + h)),
            pl.BlockSpec((CTX, 128), lambda b, h, r: (ctx_blk0 + b, vcol + h)),
            pl.BlockSpec((None, None, 2, NA_NQ, NA_NK), lambda b, h, r: (h, cls(r), 0, 0, 0)),
        ],
        out_specs=pl.BlockSpec((NA_NQ, 128), lambda b, h, r: (b * n_rb + r, h)),
        out_shape=jax.ShapeDtypeStruct((nb * L, NA_HEADS * NA_DH), BF16),
        compiler_params=_cp(("arbitrary", "arbitrary", "arbitrary")),
        name="natten",
    )(qkv, qkv, qkv, qkv, qkv, bias)


def _ctx_attn_kernel(q_ref, kc_ref, vc_ref, o_ref):
    q = q_ref[...] * (NA_DH ** -0.5)
    lane = lax.broadcasted_iota(jnp.int32, q.shape, 1)
    zero = jnp.zeros_like(q)
    o0 = _attend(jnp.where(lane < NA_DH, q, zero), None, None, kc_ref[...], vc_ref[...], None)
    o1 = _attend(jnp.where(lane >= NA_DH, q, zero), None, None, kc_ref[...], vc_ref[...], None)
    o_ref[...] = jnp.where(lane < NA_DH, o0, o1).astype(o_ref.dtype)


def _ctx_attn(qkv, nb):
    qcol, kcol, vcol = 3 * CONV_CH // 128, 3 * CONV_CH // 128 + 4, 3 * CONV_CH // 128 + 8
    ctx_blk0 = nb * L // CTX
    return pl.pallas_call(
        _ctx_attn_kernel,
        grid=(nb, NA_HEADS // 2),
        in_specs=[
            pl.BlockSpec((CTX, 128), lambda b, h: (ctx_blk0 + b, qcol + h)),
            pl.BlockSpec((CTX, 128), lambda b, h: (ctx_blk0 + b, kcol + h)),
            pl.BlockSpec((CTX, 128), lambda b, h: (ctx_blk0 + b, vcol + h)),
        ],
        out_specs=pl.BlockSpec((CTX, 128), lambda b, h: (b, h)),
        out_shape=jax.ShapeDtypeStruct((nb * CTX, NA_HEADS * NA_DH), BF16),
        compiler_params=_cp(("arbitrary", "arbitrary")),
        name="ctx_attn",
    )(qkv, qkv, qkv)


def _retention_kernel(lg_ref, q_ref, k_ref, v_ref, g_ref, kc_ref, vc_ref, o_ref, y_scr, stf_scr, stb_scr):
    h = pl.program_id(1)
    n_chunks = L // RET_C
    pos_r = lax.broadcasted_iota(jnp.int32, (RET_C, 1), 0).astype(F32)
    rel = pos_r - lax.broadcasted_iota(jnp.int32, (1, RET_C), 1).astype(F32)
    dn_t = (((1,), (1,)), ((), ()))
    dn_k = (((0,), (0,)), ((), ()))

    def decays(lg, fwd):
        if fwd:
            d_intra = jnp.where(rel >= 0, jnp.exp(lg * jnp.maximum(rel, 0.0)), 0.0)
            d_q = jnp.exp(lg * (pos_r + 1.0))
            d_k = jnp.exp(lg * (RET_C - 1.0 - pos_r))
        else:
            d_intra = jnp.where(rel <= 0, jnp.exp(lg * jnp.maximum(-rel, 0.0)), 0.0)
            d_q = jnp.exp(lg * (RET_C - pos_r))
            d_k = jnp.exp(lg * pos_r)
        d_c = jnp.exp(jnp.full((1, 1), RET_C, F32) * lg)
        return d_intra, d_q, d_k, d_c

    def ctx_state(st_scr, d_k, d_c, order):
        st_scr[...] = jnp.zeros_like(st_scr)
        for jc in order:
            kd = (kc_ref[jc * RET_C:(jc + 1) * RET_C, :].astype(F32) * d_k).astype(BF16)
            vc = vc_ref[jc * RET_C:(jc + 1) * RET_C, :]
            st_scr[...] = st_scr[...] * d_c + lax.dot_general(kd, vc, dn_k, preferred_element_type=F32)

    def chunk_out(st_scr, c, d_intra, d_q, d_k, d_c):
        r0 = pl.multiple_of(c * RET_C, RET_C)
        q = q_ref[pl.ds(r0, RET_C), :]
        k = k_ref[pl.ds(r0, RET_C), :]
        v = v_ref[pl.ds(r0, RET_C), :]
        s = lax.dot_general(q, k, dn_t, preferred_element_type=F32) * d_intra
        st = st_scr[...]
        o = (jnp.dot(s.astype(BF16), v, preferred_element_type=F32)
             + jnp.dot(q, st.astype(BF16), preferred_element_type=F32) * d_q)
        kd = (k.astype(F32) * d_k).astype(BF16)
        st_scr[...] = st * d_c + lax.dot_general(kd, v, dn_k, preferred_element_type=F32)
        return r0, o

    def finish(r0, y):
        y = y * lax.rsqrt(jnp.mean(y * y, axis=-1, keepdims=True) + EPS)
        g = g_ref[pl.ds(r0, RET_C), :].astype(F32)
        o_ref[pl.ds(r0, RET_C), :] = (_silu(g) * y).astype(o_ref.dtype)

    dec_f = decays(lg_ref[0, h], True)
    dec_b = decays(lg_ref[1, h], False)
    ctx_state(stf_scr, dec_f[2], dec_f[3], range(CTX // RET_C))
    ctx_state(stb_scr, dec_b[2], dec_b[3], range(CTX // RET_C - 1, -1, -1))

    def first_half(i, carry):
        r0, o = chunk_out(stf_scr, i, *dec_f)
        y_scr[pl.ds(r0, RET_C), :] = o
        r1, o = chunk_out(stb_scr, n_chunks - 1 - i, *dec_b)
        y_scr[pl.ds(r1, RET_C), :] = o
        return carry

    lax.fori_loop(0, n_chunks // 2, first_half, 0)

    def second_half(i, carry):
        r0, o = chunk_out(stf_scr, i, *dec_f)
        finish(r0, y_scr[pl.ds(r0, RET_C), :] + o)
        r1, o = chunk_out(stb_scr, n_chunks - 1 - i, *dec_b)
        finish(r1, y_scr[pl.ds(r1, RET_C), :] + o)
        return carry

    lax.fori_loop(n_chunks // 2, n_chunks, second_half, 0)


def _retention(qkvg, lg, nb):
    ctx_blk0 = nb * L // CTX
    kq = RET_HEADS
    vq = 2 * RET_HEADS * RET_DK // RET_DV
    gq = vq + RET_HEADS
    return pl.pallas_call(
        _retention_kernel,
        grid=(nb, RET_HEADS),
        in_specs=[
            pl.BlockSpec(memory_space=pltpu.SMEM),
            pl.BlockSpec((L, RET_DK), lambda b, h: (b, h)),
            pl.BlockSpec((L, RET_DK), lambda b, h: (b, kq + h)),
            pl.BlockSpec((L, RET_DV), lambda b, h: (b, vq + h)),
            pl.BlockSpec((L, RET_DV), lambda b, h: (b, gq + h)),
            pl.BlockSpec((CTX, RET_DK), lambda b, h: (ctx_blk0 + b, kq + h)),
            pl.BlockSpec((CTX, RET_DV), lambda b, h: (ctx_blk0 + b, vq + h)),
        ],
        out_specs=pl.BlockSpec((L, RET_DV), lambda b, h: (b, h)),
        out_shape=jax.ShapeDtypeStruct((nb * L, RET_HEADS * RET_DV), BF16),
        scratch_shapes=[pltpu.VMEM((L, RET_DV), F32), pltpu.VMEM((RET_DK, RET_DV), F32),
                        pltpu.VMEM((RET_DK, RET_DV), F32)],
        compiler_params=_cp(("arbitrary", "arbitrary")),
        name="retention",
    )(lg, qkvg, qkvg, qkvg, qkvg, qkvg, qkvg)


def _outproj_kernel(*refs, src_lens, lat_tiles):
    i = pl.program_id(0)
    srcs = []
    for n in src_lens:
        srcs.append(refs[:n])
        refs = refs[n:]
    n_act = len(src_lens) - 1
    w_refs, refs = refs[:n_act], refs[n_act:]
    g1_ref, sh2_ref, sc2_ref, n2_ref, wr_ref, br_ref, x1_ref, h2_ref, lg_ref = refs
    y = None
    for a_src, w_ref in zip(srcs[:n_act], w_refs):
        t = jnp.dot(_src_load(a_src, i, lat_tiles), w_ref[...], preferred_element_type=F32)
        y = t if y is None else y + t
    x1 = _src_load(srcs[n_act], i, lat_tiles) + g1_ref[...] * y
    x1_ref[...] = x1
    ms = jnp.mean(x1 * x1, axis=-1, keepdims=True)
    h2 = x1 * lax.rsqrt(ms + EPS) * n2_ref[...] * (1.0 + sc2_ref[...]) + sh2_ref[...]
    for k in range(D // 128):
        h2_ref[:, k, :] = h2[:, k * 128:(k + 1) * 128]
    h_hi = h2.astype(BF16)
    h_lo = (h2 - h_hi.astype(F32)).astype(BF16)
    p = jnp.dot(h_hi, wr_ref[...], preferred_element_type=F32)
    q = jnp.dot(h_lo, wr_ref[:, :128], preferred_element_type=F32)
    lg_ref[...] = p[:, :128] + p[:, 128:] + q + br_ref[...]


def _outproj(acts, ws, xsrc, g1, sh2, sc2, n2g, wr2, br, rows, n_lat_rows, n_mod, tm):
    lat_tiles = n_lat_rows // tm
    mod_idx = lambda i: (jnp.minimum((i * tm) // L, n_mod), 0, 0)
    in_specs, args = [], []
    for src in list(acts) + [xsrc]:
        in_specs += _src_specs(src, tm, lat_tiles)
        args += list(src)
    in_specs += [pl.BlockSpec(w.shape, lambda i: (0, 0)) for w in ws]
    in_specs += [pl.BlockSpec((None, 1, D), mod_idx),
                 pl.BlockSpec((None, 1, D), mod_idx),
                 pl.BlockSpec((None, 1, D), mod_idx),
                 pl.BlockSpec((1, D), lambda i: (0, 0)),
                 pl.BlockSpec((D, 256), lambda i: (0, 0)),
                 pl.BlockSpec((1, 128), lambda i: (0, 0))]
    args += list(ws) + [g1, sh2, sc2, n2g.reshape(1, D), wr2, br]
    return pl.pallas_call(
        functools.partial(_outproj_kernel, src_lens=tuple(len(s) for s in list(acts) + [xsrc]),
                          lat_tiles=lat_tiles),
        grid=(rows // tm,),
        in_specs=in_specs,
        out_specs=[pl.BlockSpec((tm, D), lambda i: (i, 0)),
                   pl.BlockSpec((tm, D // 128, 128), lambda i: (i, 0, 0)),
                   pl.BlockSpec((tm, 128), lambda i: (i, 0))],
        out_shape=[jax.ShapeDtypeStruct((rows, D), F32),
                   jax.ShapeDtypeStruct((rows, D // 128, 128), F32),
                   jax.ShapeDtypeStruct((rows, 128), F32)],
        compiler_params=_cp(("arbitrary",)),
        name="outproj_router",
    )(*args)


def _router_kernel(lg_ref, tri_ref, o_ref, cnt_ref, carry_scr):
    i = pl.program_id(0)

    @pl.when(i == 0)
    def _():
        carry_scr[...] = jnp.zeros_like(carry_scr)

    lg = lg_ref[...]
    lane = lax.broadcasted_iota(jnp.int32, lg.shape, 1)
    lane_f = lane.astype(F32)
    big = jnp.float32(1e9)

    gmask = lane < N_GROUPS
    gl = jnp.where(gmask, lg, NEG)
    gm = jnp.max(gl, axis=1, keepdims=True)
    gidx = jnp.min(jnp.where(gl == gm, lane_f, big), axis=1, keepdims=True)
    g_val = 1.0 / jnp.sum(jnp.where(gmask, jnp.exp(gl - gm), 0.0), axis=1, keepdims=True)

    base = 8.0 + 8.0 * gidx
    emask = (lane_f >= base) & (lane_f < base + EPG)
    el = jnp.where(emask, lg, NEG)
    m1 = jnp.max(el, axis=1, keepdims=True)
    i1 = jnp.min(jnp.where(el == m1, lane_f, big), axis=1, keepdims=True)
    el2 = jnp.where(lane_f == i1, NEG, el)
    m2 = jnp.max(el2, axis=1, keepdims=True)
    i2 = jnp.min(jnp.where(el2 == m2, lane_f, big), axis=1, keepdims=True)
    r = jnp.exp(m2 - m1)
    w1 = g_val / (1.0 + r)
    w2 = w1 * r
    gates = jnp.where(lane_f == i1 - base, w1, jnp.where(lane_f == i2 - base, w2, 0.0))

    onehot = jnp.where(lane_f == gidx, 1.0, 0.0)
    before = jnp.dot(tri_ref[...], onehot.astype(BF16), preferred_element_type=F32)
    carry = carry_scr[0:1, :]
    rank = jnp.sum(jnp.where(lane_f == gidx, before + carry, 0.0), axis=1, keepdims=True)
    carry_scr[0:1, :] = carry + jnp.sum(onehot, axis=0, keepdims=True)

    o_ref[...] = jnp.where(lane < EPG, gates, jnp.where(lane == 8, gidx, jnp.where(lane == 9, rank, 0.0)))
    cnt_ref[...] = carry_scr[...]


def _router(logits, tm):
    rows = logits.shape[0]
    tri = jnp.asarray(np.tril(np.ones((tm, tm), np.float32), -1), BF16)
    return pl.pallas_call(
        _router_kernel,
        grid=(rows // tm,),
        in_specs=[pl.BlockSpec((tm, 128), lambda i: (i, 0)),
                  pl.BlockSpec((tm, tm), lambda i: (0, 0))],
        out_specs=[pl.BlockSpec((tm, 128), lambda i: (i, 0)),
                   pl.BlockSpec((8, 128), lambda i: (0, 0))],
        out_shape=[jax.ShapeDtypeStruct((rows, 128), F32),
                   jax.ShapeDtypeStruct((8, 128), F32)],
        scratch_shapes=[pltpu.VMEM((8, 128), F32)],
        compiler_params=_cp(("arbitrary",)),
        name="router",
    )(logits, tri)


def _slot_kernel(route_ref, st_ref, o_ref):
    rt = route_ref[...].T
    g = rt[8:9, :]
    slot = rt[9:10, :]
    st = st_ref[...]
    for k in range(N_GROUPS):
        slot = slot + jnp.where(g == k, st[:, k:k + 1], 0.0)
    o_ref[...] = slot.astype(jnp.int32)


def _slots(route, starts_row):
    rows = route.shape[0]
    return pl.pallas_call(
        _slot_kernel,
        grid=(rows // TD,),
        in_specs=[pl.BlockSpec((TD, 128), lambda i: (i, 0)),
                  pl.BlockSpec((1, 128), lambda i: (0, 0))],
        out_specs=pl.BlockSpec((None, 1, TD), lambda i: (i, 0, 0)),
        out_shape=jax.ShapeDtypeStruct((rows // TD, 1, TD), jnp.int32),
        compiler_params=_cp(("arbitrary",)),
        name="moe_slots",
    )(route, starts_row)


def _dispatch_kernel(slot_ref, pad_ref, h2_ref, gt_ref, xs_ref, gs_ref, zx_scr, zg_scr, sem):
    i = pl.program_id(0)
    n_pad = pad_ref.shape[1]

    @pl.when(i == 0)
    def _():
        zx_scr[...] = jnp.zeros_like(zx_scr)
        zg_scr[...] = jnp.zeros_like(zg_scr)

        def zbody(r, carry):
            s = pad_ref[0, r]
            pltpu.make_async_copy(zx_scr, xs_ref.at[s], sem.at[0]).start()
            pltpu.make_async_copy(zg_scr.at[pl.ds(0, 1)], gs_ref.at[pl.ds(s, 1)], sem.at[1]).start()
            return carry

        lax.fori_loop(0, n_pad, zbody, 0)
        pltpu.make_async_copy(xs_ref.at[pl.ds(0, n_pad)], xs_ref.at[pl.ds(0, n_pad)], sem.at[0]).wait()
        pltpu.make_async_copy(gs_ref.at[pl.ds(0, n_pad)], gs_ref.at[pl.ds(0, n_pad)], sem.at[1]).wait()

    def body(j, carry):
        for u in range(2):
            r = 2 * j + u
            s = slot_ref[0, r]
            pltpu.make_async_copy(h2_ref.at[r], xs_ref.at[s], sem.at[0]).start(priority=u)
            pltpu.make_async_copy(gt_ref.at[pl.ds(r, 1)], gs_ref.at[pl.ds(s, 1)], sem.at[1]).start(priority=1 - u)
        return carry

    lax.fori_loop(0, TD // 2, body, 0)
    pltpu.make_async_copy(h2_ref, xs_ref.at[pl.ds(0, TD)], sem.at[0]).wait()
    pltpu.make_async_copy(gt_ref, gs_ref.at[pl.ds(0, TD)], sem.at[1]).wait()


def _dispatch(slot3, pad_slots, h2, gates, n_sorted):
    rows = h2.shape[0]
    n_pad = pad_slots.shape[0]
    return pl.pallas_call(
        _dispatch_kernel,
        grid=(rows // TD,),
        in_specs=[pl.BlockSpec((None, 1, TD), lambda i: (i, 0, 0), memory_space=pltpu.SMEM),
                  pl.BlockSpec((1, n_pad), lambda i: (0, 0), memory_space=pltpu.SMEM),
                  pl.BlockSpec((TD, D // 128, 128), lambda i: (i, 0, 0)),
                  pl.BlockSpec((TD, 128), lambda i: (i, 0))],
        out_specs=[pl.BlockSpec(memory_space=pl.ANY), pl.BlockSpec(memory_space=pl.ANY)],
        out_shape=[jax.ShapeDtypeStruct((n_sorted, D // 128, 128), F32),
                   jax.ShapeDtypeStruct((n_sorted, 128), F32)],
        scratch_shapes=[pltpu.VMEM((D // 128, 128), F32), pltpu.VMEM((8, 128), F32),
                        pltpu.SemaphoreType.DMA((2,))],
        compiler_params=_cp(("arbitrary",)),
        name="moe_dispatch",
    )(slot3, pad_slots.reshape(1, n_pad), h2, gates)


def _moe_kernel(tg_ref, na_ref, x_ref, g_ref, w1_ref, w3_ref, w2_ref, o_ref, hid_scr, w1b_scr, w3b_scr, w2b_scr):
    i = pl.program_id(0)

    @pl.when((i == 0) | (tg_ref[i] != tg_ref[jnp.maximum(i - 1, 0)]))
    def _():
        for e in range(EPG):
            cols = slice(e * EH, (e + 1) * EH)
            w1b_scr[:, cols] = w1_ref[:, cols].astype(BF16)
            w3b_scr[:, cols] = w3_ref[:, cols].astype(BF16)
            w2b_scr[cols, :] = w2_ref[cols, :].astype(BF16)

    @pl.when(i < na_ref[0])
    def _():
        x = jnp.concatenate([x_ref[:, k, :] for k in range(D // 128)], axis=1).astype(BF16)
        gt = g_ref[...]
        for e in range(EPG):
            cols = slice(e * EH, (e + 1) * EH)
            h1 = jnp.dot(x, w1b_scr[:, cols], preferred_element_type=F32)
            h3 = jnp.dot(x, w3b_scr[:, cols], preferred_element_type=F32)
            hid_scr[:, cols] = (_silu(h1) * h3 * gt[:, e:e + 1]).astype(BF16)
        y = jnp.dot(hid_scr[...], w2b_scr[...], preferred_element_type=F32)
        for k in range(D // 128):
            o_ref[:, k, :] = y[:, k * 128:(k + 1) * 128]

    @pl.when(i >= na_ref[0])
    def _():
        o_ref[...] = jnp.zeros_like(o_ref)


def _moe(tile_group, n_active, xs, gs, w1, w3, w2, layer, n_tiles):
    x_idx = lambda i, tg, na: (jnp.minimum(i, na[0] - 1), 0, 0)
    g_idx = lambda i, tg, na: (jnp.minimum(i, na[0] - 1), 0)
    out_idx = lambda i, tg, na: (i, 0, 0)
    w_idx = lambda i, tg, na: (layer, tg[i], 0, 0)
    once = pl.Buffered(1)
    grid_spec = pltpu.PrefetchScalarGridSpec(
        num_scalar_prefetch=2,
        grid=(n_tiles,),
        in_specs=[pl.BlockSpec((TS, D // 128, 128), x_idx),
                  pl.BlockSpec((TS, 128), g_idx),
                  pl.BlockSpec((None, None, D, GH), w_idx, pipeline_mode=once),
                  pl.BlockSpec((None, None, D, GH), w_idx, pipeline_mode=once),
                  pl.BlockSpec((None, None, GH, D), w_idx, pipeline_mode=once)],
        out_specs=pl.BlockSpec((TS, D // 128, 128), out_idx),
        scratch_shapes=[pltpu.VMEM((TS, GH), BF16), pltpu.VMEM((D, GH), BF16), pltpu.VMEM((D, GH), BF16),
                        pltpu.VMEM((GH, D), BF16)],
    )
    return pl.pallas_call(
        _moe_kernel,
        grid_spec=grid_spec,
        out_shape=jax.ShapeDtypeStruct(xs.shape, F32),
        compiler_params=_cp(("arbitrary",)),
        name="moe_experts",
    )(tile_group, n_active, xs, gs, w1, w3, w2)


def _combine_kernel(*refs, final, n_steps):
    if final:
        slot_ref, slot_next_ref, x1_ref, g2_ref, fg_ref, ys_ref, o_ref, buf, sem = refs
    else:
        slot_ref, slot_next_ref, x1_ref, g2_ref, ys_ref, o_ref, buf, sem = refs
    i = pl.program_id(0)

    def issue(sref, b):
        def body(j, carry):
            for u in range(2):
                r = 2 * j + u
                pltpu.make_async_copy(ys_ref.at[sref[0, r]], buf.at[b, r], sem.at[b]).start(priority=u)
            return carry

        lax.fori_loop(0, TD // 2, body, 0)

    @pl.when(i == 0)
    def _():
        issue(slot_ref, 0)

    @pl.when(i + 1 < n_steps)
    def _():
        issue(slot_next_ref, (i + 1) % 2)

    b = i % 2
    pltpu.make_async_copy(ys_ref.at[pl.ds(0, TD)], buf.at[b], sem.at[b]).wait()

    y = jnp.concatenate([buf[b, :, k, :] for k in range(D // 128)], axis=1)
    x2 = x1_ref[...] + g2_ref[...] * y
    if final:
        ms = jnp.mean(x2 * x2, axis=-1, keepdims=True)
        x2 = x2 * lax.rsqrt(ms + EPS) * fg_ref[...]
    o_ref[...] = x2


def _combine(slot3, x1, g2, ys, n_mod, final_g=None):
    rows = x1.shape[0]
    n_steps = rows // TD
    final = final_g is not None
    mod_idx = lambda i: (jnp.minimum((i * TD) // L, n_mod), 0, 0)
    in_specs = [pl.BlockSpec((None, 1, TD), lambda i: (i, 0, 0), memory_space=pltpu.SMEM),
                pl.BlockSpec((None, 1, TD), lambda i: (jnp.minimum(i + 1, n_steps - 1), 0, 0),
                             memory_space=pltpu.SMEM),
                pl.BlockSpec((TD, D), lambda i: (i, 0)),
                pl.BlockSpec((None, 1, D), mod_idx)]
    args = [slot3, slot3, x1, g2]
    if final:
        in_specs.append(pl.BlockSpec((1, D), lambda i: (0, 0)))
        args.append(final_g.reshape(1, D))
    in_specs.append(pl.BlockSpec(memory_space=pl.ANY))
    args.append(ys)
    return pl.pallas_call(
        functools.partial(_combine_kernel, final=final, n_steps=n_steps),
        grid=(n_steps,),
        in_specs=in_specs,
        out_specs=pl.BlockSpec((TD, D), lambda i: (i, 0)),
        out_shape=jax.ShapeDtypeStruct((rows, D), F32),
        scratch_shapes=[pltpu.VMEM((2, TD, D // 128, 128), F32), pltpu.SemaphoreType.DMA((2,))],
        compiler_params=_cp(("arbitrary",)),
        name="moe_combine_final" if final else "moe_combine",
    )(*args)


def _sort_plan(counts_row, rows):
    counts = counts_row[:N_GROUPS].astype(jnp.int32)
    padded = ((counts + TS - 1) // TS) * TS
    ends = jnp.cumsum(padded)
    starts = ends - padded
    starts_row = jnp.zeros((1, 128), F32).at[0, :N_GROUPS].set(starts.astype(F32))
    n_tiles = rows // TS + N_GROUPS
    n_active = (ends[-1] // TS).reshape(1)
    tile_start = jnp.arange(n_tiles, dtype=jnp.int32) * TS
    tile_group = jnp.minimum(jnp.sum(tile_start[:, None] >= ends[None, :], axis=1), N_GROUPS - 1).astype(jnp.int32)
    r = jnp.arange(TS, dtype=jnp.int32)
    pad_real = ((starts + counts)[:, None] + r[None, :]).reshape(-1)
    is_pad = pad_real < jnp.repeat(ends, TS)
    tail = ends[-1] + jnp.cumsum(jnp.logical_not(is_pad).astype(jnp.int32)) - 1
    pad_slots = jnp.where(is_pad, pad_real, tail)
    return starts_row, pad_slots, tile_group, n_active, n_tiles, n_tiles * TS


def _moe_block(x1, h2, logits, g2, w1, w3, w2, layer, n_mod, tm, final_g=None):
    rows = x1.shape[0]
    route, cnt = _router(logits, tm)
    starts_row, pad_slots, tile_group, n_active, n_tiles, n_sorted = _sort_plan(cnt[0], rows)
    slot3 = _slots(route, starts_row)
    xs, gs = _dispatch(slot3, pad_slots, h2, route, n_sorted)
    ys = _moe(tile_group, n_active, xs, gs, w1, w3, w2, layer, n_tiles)
    return _combine(slot3, x1, g2, ys, n_mod, final_g)


def _router_weights(gw, gb, ew, eb):
    wr = jnp.zeros((D, 128), F32).at[:, :N_GROUPS].set(gw).at[:, 8:8 + N_GROUPS * EPG].set(ew)
    br = jnp.zeros((1, 128), F32).at[0, :N_GROUPS].set(gb).at[0, 8:8 + N_GROUPS * EPG].set(eb)
    hi = wr.astype(BF16)
    lo = (wr - hi.astype(F32)).astype(BF16)
    return jnp.concatenate([hi, lo], axis=1), br


def kernel(x, c, ctx, c_ctx, ada_w, ada_b, norm1_g, norm2_g, w_in_e, conv_w, na_rpb, w_out_e, w_in_o,
           ret_decay_f, ret_decay_b, w_out_o, router_gw, router_gb, router_ew, router_eb,
           moe_w1, moe_w3, moe_w2, final_g):
    nb = x.shape[0]
    assert x.shape[1:] == (L, D) and ctx.shape[1:] == (CTX, D) and nb < 16 and ada_w.shape[0] == 2
    t_lat = nb * L
    t_ctx = nb * CTX
    t_all = t_lat + t_ctx
    tm = 1024 if t_ctx % 1024 == 0 else 512

    xsrc = (x.reshape(t_lat, D), ctx.reshape(t_ctx, D))
    c16 = jnp.zeros((16, D), F32).at[:nb].set(c).at[nb].set(c_ctx)
    mod = _ada(c16, ada_w, ada_b).reshape(2, 16, 6, 1, D)
    mods = [[mod[l, :, k] for k in range(6)] for l in range(2)]

    sh1, sc1, g1, sh2, sc2, g2 = mods[0]
    qkv = _inproj(xsrc, norm1_g[0], sh1, sc1, w_in_e[0].astype(BF16), t_lat, nb, tm)
    y_conv = (_conv(qkv, conv_w[0], L, nb, 0), _conv(qkv, conv_w[0], CTX, nb, t_lat // CTX))
    y_attn = (_natten(qkv, _natten_bias(na_rpb[0]), nb), _ctx_attn(qkv, nb))
    w_out = w_out_e[0].astype(BF16)
    wr2, br = _router_weights(router_gw[0], router_gb[0], router_ew[0], router_eb[0])
    x1, h2, logits = _outproj([y_conv, y_attn], [w_out[:CONV_CH], w_out[CONV_CH:]], xsrc, g1, sh2, sc2,
                              norm2_g[0], wr2, br, t_all, t_lat, nb, tm)
    xa = _moe_block(x1, h2, logits, g2, moe_w1, moe_w3, moe_w2, 0, nb, tm)

    sh1, sc1, g1, sh2, sc2, g2 = mods[1]
    qkvg = _inproj((xa,), norm1_g[1], sh1, sc1, w_in_o[0].astype(BF16), t_lat, nb, tm, _rope_tables(tm))
    lg = jnp.stack([jnp.log1p(-jnp.exp(ret_decay_f[0].astype(F32))),
                    jnp.log1p(-jnp.exp(ret_decay_b[0].astype(F32)))])
    y_ret = _retention(qkvg, lg, nb)
    wr2, br = _router_weights(router_gw[1], router_gb[1], router_ew[1], router_eb[1])
    x1, h2, logits = _outproj([(y_ret,)], [w_out_o[0].astype(BF16)], (xa,), g1, sh2, sc2, norm2_g[1], wr2, br,
                              t_lat, t_lat, nb, tm)
    out = _moe_block(x1, h2, logits, g2, moe_w1, moe_w3, moe_w2, 1, nb, tm, final_g)
    return out.reshape(nb, L, D)
```

```python
import functools
import math

import numpy as np
import jax
import jax.numpy as jnp
from jax import lax
from jax.experimental import pallas as pl
from jax.experimental.pallas import tpu as pltpu

F32 = jnp.float32
BF16 = jnp.bfloat16
HIGHEST = lax.Precision.HIGHEST

D = 1024
L = 4096
GRID_W = 64
GRID_H = L // GRID_W
CTX = 256
EPS = 1e-6

CONV_CH = 512
NA_HEADS = 8
NA_DH = 64
WIN_H = 8
WIN_W = 16
EVEN_IN = 3072

RET_HEADS = 4
RET_DK = 256
RET_DV = 512
RET_C = 256
ODD_IN = 6144
ROPE_BASE = 10000.0

N_GROUPS = 4
EPG = 8
EH = 256
GH = EPG * EH

NEG = -1e30

NA_QR = 4
NA_KR = NA_QR + WIN_H - 1
NA_NQ = NA_QR * GRID_W
NA_NK = NA_KR * GRID_W
NA_WS_MAX = GRID_H - NA_KR

TS = 512
TD = 512

VMEM_LIMIT = 56 * 1024 * 1024


def _cp(sem, vmem=VMEM_LIMIT):
    return pltpu.CompilerParams(dimension_semantics=sem, vmem_limit_bytes=vmem)


def _silu(v):
    return v * (1.0 / (1.0 + jnp.exp(-v)))


def _src_specs(src, tm, lat_tiles):
    tail = tuple(src[0].shape[1:])
    zeros = (0,) * len(tail)
    if len(src) == 1:
        return [pl.BlockSpec((tm,) + tail, lambda i, *_: (i,) + zeros)]
    return [pl.BlockSpec((tm,) + tail, lambda i, *_: (jnp.minimum(i, lat_tiles - 1),) + zeros),
            pl.BlockSpec((tm,) + tail, lambda i, *_: (jnp.maximum(i - lat_tiles, 0),) + zeros)]


def _src_load(refs, i, lat_tiles):
    if len(refs) == 1:
        return refs[0][...]
    return jnp.where(i < lat_tiles, refs[0][...], refs[1][...])


def _ada_kernel(c_ref, w_ref, b_ref, o_ref):
    s = _silu(c_ref[...])
    o_ref[...] = jnp.dot(s, w_ref[...], precision=HIGHEST, preferred_element_type=F32) + b_ref[...]


def _ada(c16, ada_w, ada_b):
    depth = ada_w.shape[0]
    tn = 1536
    return pl.pallas_call(
        _ada_kernel,
        grid=(depth, 6 * D // tn),
        in_specs=[
            pl.BlockSpec((16, D), lambda l, j: (0, 0)),
            pl.BlockSpec((None, D, tn), lambda l, j: (l, 0, j)),
            pl.BlockSpec((None, 1, tn), lambda l, j: (l, 0, j)),
        ],
        out_specs=pl.BlockSpec((None, 16, tn), lambda l, j: (l, 0, j)),
        out_shape=jax.ShapeDtypeStruct((depth, 16, 6 * D), F32),
        compiler_params=_cp(("arbitrary", "arbitrary")),
        name="ada_mod",
    )(c16, ada_w, ada_b.reshape(depth, 1, 6 * D))


def _inproj_kernel(*refs, rope, tn, n_src, lat_tiles):
    x_refs, refs = refs[:n_src], refs[n_src:]
    if rope:
        g_ref, sh_ref, sc_ref, w_ref, cos_ref, sin_ref, o_ref, h_scr = refs
    else:
        g_ref, sh_ref, sc_ref, w_ref, o_ref, h_scr = refs
    i = pl.program_id(0)
    j = pl.program_id(1)

    @pl.when(j == 0)
    def _():
        x = _src_load(x_refs, i, lat_tiles)
        ms = jnp.mean(x * x, axis=-1, keepdims=True)
        y = x * lax.rsqrt(ms + EPS) * g_ref[...]
        h_scr[...] = (y * (1.0 + sc_ref[...]) + sh_ref[...]).astype(BF16)

    acc = jnp.dot(h_scr[...], w_ref[...], preferred_element_type=F32)
    if not rope:
        o_ref[...] = acc.astype(o_ref.dtype)
        return

    @pl.when(j < 2)
    def _():
        cos = cos_ref[...]
        sin = sin_ref[...]
        scale = jnp.where(j == 1, RET_DK ** -0.5, 1.0).astype(F32)
        pieces = []
        for cblk in range(tn // 128):
            t = acc[:, cblk * 128:(cblk + 1) * 128]
            half = (cblk % 2) * 128
            r = pltpu.roll(t, 64, axis=1)
            pieces.append(t * cos[:, half:half + 128] + r * sin[:, half:half + 128])
        o_ref[...] = (jnp.concatenate(pieces, axis=1) * scale).astype(o_ref.dtype)

    @pl.when(j >= 2)
    def _():
        o_ref[...] = acc.astype(o_ref.dtype)


def _inproj(xsrc, gain, sh, sc, w, n_lat_rows, n_mod, tm, rope_tables=None):
    rows = sum(a.shape[0] for a in xsrc)
    n = w.shape[1]
    tn = 1024
    lat_tiles = n_lat_rows // tm
    mod_idx = lambda i, j: (jnp.minimum((i * tm) // L, n_mod), 0, 0)
    in_specs = _src_specs(xsrc, tm, lat_tiles) + [
        pl.BlockSpec((1, D), lambda i, j: (0, 0)),
        pl.BlockSpec((None, 1, D), mod_idx),
        pl.BlockSpec((None, 1, D), mod_idx),
        pl.BlockSpec((D, tn), lambda i, j: (0, j)),
    ]
    args = list(xsrc) + [gain.reshape(1, D), sh, sc, w]
    rope = rope_tables is not None
    if rope:
        per_seq = L // tm
        tab_idx = lambda i, j: (jnp.where(i < lat_tiles, i % per_seq, per_seq), 0)
        in_specs += [pl.BlockSpec((tm, RET_DK), tab_idx), pl.BlockSpec((tm, RET_DK), tab_idx)]
        args += list(rope_tables)
    return pl.pallas_call(
        functools.partial(_inproj_kernel, rope=rope, tn=tn, n_src=len(xsrc), lat_tiles=lat_tiles),
        grid=(rows // tm, n // tn),
        in_specs=in_specs,
        out_specs=pl.BlockSpec((tm, tn), lambda i, j: (i, j)),
        out_shape=jax.ShapeDtypeStruct((rows, n), BF16),
        scratch_shapes=[pltpu.VMEM((tm, D), BF16)],
        compiler_params=_cp(("arbitrary", "arbitrary")),
        name="inproj_rope" if rope else "inproj",
    )(*args)


def _inproj_comb_kernel(slot_ref, slotn_ref, x1_ref, g2_ref, g_ref, sh_ref, sc_ref, w_ref, cos_ref, sin_ref, ys_ref,
                        o_ref, xa_ref, h_scr, buf, sem, *, tn, tm, n_tiles):
    i = pl.program_id(0)
    j = pl.program_id(1)

    def issue(sref, b):
        def body(jj, carry):
            for u in range(2):
                r = 2 * jj + u
                pltpu.make_async_copy(ys_ref.at[sref[0, r]], buf.at[b, r], sem.at[b]).start(priority=u)
            return carry

        lax.fori_loop(0, tm // 2, body, 0)

    @pl.when(j == 0)
    def _():
        @pl.when(i == 0)
        def _():
            issue(slot_ref, 0)

        b = i % 2
        pltpu.make_async_copy(ys_ref.at[pl.ds(0, tm)], buf.at[b], sem.at[b]).wait()

        @pl.when(i + 1 < n_tiles)
        def _():
            issue(slotn_ref, (i + 1) % 2)

        y = jnp.concatenate([buf[b, :, k, :] for k in range(D // 128)], axis=1)
        x = x1_ref[...] + g2_ref[...] * y
        xa_ref[...] = x
        ms = jnp.mean(x * x, axis=-1, keepdims=True)
        hn = x * lax.rsqrt(ms + EPS) * g_ref[...]
        h_scr[...] = (hn * (1.0 + sc_ref[...]) + sh_ref[...]).astype(BF16)

    acc = jnp.dot(h_scr[...], w_ref[...], preferred_element_type=F32)

    @pl.when(j < 2)
    def _():
        cos = cos_ref[...]
        sin = sin_ref[...]
        scale = jnp.where(j == 1, RET_DK ** -0.5, 1.0).astype(F32)
        pieces = []
        for cblk in range(tn // 128):
            t = acc[:, cblk * 128:(cblk + 1) * 128]
            half = (cblk % 2) * 128
            r = pltpu.roll(t, 64, axis=1)
            pieces.append(t * cos[:, half:half + 128] + r * sin[:, half:half + 128])
        o_ref[...] = (jnp.concatenate(pieces, axis=1) * scale).astype(o_ref.dtype)

    @pl.when(j >= 2)
    def _():
        o_ref[...] = acc.astype(o_ref.dtype)


def _inproj_comb(slot3, ys, x1, g2, gain, sh, sc, w, n_lat_rows, n_mod, tm, rope_tables):
    rows = x1.shape[0]
    n = w.shape[1]
    tn = 1024
    n_tiles = rows // tm
    lat_tiles = n_lat_rows // tm
    per_seq = L // tm
    mod_idx = lambda i, j: (jnp.minimum((i * tm) // L, n_mod), 0, 0)
    tab_idx = lambda i, j: (jnp.where(i < lat_tiles, i % per_seq, per_seq), 0)
    in_specs = [_slot_spec(slot3, tm), _slot_spec(slot3, tm, ahead=1),
                pl.BlockSpec((tm, D), lambda i, j: (i, 0)),
                pl.BlockSpec((None, 1, D), mod_idx),
                pl.BlockSpec((1, D), lambda i, j: (0, 0)),
                pl.BlockSpec((None, 1, D), mod_idx),
                pl.BlockSpec((None, 1, D), mod_idx),
                pl.BlockSpec((D, tn), lambda i, j: (0, j)),
                pl.BlockSpec((tm, RET_DK), tab_idx),
                pl.BlockSpec((tm, RET_DK), tab_idx),
                pl.BlockSpec(memory_space=pl.ANY)]
    return pl.pallas_call(
        functools.partial(_inproj_comb_kernel, tn=tn, tm=tm, n_tiles=n_tiles),
        grid=(n_tiles, n // tn),
        in_specs=in_specs,
        out_specs=[pl.BlockSpec((tm, tn), lambda i, j: (i, j)),
                   pl.BlockSpec((tm, D), lambda i, j: (i, 0))],
        out_shape=[jax.ShapeDtypeStruct((rows, n), BF16),
                   jax.ShapeDtypeStruct((rows, D), F32)],
        scratch_shapes=[pltpu.VMEM((tm, D), BF16), pltpu.VMEM((2, tm, D // 128, 128), F32),
                        pltpu.SemaphoreType.DMA((2,))],
        compiler_params=_cp(("arbitrary", "arbitrary")),
        name="inproj_rope_combine",
    )(slot3, slot3, x1, g2, gain.reshape(1, D), sh, sc, w, rope_tables[0], rope_tables[1], ys)


def _rope_tables(tm):
    pos = np.arange(L)
    half = RET_DK // 4
    freqs = ROPE_BASE ** (-np.arange(half, dtype=np.float64) / half)
    ang_r = (pos // GRID_W)[:, None] * freqs[None, :]
    ang_c = (pos % GRID_W)[:, None] * freqs[None, :]
    cos = np.concatenate([np.cos(ang_r), np.cos(ang_r), np.cos(ang_c), np.cos(ang_c)], axis=1)
    sin = np.concatenate([-np.sin(ang_r), np.sin(ang_r), -np.sin(ang_c), np.sin(ang_c)], axis=1)
    cos = np.concatenate([cos, np.ones((tm, RET_DK))], axis=0)
    sin = np.concatenate([sin, np.zeros((tm, RET_DK))], axis=0)
    return jnp.asarray(cos, F32), jnp.asarray(sin, F32)


def _conv_kernel(ab_ref, ac_ref, ax_ref, w_ref, o_ref, *, seq):
    u = ac_ref[...].astype(F32) * ax_ref[...].astype(F32)
    row = lax.broadcasted_iota(jnp.int32, u.shape, 0)
    up = jnp.where(row == 0, 0.0, pltpu.roll(u, 1, axis=0))
    un = jnp.where(row == seq - 1, 0.0, pltpu.roll(u, seq - 1, axis=0))
    w = w_ref[...]
    y = ab_ref[...].astype(F32) * (up * w[0:1, :] + u * w[1:2, :] + un * w[2:3, :])
    o_ref[...] = y.astype(o_ref.dtype)


def _conv(qkv, conv_w, seq, n_seq, row_blk0):
    nb = CONV_CH // 128
    return pl.pallas_call(
        functools.partial(_conv_kernel, seq=seq),
        grid=(n_seq, nb),
        in_specs=[
            pl.BlockSpec((seq, 128), lambda b, j: (row_blk0 + b, j)),
            pl.BlockSpec((seq, 128), lambda b, j: (row_blk0 + b, nb + j)),
            pl.BlockSpec((seq, 128), lambda b, j: (row_blk0 + b, 2 * nb + j)),
            pl.BlockSpec((3, 128), lambda b, j: (0, j)),
        ],
        out_specs=pl.BlockSpec((seq, 128), lambda b, j: (b, j)),
        out_shape=jax.ShapeDtypeStruct((n_seq * seq, CONV_CH), BF16),
        compiler_params=_cp(("arbitrary", "arbitrary")),
        name="short_conv",
    )(qkv, qkv, qkv, conv_w)


def _natten_bias(rpb):
    qc = np.arange(GRID_W)
    kc = np.arange(GRID_W)
    cs = np.clip(qc - WIN_W // 2, 0, GRID_W - WIN_W)
    col_ok = (kc[None, :] >= cs[:, None]) & (kc[None, :] < cs[:, None] + WIN_W)
    dx = np.clip(kc[None, :] - qc[:, None] + WIN_W - 1, 0, 2 * WIN_W - 2)
    tcol = jnp.where(col_ok[None, None], rpb.astype(F32)[:, :, dx], NEG)
    dys, oks = [], []
    for r0 in (0, 2 * NA_QR, GRID_H - NA_QR):
        ws = int(np.clip(r0 - WIN_H // 2, 0, NA_WS_MAX))
        qr = r0 + np.arange(NA_QR)
        kr = ws + np.arange(NA_KR)
        rs = np.clip(qr - WIN_H // 2, 0, GRID_H - WIN_H)
        oks.append((kr[None, :] >= rs[:, None]) & (kr[None, :] < rs[:, None] + WIN_H))
        dys.append(np.clip(kr[None, :] - qr[:, None] + WIN_H - 1, 0, 2 * WIN_H - 2))
    dy = np.stack(dys)
    ok = np.stack(oks)
    b = jnp.take(tcol, jnp.asarray(dy.reshape(-1)), axis=1)
    b = b.reshape(NA_HEADS, 3, NA_QR, NA_KR, GRID_W, GRID_W)
    b = jnp.where(ok[None, :, :, :, None, None], b, NEG)
    b = b.transpose(0, 1, 2, 4, 3, 5).reshape(NA_HEADS // 2, 2, 3, NA_NQ, NA_NK)
    return b.transpose(0, 2, 1, 3, 4)


def _attend(qa, kw, vw, kc, vc, bias):
    dn = (((1,), (1,)), ((), ()))
    s2 = lax.dot_general(qa, kc, dn, preferred_element_type=F32)
    m = jnp.max(s2, axis=1, keepdims=True)
    if kw is not None:
        s1 = lax.dot_general(qa, kw, dn, preferred_element_type=F32) + bias
        m = jnp.maximum(m, jnp.max(s1, axis=1, keepdims=True))
        p1 = jnp.exp(s1 - m)
    p2 = jnp.exp(s2 - m)
    den = jnp.sum(p2, axis=1, keepdims=True)
    o = jnp.dot(p2.astype(BF16), vc, preferred_element_type=F32)
    if kw is not None:
        den = den + jnp.sum(p1, axis=1, keepdims=True)
        o = o + jnp.dot(p1.astype(BF16), vw, preferred_element_type=F32)
    return o / den


NA_SUB = 2


def _natten_kernel(q_ref, k_ref, v_ref, kc_ref, vc_ref, bias_ref, o_ref):
    n_rb = GRID_H // NA_QR
    lane = lax.broadcasted_iota(jnp.int32, (NA_NQ, 128), 1)
    for sb in range(NA_SUB):
        rb = pl.program_id(2) * NA_SUB + sb
        ws = jnp.clip(rb * NA_QR - WIN_H // 2, 0, NA_WS_MAX)
        start = pl.multiple_of(ws * GRID_W, GRID_W)
        kw = k_ref[pl.ds(start, NA_NK), :]
        vw = v_ref[pl.ds(start, NA_NK), :]
        cls = jnp.where(rb == 0, 0, jnp.where(rb == n_rb - 1, 2, 1))
        q = q_ref[sb * NA_NQ:(sb + 1) * NA_NQ, :] * (NA_DH ** -0.5)
        zero = jnp.zeros_like(q)
        o0 = _attend(jnp.where(lane < NA_DH, q, zero), kw, vw, kc_ref[...], vc_ref[...], bias_ref[cls, 0])
        o1 = _attend(jnp.where(lane >= NA_DH, q, zero), kw, vw, kc_ref[...], vc_ref[...], bias_ref[cls, 1])
        o_ref[sb * NA_NQ:(sb + 1) * NA_NQ, :] = jnp.where(lane < NA_DH, o0, o1).astype(o_ref.dtype)


def _natten(qkv, bias, nb):
    n_steps = GRID_H // NA_QR // NA_SUB
    qcol, kcol, vcol = 3 * CONV_CH // 128, 3 * CONV_CH // 128 + 4, 3 * CONV_CH // 128 + 8
    ctx_blk0 = nb * L // CTX
    return pl.pallas_call(
        _natten_kernel,
        grid=(nb, NA_HEADS // 2, n_steps),
        in_specs=[
            pl.BlockSpec((NA_SUB * NA_NQ, 128), lambda b, h, r: (b * n_steps + r, qcol + h)),
            pl.BlockSpec((L, 128), lambda b, h, r: (b, kcol + h)),
            pl.BlockSpec((L, 128), lambda b, h, r: (b, vcol + h)),
            pl.BlockSpec((CTX, 128), lambda b, h, r: (ctx_blk0 + b, kcol + h)),
            pl.BlockSpec((CTX, 128), lambda b, h, r: (ctx_blk0 + b, vcol + h)),
            pl.BlockSpec((None, 3, 2, NA_NQ, NA_NK), lambda b, h, r: (h, 0, 0, 0, 0)),
        ],
        out_specs=pl.BlockSpec((NA_SUB * NA_NQ, 128), lambda b, h, r: (b * n_steps + r, h)),
        out_shape=jax.ShapeDtypeStruct((nb * L, NA_HEADS * NA_DH), BF16),
        compiler_params=_cp(("arbitrary", "arbitrary", "arbitrary")),
        name="natten",
    )(qkv, qkv, qkv, qkv, qkv, bias)


def _ctx_attn_kernel(q_ref, kc_ref, vc_ref, o_ref):
    q = q_ref[...] * (NA_DH ** -0.5)
    lane = lax.broadcasted_iota(jnp.int32, q.shape, 1)
    zero = jnp.zeros_like(q)
    o0 = _attend(jnp.where(lane < NA_DH, q, zero), None, None, kc_ref[...], vc_ref[...], None)
    o1 = _attend(jnp.where(lane >= NA_DH, q, zero), None, None, kc_ref[...], vc_ref[...], None)
    o_ref[...] = jnp.where(lane < NA_DH, o0, o1).astype(o_ref.dtype)


def _ctx_attn(qkv, nb):
    qcol, kcol, vcol = 3 * CONV_CH // 128, 3 * CONV_CH // 128 + 4, 3 * CONV_CH // 128 + 8
    ctx_blk0 = nb * L // CTX
    return pl.pallas_call(
        _ctx_attn_kernel,
        grid=(nb, NA_HEADS // 2),
        in_specs=[
            pl.BlockSpec((CTX, 128), lambda b, h: (ctx_blk0 + b, qcol + h)),
            pl.BlockSpec((CTX, 128), lambda b, h: (ctx_blk0 + b, kcol + h)),
            pl.BlockSpec((CTX, 128), lambda b, h: (ctx_blk0 + b, vcol + h)),
        ],
        out_specs=pl.BlockSpec((CTX, 128), lambda b, h: (b, h)),
        out_shape=jax.ShapeDtypeStruct((nb * CTX, NA_HEADS * NA_DH), BF16),
        compiler_params=_cp(("arbitrary", "arbitrary")),
        name="ctx_attn",
    )(qkv, qkv, qkv)


def _retention_kernel(lg_ref, q_ref, k_ref, v_ref, g_ref, kc_ref, vc_ref, o_ref, y_scr, stf_scr, stb_scr):
    h = pl.program_id(1)
    n_chunks = L // RET_C
    pos_r = lax.broadcasted_iota(jnp.int32, (RET_C, 1), 0).astype(F32)
    rel = pos_r - lax.broadcasted_iota(jnp.int32, (1, RET_C), 1).astype(F32)
    dn_t = (((1,), (1,)), ((), ()))
    dn_k = (((0,), (0,)), ((), ()))

    def decays(lg, fwd):
        if fwd:
            d_intra = jnp.where(rel >= 0, jnp.exp(lg * jnp.maximum(rel, 0.0)), 0.0)
            d_q = jnp.exp(lg * (pos_r + 1.0))
            d_k = jnp.exp(lg * (RET_C - 1.0 - pos_r))
        else:
            d_intra = jnp.where(rel <= 0, jnp.exp(lg * jnp.maximum(-rel, 0.0)), 0.0)
            d_q = jnp.exp(lg * (RET_C - pos_r))
            d_k = jnp.exp(lg * pos_r)
        d_c = jnp.exp(jnp.full((1, 1), RET_C, F32) * lg)
        return d_intra, d_q, d_k, d_c

    def ctx_state(st_scr, d_k, d_c, order):
        st_scr[...] = jnp.zeros_like(st_scr)
        for jc in order:
            kd = (kc_ref[jc * RET_C:(jc + 1) * RET_C, :].astype(F32) * d_k).astype(BF16)
            vc = vc_ref[jc * RET_C:(jc + 1) * RET_C, :]
            st_scr[...] = st_scr[...] * d_c + lax.dot_general(kd, vc, dn_k, preferred_element_type=F32)

    def chunk_out(st_scr, c, d_intra, d_q, d_k, d_c):
        r0 = pl.multiple_of(c * RET_C, RET_C)
        q = q_ref[pl.ds(r0, RET_C), :]
        k = k_ref[pl.ds(r0, RET_C), :]
        v = v_ref[pl.ds(r0, RET_C), :]
        s = lax.dot_general(q, k, dn_t, preferred_element_type=F32) * d_intra
        st = st_scr[...]
        o = (jnp.dot(s.astype(BF16), v, preferred_element_type=F32)
             + jnp.dot(q, st.astype(BF16), preferred_element_type=F32) * d_q)
        kd = (k.astype(F32) * d_k).astype(BF16)
        st_scr[...] = st * d_c + lax.dot_general(kd, v, dn_k, preferred_element_type=F32)
        return r0, o

    def finish(r0, y):
        y = y * lax.rsqrt(jnp.mean(y * y, axis=-1, keepdims=True) + EPS)
        g = g_ref[pl.ds(r0, RET_C), :].astype(F32)
        o_ref[pl.ds(r0, RET_C), :] = (_silu(g) * y).astype(o_ref.dtype)

    dec_f = decays(lg_ref[0, h], True)
    dec_b = decays(lg_ref[1, h], False)
    ctx_state(stf_scr, dec_f[2], dec_f[3], range(CTX // RET_C))
    ctx_state(stb_scr, dec_b[2], dec_b[3], range(CTX // RET_C - 1, -1, -1))

    def first_half(i, carry):
        r0, o = chunk_out(stf_scr, i, *dec_f)
        y_scr[pl.ds(r0, RET_C), :] = o
        r1, o = chunk_out(stb_scr, n_chunks - 1 - i, *dec_b)
        y_scr[pl.ds(r1, RET_C), :] = o
        return carry

    lax.fori_loop(0, n_chunks // 2, first_half, 0)

    def second_half(i, carry):
        r0, o = chunk_out(stf_scr, i, *dec_f)
        finish(r0, y_scr[pl.ds(r0, RET_C), :] + o)
        r1, o = chunk_out(stb_scr, n_chunks - 1 - i, *dec_b)
        finish(r1, y_scr[pl.ds(r1, RET_C), :] + o)
        return carry

    lax.fori_loop(n_chunks // 2, n_chunks, second_half, 0)


def _retention(qkvg, lg, nb):
    ctx_blk0 = nb * L // CTX
    kq = RET_HEADS
    vq = 2 * RET_HEADS * RET_DK // RET_DV
    gq = vq + RET_HEADS
    return pl.pallas_call(
        _retention_kernel,
        grid=(nb, RET_HEADS),
        in_specs=[
            pl.BlockSpec(memory_space=pltpu.SMEM),
            pl.BlockSpec((L, RET_DK), lambda b, h: (b, h)),
            pl.BlockSpec((L, RET_DK), lambda b, h: (b, kq + h)),
            pl.BlockSpec((L, RET_DV), lambda b, h: (b, vq + h)),
            pl.BlockSpec((L, RET_DV), lambda b, h: (b, gq + h)),
            pl.BlockSpec((CTX, RET_DK), lambda b, h: (ctx_blk0 + b, kq + h)),
            pl.BlockSpec((CTX, RET_DV), lambda b, h: (ctx_blk0 + b, vq + h)),
        ],
        out_specs=pl.BlockSpec((L, RET_DV), lambda b, h: (b, h)),
        out_shape=jax.ShapeDtypeStruct((nb * L, RET_HEADS * RET_DV), BF16),
        scratch_shapes=[pltpu.VMEM((L, RET_DV), F32), pltpu.VMEM((RET_DK, RET_DV), F32),
                        pltpu.VMEM((RET_DK, RET_DV), F32)],
        compiler_params=_cp(("arbitrary", "arbitrary")),
        name="retention",
    )(lg, qkvg, qkvg, qkvg, qkvg, qkvg, qkvg)


def _outproj_kernel(*refs, src_lens, lat_tiles):
    i = pl.program_id(0)
    srcs = []
    for n in src_lens:
        srcs.append(refs[:n])
        refs = refs[n:]
    n_act = len(src_lens) - 1
    w_refs, refs = refs[:n_act], refs[n_act:]
    g1_ref, sh2_ref, sc2_ref, n2_ref, wr_ref, br_ref, x1_ref, h2_ref, lg_ref = refs
    y = None
    for a_src, w_ref in zip(srcs[:n_act], w_refs):
        t = jnp.dot(_src_load(a_src, i, lat_tiles), w_ref[...], preferred_element_type=F32)
        y = t if y is None else y + t
    x1 = _src_load(srcs[n_act], i, lat_tiles) + g1_ref[...] * y
    x1_ref[...] = x1
    ms = jnp.mean(x1 * x1, axis=-1, keepdims=True)
    h2 = x1 * lax.rsqrt(ms + EPS) * n2_ref[...] * (1.0 + sc2_ref[...]) + sh2_ref[...]
    for k in range(D // 128):
        h2_ref[:, k, :] = h2[:, k * 128:(k + 1) * 128]
    h_hi = h2.astype(BF16)
    h_lo = (h2 - h_hi.astype(F32)).astype(BF16)
    p = jnp.dot(h_hi, wr_ref[...], preferred_element_type=F32)
    q = jnp.dot(h_lo, wr_ref[:, :128], preferred_element_type=F32)
    lg_ref[...] = p[:, :128] + p[:, 128:] + q + br_ref[...]


def _outproj(acts, ws, xsrc, g1, sh2, sc2, n2g, wr2, br, rows, n_lat_rows, n_mod, tm):
    lat_tiles = n_lat_rows // tm
    mod_idx = lambda i: (jnp.minimum((i * tm) // L, n_mod), 0, 0)
    in_specs, args = [], []
    for src in list(acts) + [xsrc]:
        in_specs += _src_specs(src, tm, lat_tiles)
        args += list(src)
    in_specs += [pl.BlockSpec(w.shape, lambda i: (0, 0)) for w in ws]
    in_specs += [pl.BlockSpec((None, 1, D), mod_idx),
                 pl.BlockSpec((None, 1, D), mod_idx),
                 pl.BlockSpec((None, 1, D), mod_idx),
                 pl.BlockSpec((1, D), lambda i: (0, 0)),
                 pl.BlockSpec((D, 256), lambda i: (0, 0)),
                 pl.BlockSpec((1, 128), lambda i: (0, 0))]
    args += list(ws) + [g1, sh2, sc2, n2g.reshape(1, D), wr2, br]
    return pl.pallas_call(
        functools.partial(_outproj_kernel, src_lens=tuple(len(s) for s in list(acts) + [xsrc]),
                          lat_tiles=lat_tiles),
        grid=(rows // tm,),
        in_specs=in_specs,
        out_specs=[pl.BlockSpec((tm, D), lambda i: (i, 0)),
                   pl.BlockSpec((tm, D // 128, 128), lambda i: (i, 0, 0)),
                   pl.BlockSpec((tm, 128), lambda i: (i, 0))],
        out_shape=[jax.ShapeDtypeStruct((rows, D), F32),
                   jax.ShapeDtypeStruct((rows, D // 128, 128), F32),
                   jax.ShapeDtypeStruct((rows, 128), F32)],
        compiler_params=_cp(("arbitrary",)),
        name="outproj_router",
    )(*args)


def _router_kernel(lg_ref, tri_ref, o_ref, rt_ref, cnt_ref, carry_scr):
    i = pl.program_id(0)

    @pl.when(i == 0)
    def _():
        carry_scr[...] = jnp.zeros_like(carry_scr)

    lg = lg_ref[...]
    lane = lax.broadcasted_iota(jnp.int32, lg.shape, 1)
    lane_f = lane.astype(F32)
    big = jnp.float32(1e9)

    gmask = lane < N_GROUPS
    gl = jnp.where(gmask, lg, NEG)
    gm = jnp.max(gl, axis=1, keepdims=True)
    gidx = jnp.min(jnp.where(gl == gm, lane_f, big), axis=1, keepdims=True)
    g_val = 1.0 / jnp.sum(jnp.where(gmask, jnp.exp(gl - gm), 0.0), axis=1, keepdims=True)

    base = 8.0 + 8.0 * gidx
    emask = (lane_f >= base) & (lane_f < base + EPG)
    el = jnp.where(emask, lg, NEG)
    m1 = jnp.max(el, axis=1, keepdims=True)
    i1 = jnp.min(jnp.where(el == m1, lane_f, big), axis=1, keepdims=True)
    el2 = jnp.where(lane_f == i1, NEG, el)
    m2 = jnp.max(el2, axis=1, keepdims=True)
    i2 = jnp.min(jnp.where(el2 == m2, lane_f, big), axis=1, keepdims=True)
    r = jnp.exp(m2 - m1)
    w1 = g_val / (1.0 + r)
    w2 = w1 * r
    gates = jnp.where(lane_f == i1 - base, w1, jnp.where(lane_f == i2 - base, w2, 0.0))

    onehot = jnp.where(lane_f == gidx, 1.0, 0.0)
    before = jnp.dot(tri_ref[...], onehot.astype(BF16), preferred_element_type=F32)
    carry = carry_scr[0:1, :]
    rank = jnp.sum(jnp.where(lane_f == gidx, before + carry, 0.0), axis=1, keepdims=True)
    carry_scr[0:1, :] = carry + jnp.sum(onehot, axis=0, keepdims=True)

    out = jnp.where(lane < EPG, gates, jnp.where(lane == 8, gidx, jnp.where(lane == 9, rank, 0.0)))
    o_ref[...] = out
    rt_ref[...] = out.T[8:16, :]
    cnt_ref[...] = carry_scr[...]


def _router(logits, tm):
    rows = logits.shape[0]
    tri = jnp.asarray(np.tril(np.ones((tm, tm), np.float32), -1), BF16)
    return pl.pallas_call(
        _router_kernel,
        grid=(rows // tm,),
        in_specs=[pl.BlockSpec((tm, 128), lambda i: (i, 0)),
                  pl.BlockSpec((tm, tm), lambda i: (0, 0))],
        out_specs=[pl.BlockSpec((tm, 128), lambda i: (i, 0)),
                   pl.BlockSpec((None, 8, tm), lambda i: (i, 0, 0)),
                   pl.BlockSpec((8, 128), lambda i: (0, 0))],
        out_shape=[jax.ShapeDtypeStruct((rows, 128), F32),
                   jax.ShapeDtypeStruct((rows // tm, 8, tm), F32),
                   jax.ShapeDtypeStruct((8, 128), F32)],
        scratch_shapes=[pltpu.VMEM((8, 128), F32)],
        compiler_params=_cp(("arbitrary",)),
        name="router",
    )(logits, tri)


def _slot_kernel(rt_ref, st_ref, o_ref):
    g = rt_ref[0:1, :]
    slot = rt_ref[1:2, :]
    st = st_ref[...]
    for k in range(N_GROUPS):
        slot = slot + jnp.where(g == k, st[:, k:k + 1], 0.0)
    o_ref[...] = slot.astype(jnp.int32)


def _slots(route_t, starts_row):
    n, _, st = route_t.shape
    return pl.pallas_call(
        _slot_kernel,
        grid=(n,),
        in_specs=[pl.BlockSpec((None, 8, st), lambda i: (i, 0, 0)),
                  pl.BlockSpec((1, 128), lambda i: (0, 0))],
        out_specs=pl.BlockSpec((None, 1, st), lambda i: (i, 0, 0)),
        out_shape=jax.ShapeDtypeStruct((n, 1, st), jnp.int32),
        compiler_params=_cp(("arbitrary",)),
        name="moe_slots",
    )(route_t, starts_row)


def _slot_spec(slot3, tile, ahead=0):
    n, _, st = slot3.shape
    per = st // tile
    last = n * per - 1

    def idx(i, *_):
        t = jnp.minimum(i + ahead, last)
        return (t // per, 0, t % per)

    return pl.BlockSpec((None, 1, tile), idx, memory_space=pltpu.SMEM)


def _dispatch_kernel(slot_ref, pad_ref, h2_ref, gt_ref, xs_ref, gs_ref, zx_scr, zg_scr, sem):
    i = pl.program_id(0)
    n_pad = pad_ref.shape[1]

    @pl.when(i == 0)
    def _():
        zx_scr[...] = jnp.zeros_like(zx_scr)
        zg_scr[...] = jnp.zeros_like(zg_scr)

        def zbody(r, carry):
            s = pad_ref[0, r]
            pltpu.make_async_copy(zx_scr, xs_ref.at[s], sem.at[0]).start()
            pltpu.make_async_copy(zg_scr.at[pl.ds(0, 1)], gs_ref.at[pl.ds(s, 1)], sem.at[1]).start()
            return carry

        lax.fori_loop(0, n_pad, zbody, 0)
        pltpu.make_async_copy(xs_ref.at[pl.ds(0, n_pad)], xs_ref.at[pl.ds(0, n_pad)], sem.at[0]).wait()
        pltpu.make_async_copy(gs_ref.at[pl.ds(0, n_pad)], gs_ref.at[pl.ds(0, n_pad)], sem.at[1]).wait()

    def body(j, carry):
        for u in range(2):
            r = 2 * j + u
            s = slot_ref[0, r]
            pltpu.make_async_copy(h2_ref.at[r], xs_ref.at[s], sem.at[0]).start(priority=u)
            pltpu.make_async_copy(gt_ref.at[pl.ds(r, 1)], gs_ref.at[pl.ds(s, 1)], sem.at[1]).start(priority=1 - u)
        return carry

    lax.fori_loop(0, TD // 2, body, 0)
    pltpu.make_async_copy(h2_ref, xs_ref.at[pl.ds(0, TD)], sem.at[0]).wait()
    pltpu.make_async_copy(gt_ref, gs_ref.at[pl.ds(0, TD)], sem.at[1]).wait()


def _dispatch(slot3, pad_slots, h2, gates, n_sorted):
    rows = h2.shape[0]
    n_pad = pad_slots.shape[0]
    return pl.pallas_call(
        _dispatch_kernel,
        grid=(rows // TD,),
        in_specs=[_slot_spec(slot3, TD),
                  pl.BlockSpec((1, n_pad), lambda i: (0, 0), memory_space=pltpu.SMEM),
                  pl.BlockSpec((TD, D // 128, 128), lambda i: (i, 0, 0)),
                  pl.BlockSpec((TD, 128), lambda i: (i, 0))],
        out_specs=[pl.BlockSpec(memory_space=pl.ANY), pl.BlockSpec(memory_space=pl.ANY)],
        out_shape=[jax.ShapeDtypeStruct((n_sorted, D // 128, 128), F32),
                   jax.ShapeDtypeStruct((n_sorted, 128), F32)],
        scratch_shapes=[pltpu.VMEM((D // 128, 128), F32), pltpu.VMEM((8, 128), F32),
                        pltpu.SemaphoreType.DMA((2,))],
        compiler_params=_cp(("arbitrary",)),
        name="moe_dispatch",
    )(slot3, pad_slots.reshape(1, n_pad), h2, gates)


def _moe_kernel(tg_ref, na_ref, x_ref, g_ref, w1_ref, w3_ref, w2_ref, o_ref, hid_scr, w1b_scr, w3b_scr, w2b_scr):
    i = pl.program_id(0)

    @pl.when((i == 0) | (tg_ref[i] != tg_ref[jnp.maximum(i - 1, 0)]))
    def _():
        for e in range(EPG):
            cols = slice(e * EH, (e + 1) * EH)
            w1b_scr[:, cols] = w1_ref[:, cols].astype(BF16)
            w3b_scr[:, cols] = w3_ref[:, cols].astype(BF16)
            w2b_scr[cols, :] = w2_ref[cols, :].astype(BF16)

    @pl.when(i < na_ref[0])
    def _():
        x = jnp.concatenate([x_ref[:, k, :] for k in range(D // 128)], axis=1).astype(BF16)
        gt = g_ref[...]
        for e in range(EPG):
            cols = slice(e * EH, (e + 1) * EH)
            h1 = jnp.dot(x, w1b_scr[:, cols], preferred_element_type=F32)
            h3 = jnp.dot(x, w3b_scr[:, cols], preferred_element_type=F32)
            hid_scr[:, cols] = (_silu(h1) * h3 * gt[:, e:e + 1]).astype(BF16)
        y = jnp.dot(hid_scr[...], w2b_scr[...], preferred_element_type=F32)
        for k in range(D // 128):
            o_ref[:, k, :] = y[:, k * 128:(k + 1) * 128]

    @pl.when(i >= na_ref[0])
    def _():
        o_ref[...] = jnp.zeros_like(o_ref)


def _moe(tile_group, n_active, xs, gs, w1, w3, w2, layer, n_tiles):
    x_idx = lambda i, tg, na: (jnp.minimum(i, na[0] - 1), 0, 0)
    g_idx = lambda i, tg, na: (jnp.minimum(i, na[0] - 1), 0)
    out_idx = lambda i, tg, na: (i, 0, 0)
    w_idx = lambda i, tg, na: (layer, tg[i], 0, 0)
    once = pl.Buffered(1)
    grid_spec = pltpu.PrefetchScalarGridSpec(
        num_scalar_prefetch=2,
        grid=(n_tiles,),
        in_specs=[pl.BlockSpec((TS, D // 128, 128), x_idx),
                  pl.BlockSpec((TS, 128), g_idx),
                  pl.BlockSpec((None, None, D, GH), w_idx, pipeline_mode=once),
                  pl.BlockSpec((None, None, D, GH), w_idx, pipeline_mode=once),
                  pl.BlockSpec((None, None, GH, D), w_idx, pipeline_mode=once)],
        out_specs=pl.BlockSpec((TS, D // 128, 128), out_idx),
        scratch_shapes=[pltpu.VMEM((TS, GH), BF16), pltpu.VMEM((D, GH), BF16), pltpu.VMEM((D, GH), BF16),
                        pltpu.VMEM((GH, D), BF16)],
    )
    return pl.pallas_call(
        _moe_kernel,
        grid_spec=grid_spec,
        out_shape=jax.ShapeDtypeStruct(xs.shape, F32),
        compiler_params=_cp(("arbitrary",)),
        name="moe_experts",
    )(tile_group, n_active, xs, gs, w1, w3, w2)


def _combine_kernel(*refs, final, n_steps):
    if final:
        slot_ref, slot_next_ref, x1_ref, g2_ref, fg_ref, ys_ref, o_ref, buf, sem = refs
    else:
        slot_ref, slot_next_ref, x1_ref, g2_ref, ys_ref, o_ref, buf, sem = refs
    i = pl.program_id(0)

    def issue(sref, b):
        def body(j, carry):
            for u in range(2):
                r = 2 * j + u
                pltpu.make_async_copy(ys_ref.at[sref[0, r]], buf.at[b, r], sem.at[b]).start(priority=u)
            return carry

        lax.fori_loop(0, TD // 2, body, 0)

    @pl.when(i == 0)
    def _():
        issue(slot_ref, 0)

    @pl.when(i + 1 < n_steps)
    def _():
        issue(slot_next_ref, (i + 1) % 2)

    b = i % 2
    pltpu.make_async_copy(ys_ref.at[pl.ds(0, TD)], buf.at[b], sem.at[b]).wait()

    y = jnp.concatenate([buf[b, :, k, :] for k in range(D // 128)], axis=1)
    x2 = x1_ref[...] + g2_ref[...] * y
    if final:
        ms = jnp.mean(x2 * x2, axis=-1, keepdims=True)
        x2 = x2 * lax.rsqrt(ms + EPS) * fg_ref[...]
    o_ref[...] = x2


def _combine(slot3, x1, g2, ys, n_mod, final_g=None):
    rows = x1.shape[0]
    n_steps = rows // TD
    final = final_g is not None
    mod_idx = lambda i: (jnp.minimum((i * TD) // L, n_mod), 0, 0)
    in_specs = [_slot_spec(slot3, TD), _slot_spec(slot3, TD, ahead=1),
                pl.BlockSpec((TD, D), lambda i: (i, 0)),
                pl.BlockSpec((None, 1, D), mod_idx)]
    args = [slot3, slot3, x1, g2]
    if final:
        in_specs.append(pl.BlockSpec((1, D), lambda i: (0, 0)))
        args.append(final_g.reshape(1, D))
    in_specs.append(pl.BlockSpec(memory_space=pl.ANY))
    args.append(ys)
    return pl.pallas_call(
        functools.partial(_combine_kernel, final=final, n_steps=n_steps),
        grid=(n_steps,),
        in_specs=in_specs,
        out_specs=pl.BlockSpec((TD, D), lambda i: (i, 0)),
        out_shape=jax.ShapeDtypeStruct((rows, D), F32),
        scratch_shapes=[pltpu.VMEM((2, TD, D // 128, 128), F32), pltpu.SemaphoreType.DMA((2,))],
        compiler_params=_cp(("arbitrary",)),
        name="moe_combine_final" if final else "moe_combine",
    )(*args)


def _sort_plan(counts_row, rows):
    counts = counts_row[:N_GROUPS].astype(jnp.int32)
    padded = ((counts + TS - 1) // TS) * TS
    ends = jnp.cumsum(padded)
    starts = ends - padded
    starts_row = jnp.zeros((1, 128), F32).at[0, :N_GROUPS].set(starts.astype(F32))
    n_tiles = rows // TS + N_GROUPS
    n_active = (ends[-1] // TS).reshape(1)
    tile_start = jnp.arange(n_tiles, dtype=jnp.int32) * TS
    tile_group = jnp.minimum(jnp.sum(tile_start[:, None] >= ends[None, :], axis=1), N_GROUPS - 1).astype(jnp.int32)
    r = jnp.arange(TS, dtype=jnp.int32)
    pad_real = ((starts + counts)[:, None] + r[None, :]).reshape(-1)
    is_pad = pad_real < jnp.repeat(ends, TS)
    tail = ends[-1] + jnp.cumsum(jnp.logical_not(is_pad).astype(jnp.int32)) - 1
    pad_slots = jnp.where(is_pad, pad_real, tail)
    return starts_row, pad_slots, tile_group, n_active, n_tiles, n_tiles * TS


def _moe_sorted(h2, logits, w1, w3, w2, layer, tm):
    rows = h2.shape[0]
    route, route_t, cnt = _router(logits, tm)
    starts_row, pad_slots, tile_group, n_active, n_tiles, n_sorted = _sort_plan(cnt[0], rows)
    slot3 = _slots(route_t, starts_row)
    xs, gs = _dispatch(slot3, pad_slots, h2, route, n_sorted)
    ys = _moe(tile_group, n_active, xs, gs, w1, w3, w2, layer, n_tiles)
    return slot3, ys


def _router_weights(gw, gb, ew, eb):
    wr = jnp.zeros((D, 128), F32).at[:, :N_GROUPS].set(gw).at[:, 8:8 + N_GROUPS * EPG].set(ew)
    br = jnp.zeros((1, 128), F32).at[0, :N_GROUPS].set(gb).at[0, 8:8 + N_GROUPS * EPG].set(eb)
    hi = wr.astype(BF16)
    lo = (wr - hi.astype(F32)).astype(BF16)
    return jnp.concatenate([hi, lo], axis=1), br


def kernel(x, c, ctx, c_ctx, ada_w, ada_b, norm1_g, norm2_g, w_in_e, conv_w, na_rpb, w_out_e, w_in_o,
           ret_decay_f, ret_decay_b, w_out_o, router_gw, router_gb, router_ew, router_eb,
           moe_w1, moe_w3, moe_w2, final_g):
    nb = x.shape[0]
    assert x.shape[1:] == (L, D) and ctx.shape[1:] == (CTX, D) and nb < 16 and ada_w.shape[0] == 2
    t_lat = nb * L
    t_ctx = nb * CTX
    t_all = t_lat + t_ctx
    tm = 1024 if t_ctx % 1024 == 0 else 512

    xsrc = (x.reshape(t_lat, D), ctx.reshape(t_ctx, D))
    c16 = jnp.zeros((16, D), F32).at[:nb].set(c).at[nb].set(c_ctx)
    mod = _ada(c16, ada_w, ada_b).reshape(2, 16, 6, 1, D)
    mods = [[mod[l, :, k] for k in range(6)] for l in range(2)]

    sh1, sc1, g1, sh2, sc2, g2 = mods[0]
    qkv = _inproj(xsrc, norm1_g[0], sh1, sc1, w_in_e[0].astype(BF16), t_lat, nb, tm)
    y_conv = (_conv(qkv, conv_w[0], L, nb, 0), _conv(qkv, conv_w[0], CTX, nb, t_lat // CTX))
    y_attn = (_natten(qkv, _natten_bias(na_rpb[0]), nb), _ctx_attn(qkv, nb))
    w_out = w_out_e[0].astype(BF16)
    wr2, br = _router_weights(router_gw[0], router_gb[0], router_ew[0], router_eb[0])
    x1, h2, logits = _outproj([y_conv, y_attn], [w_out[:CONV_CH], w_out[CONV_CH:]], xsrc, g1, sh2, sc2,
                              norm2_g[0], wr2, br, t_all, t_lat, nb, tm)
    slot3, ys = _moe_sorted(h2, logits, moe_w1, moe_w3, moe_w2, 0, tm)

    sh1, sc1, g1n, sh2n, sc2n, g2n = mods[1]
    qkvg, xa = _inproj_comb(slot3, ys, x1, g2, norm1_g[1], sh1, sc1, w_in_o[0].astype(BF16), t_lat, nb, tm,
                            _rope_tables(tm))
    g1, sh2, sc2, g2 = g1n, sh2n, sc2n, g2n
    lg = jnp.stack([jnp.log1p(-jnp.exp(ret_decay_f[0].astype(F32))),
                    jnp.log1p(-jnp.exp(ret_decay_b[0].astype(F32)))])
    y_ret = _retention(qkvg, lg, nb)
    wr2, br = _router_weights(router_gw[1], router_gb[1], router_ew[1], router_eb[1])
    x1, h2, logits = _outproj([(y_ret,)], [w_out_o[0].astype(BF16)], (xa,), g1, sh2, sc2, norm2_g[1], wr2, br,
                              t_lat, t_lat, nb, tm)
    slot3, ys = _moe_sorted(h2, logits, moe_w1, moe_w3, moe_w2, 1, tm)
    out = _combine(slot3, x1, g2, ys, nb, final_g)
    return out.reshape(nb, L, D)
```

```python
import functools
import math

import numpy as np
import jax
import jax.numpy as jnp
from jax import lax
from jax.experimental import pallas as pl
from jax.experimental.pallas import tpu as pltpu

F32 = jnp.float32
BF16 = jnp.bfloat16
HIGHEST = lax.Precision.HIGHEST

D = 1024
L = 4096
GRID_W = 64
GRID_H = L // GRID_W
CTX = 256
EPS = 1e-6

CONV_CH = 512
NA_HEADS = 8
NA_DH = 64
WIN_H = 8
WIN_W = 16
EVEN_IN = 3072

RET_HEADS = 4
RET_DK = 256
RET_DV = 512
RET_C = 256
ODD_IN = 6144
ROPE_BASE = 10000.0

N_GROUPS = 4
EPG = 8
EH = 256
GH = EPG * EH

NEG = -1e30

NA_QR = 4
NA_KR = NA_QR + WIN_H - 1
NA_NQ = NA_QR * GRID_W
NA_NK = NA_KR * GRID_W
NA_WS_MAX = GRID_H - NA_KR

TS = 512
TD = 512

VMEM_LIMIT = 56 * 1024 * 1024


def _cp(sem, vmem=VMEM_LIMIT):
    return pltpu.CompilerParams(dimension_semantics=sem, vmem_limit_bytes=vmem)


def _silu(v):
    return v * (1.0 / (1.0 + jnp.exp(-v)))


def _src_specs(src, tm, lat_tiles):
    tail = tuple(src[0].shape[1:])
    zeros = (0,) * len(tail)
    if len(src) == 1:
        return [pl.BlockSpec((tm,) + tail, lambda i, *_: (i,) + zeros)]
    return [pl.BlockSpec((tm,) + tail, lambda i, *_: (jnp.minimum(i, lat_tiles - 1),) + zeros),
            pl.BlockSpec((tm,) + tail, lambda i, *_: (jnp.maximum(i - lat_tiles, 0),) + zeros)]


def _src_load(refs, i, lat_tiles):
    if len(refs) == 1:
        return refs[0][...]
    return jnp.where(i < lat_tiles, refs[0][...], refs[1][...])


def _ada_kernel(c_ref, w_ref, b_ref, o_ref):
    s = _silu(c_ref[...])
    o_ref[...] = jnp.dot(s, w_ref[...], precision=HIGHEST, preferred_element_type=F32) + b_ref[...]


def _ada(c16, ada_w, ada_b):
    depth = ada_w.shape[0]
    tn = 1536
    return pl.pallas_call(
        _ada_kernel,
        grid=(depth, 6 * D // tn),
        in_specs=[
            pl.BlockSpec((16, D), lambda l, j: (0, 0)),
            pl.BlockSpec((None, D, tn), lambda l, j: (l, 0, j)),
            pl.BlockSpec((None, 1, tn), lambda l, j: (l, 0, j)),
        ],
        out_specs=pl.BlockSpec((None, 16, tn), lambda l, j: (l, 0, j)),
        out_shape=jax.ShapeDtypeStruct((depth, 16, 6 * D), F32),
        compiler_params=_cp(("arbitrary", "arbitrary")),
        name="ada_mod",
    )(c16, ada_w, ada_b.reshape(depth, 1, 6 * D))


def _inproj_kernel(*refs, rope, tn, n_src, lat_tiles):
    x_refs, refs = refs[:n_src], refs[n_src:]
    if rope:
        g_ref, sh_ref, sc_ref, w_ref, cos_ref, sin_ref, o_ref, h_scr = refs
    else:
        g_ref, sh_ref, sc_ref, w_ref, o_ref, h_scr = refs
    i = pl.program_id(0)
    j = pl.program_id(1)

    @pl.when(j == 0)
    def _():
        x = _src_load(x_refs, i, lat_tiles)
        ms = jnp.mean(x * x, axis=-1, keepdims=True)
        y = x * lax.rsqrt(ms + EPS) * g_ref[...]
        h_scr[...] = (y * (1.0 + sc_ref[...]) + sh_ref[...]).astype(BF16)

    acc = jnp.dot(h_scr[...], w_ref[...], preferred_element_type=F32)
    if not rope:
        o_ref[...] = acc.astype(o_ref.dtype)
        return

    @pl.when(j < 2)
    def _():
        cos = cos_ref[...]
        sin = sin_ref[...]
        scale = jnp.where(j == 1, RET_DK ** -0.5, 1.0).astype(F32)
        pieces = []
        for cblk in range(tn // 128):
            t = acc[:, cblk * 128:(cblk + 1) * 128]
            half = (cblk % 2) * 128
            r = pltpu.roll(t, 64, axis=1)
            pieces.append(t * cos[:, half:half + 128] + r * sin[:, half:half + 128])
        o_ref[...] = (jnp.concatenate(pieces, axis=1) * scale).astype(o_ref.dtype)

    @pl.when(j >= 2)
    def _():
        o_ref[...] = acc.astype(o_ref.dtype)


def _inproj(xsrc, gain, sh, sc, w, n_lat_rows, n_mod, tm, rope_tables=None):
    rows = sum(a.shape[0] for a in xsrc)
    n = w.shape[1]
    tn = 1024
    lat_tiles = n_lat_rows // tm
    mod_idx = lambda i, j: (jnp.minimum((i * tm) // L, n_mod), 0, 0)
    in_specs = _src_specs(xsrc, tm, lat_tiles) + [
        pl.BlockSpec((1, D), lambda i, j: (0, 0)),
        pl.BlockSpec((None, 1, D), mod_idx),
        pl.BlockSpec((None, 1, D), mod_idx),
        pl.BlockSpec((D, tn), lambda i, j: (0, j)),
    ]
    args = list(xsrc) + [gain.reshape(1, D), sh, sc, w]
    rope = rope_tables is not None
    if rope:
        per_seq = L // tm
        tab_idx = lambda i, j: (jnp.where(i < lat_tiles, i % per_seq, per_seq), 0)
        in_specs += [pl.BlockSpec((tm, RET_DK), tab_idx), pl.BlockSpec((tm, RET_DK), tab_idx)]
        args += list(rope_tables)
    return pl.pallas_call(
        functools.partial(_inproj_kernel, rope=rope, tn=tn, n_src=len(xsrc), lat_tiles=lat_tiles),
        grid=(rows // tm, n // tn),
        in_specs=in_specs,
        out_specs=pl.BlockSpec((tm, tn), lambda i, j: (i, j)),
        out_shape=jax.ShapeDtypeStruct((rows, n), BF16),
        scratch_shapes=[pltpu.VMEM((tm, D), BF16)],
        compiler_params=_cp(("arbitrary", "arbitrary")),
        name="inproj_rope" if rope else "inproj",
    )(*args)


def _inproj_comb_kernel(slot_ref, slotn_ref, x1_ref, g2_ref, g_ref, sh_ref, sc_ref, w_ref, cos_ref, sin_ref, ys_ref,
                        o_ref, xa_ref, h_scr, buf, sem, *, tn, tm, n_tiles):
    i = pl.program_id(0)
    j = pl.program_id(1)

    def issue(sref, b):
        def body(jj, carry):
            for u in range(2):
                r = 2 * jj + u
                pltpu.make_async_copy(ys_ref.at[pl.ds(sref[0, r], 1)], buf.at[b, pl.ds(r, 1)],
                                      sem.at[b]).start(priority=u)
            return carry

        lax.fori_loop(0, tm // 2, body, 0)

    @pl.when(j == 0)
    def _():
        @pl.when(i == 0)
        def _():
            issue(slot_ref, 0)

        b = i % 2
        pltpu.make_async_copy(ys_ref.at[pl.ds(0, tm)], buf.at[b], sem.at[b]).wait()

        @pl.when(i + 1 < n_tiles)
        def _():
            issue(slotn_ref, (i + 1) % 2)

        x = x1_ref[...] + g2_ref[...] * buf[b]
        xa_ref[...] = x
        ms = jnp.mean(x * x, axis=-1, keepdims=True)
        coef = g_ref[...] * (1.0 + sc_ref[...])
        h_scr[...] = (x * lax.rsqrt(ms + EPS) * coef + sh_ref[...]).astype(BF16)

    acc = jnp.dot(h_scr[...], w_ref[...], preferred_element_type=F32)

    @pl.when(j < 2)
    def _():
        cos = cos_ref[...]
        sin = sin_ref[...]
        scale = jnp.where(j == 1, RET_DK ** -0.5, 1.0).astype(F32)
        pieces = []
        for cblk in range(tn // 128):
            t = acc[:, cblk * 128:(cblk + 1) * 128]
            half = (cblk % 2) * 128
            r = pltpu.roll(t, 64, axis=1)
            pieces.append(t * cos[:, half:half + 128] + r * sin[:, half:half + 128])
        o_ref[...] = (jnp.concatenate(pieces, axis=1) * scale).astype(o_ref.dtype)

    @pl.when(j >= 2)
    def _():
        o_ref[...] = acc.astype(o_ref.dtype)


def _inproj_comb(slot3, ys, x1, g2, gain, sh, sc, w, n_lat_rows, n_mod, tm, rope_tables):
    rows = x1.shape[0]
    n = w.shape[1]
    tn = 1024
    n_tiles = rows // tm
    lat_tiles = n_lat_rows // tm
    per_seq = L // tm
    mod_idx = lambda i, j: (jnp.minimum((i * tm) // L, n_mod), 0, 0)
    tab_idx = lambda i, j: (jnp.where(i < lat_tiles, i % per_seq, per_seq), 0)
    in_specs = [_slot_spec(slot3, tm), _slot_spec(slot3, tm, ahead=1),
                pl.BlockSpec((tm, D), lambda i, j: (i, 0)),
                pl.BlockSpec((None, 1, D), mod_idx),
                pl.BlockSpec((1, D), lambda i, j: (0, 0)),
                pl.BlockSpec((None, 1, D), mod_idx),
                pl.BlockSpec((None, 1, D), mod_idx),
                pl.BlockSpec((D, tn), lambda i, j: (0, j)),
                pl.BlockSpec((tm, RET_DK), tab_idx),
                pl.BlockSpec((tm, RET_DK), tab_idx),
                pl.BlockSpec(memory_space=pl.ANY)]
    return pl.pallas_call(
        functools.partial(_inproj_comb_kernel, tn=tn, tm=tm, n_tiles=n_tiles),
        grid=(n_tiles, n // tn),
        in_specs=in_specs,
        out_specs=[pl.BlockSpec((tm, tn), lambda i, j: (i, j)),
                   pl.BlockSpec((tm, D), lambda i, j: (i, 0))],
        out_shape=[jax.ShapeDtypeStruct((rows, n), BF16),
                   jax.ShapeDtypeStruct((rows, D), F32)],
        scratch_shapes=[pltpu.VMEM((tm, D), BF16), pltpu.VMEM((2, tm, D), F32),
                        pltpu.SemaphoreType.DMA((2,))],
        compiler_params=_cp(("arbitrary", "arbitrary")),
        name="inproj_rope_combine",
    )(slot3, slot3, x1, g2, gain.reshape(1, D), sh, sc, w, rope_tables[0], rope_tables[1], ys)


def _rope_tables(tm):
    pos = np.arange(L)
    half = RET_DK // 4
    freqs = ROPE_BASE ** (-np.arange(half, dtype=np.float64) / half)
    ang_r = (pos // GRID_W)[:, None] * freqs[None, :]
    ang_c = (pos % GRID_W)[:, None] * freqs[None, :]
    cos = np.concatenate([np.cos(ang_r), np.cos(ang_r), np.cos(ang_c), np.cos(ang_c)], axis=1)
    sin = np.concatenate([-np.sin(ang_r), np.sin(ang_r), -np.sin(ang_c), np.sin(ang_c)], axis=1)
    cos = np.concatenate([cos, np.ones((tm, RET_DK))], axis=0)
    sin = np.concatenate([sin, np.zeros((tm, RET_DK))], axis=0)
    return jnp.asarray(cos, F32), jnp.asarray(sin, F32)


def _conv_kernel(ab_ref, ac_ref, ax_ref, w_ref, o_ref, *, seq):
    u = ac_ref[...].astype(F32) * ax_ref[...].astype(F32)
    row = lax.broadcasted_iota(jnp.int32, u.shape, 0)
    up = jnp.where(row == 0, 0.0, pltpu.roll(u, 1, axis=0))
    un = jnp.where(row == seq - 1, 0.0, pltpu.roll(u, seq - 1, axis=0))
    w = w_ref[...]
    y = ab_ref[...].astype(F32) * (up * w[0:1, :] + u * w[1:2, :] + un * w[2:3, :])
    o_ref[...] = y.astype(o_ref.dtype)


def _conv(qkv, conv_w, seq, n_seq, row_blk0):
    nb = CONV_CH // 128
    return pl.pallas_call(
        functools.partial(_conv_kernel, seq=seq),
        grid=(n_seq, nb),
        in_specs=[
            pl.BlockSpec((seq, 128), lambda b, j: (row_blk0 + b, j)),
            pl.BlockSpec((seq, 128), lambda b, j: (row_blk0 + b, nb + j)),
            pl.BlockSpec((seq, 128), lambda b, j: (row_blk0 + b, 2 * nb + j)),
            pl.BlockSpec((3, 128), lambda b, j: (0, j)),
        ],
        out_specs=pl.BlockSpec((seq, 128), lambda b, j: (b, j)),
        out_shape=jax.ShapeDtypeStruct((n_seq * seq, CONV_CH), BF16),
        compiler_params=_cp(("arbitrary", "arbitrary")),
        name="short_conv",
    )(qkv, qkv, qkv, conv_w)


def _natten_bias(rpb):
    qc = np.arange(GRID_W)
    kc = np.arange(GRID_W)
    cs = np.clip(qc - WIN_W // 2, 0, GRID_W - WIN_W)
    col_ok = (kc[None, :] >= cs[:, None]) & (kc[None, :] < cs[:, None] + WIN_W)
    dx = np.clip(kc[None, :] - qc[:, None] + WIN_W - 1, 0, 2 * WIN_W - 2)
    tcol = jnp.where(col_ok[None, None], rpb.astype(F32)[:, :, dx], NEG)
    dys, oks = [], []
    for r0 in (0, 2 * NA_QR, GRID_H - NA_QR):
        ws = int(np.clip(r0 - WIN_H // 2, 0, NA_WS_MAX))
        qr = r0 + np.arange(NA_QR)
        kr = ws + np.arange(NA_KR)
        rs = np.clip(qr - WIN_H // 2, 0, GRID_H - WIN_H)
        oks.append((kr[None, :] >= rs[:, None]) & (kr[None, :] < rs[:, None] + WIN_H))
        dys.append(np.clip(kr[None, :] - qr[:, None] + WIN_H - 1, 0, 2 * WIN_H - 2))
    dy = np.stack(dys)
    ok = np.stack(oks)
    b = jnp.take(tcol, jnp.asarray(dy.reshape(-1)), axis=1)
    b = b.reshape(NA_HEADS, 3, NA_QR, NA_KR, GRID_W, GRID_W)
    b = jnp.where(ok[None, :, :, :, None, None], b, NEG)
    b = b.transpose(0, 1, 2, 4, 3, 5).reshape(NA_HEADS // 2, 2, 3, NA_NQ, NA_NK)
    return b.transpose(0, 2, 1, 3, 4)


def _attend(qa, kw, vw, kc, vc, bias):
    dn = (((1,), (1,)), ((), ()))
    s2 = lax.dot_general(qa, kc, dn, preferred_element_type=F32)
    m = jnp.max(s2, axis=1, keepdims=True)
    if kw is not None:
        s1 = lax.dot_general(qa, kw, dn, preferred_element_type=F32) + bias
        m = jnp.maximum(m, jnp.max(s1, axis=1, keepdims=True))
        p1 = jnp.exp(s1 - m)
    p2 = jnp.exp(s2 - m)
    den = jnp.sum(p2, axis=1, keepdims=True)
    o = jnp.dot(p2.astype(BF16), vc, preferred_element_type=F32)
    if kw is not None:
        den = den + jnp.sum(p1, axis=1, keepdims=True)
        o = o + jnp.dot(p1.astype(BF16), vw, preferred_element_type=F32)
    return o / den


NA_SUB = 4


def _natten_kernel(q_ref, k_ref, v_ref, kc_ref, vc_ref, bias_ref, o_ref):
    n_rb = GRID_H // NA_QR
    lane = lax.broadcasted_iota(jnp.int32, (NA_NQ, 128), 1)
    for sb in range(NA_SUB):
        rb = pl.program_id(2) * NA_SUB + sb
        ws = jnp.clip(rb * NA_QR - WIN_H // 2, 0, NA_WS_MAX)
        start = pl.multiple_of(ws * GRID_W, GRID_W)
        kw = k_ref[pl.ds(start, NA_NK), :]
        vw = v_ref[pl.ds(start, NA_NK), :]
        cls = jnp.where(rb == 0, 0, jnp.where(rb == n_rb - 1, 2, 1))
        q = q_ref[sb * NA_NQ:(sb + 1) * NA_NQ, :] * (NA_DH ** -0.5)
        zero = jnp.zeros_like(q)
        o0 = _attend(jnp.where(lane < NA_DH, q, zero), kw, vw, kc_ref[...], vc_ref[...], bias_ref[cls, 0])
        o1 = _attend(jnp.where(lane >= NA_DH, q, zero), kw, vw, kc_ref[...], vc_ref[...], bias_ref[cls, 1])
        o_ref[sb * NA_NQ:(sb + 1) * NA_NQ, :] = jnp.where(lane < NA_DH, o0, o1).astype(o_ref.dtype)


def _natten(qkv, bias, nb):
    n_steps = GRID_H // NA_QR // NA_SUB
    qcol, kcol, vcol = 3 * CONV_CH // 128, 3 * CONV_CH // 128 + 4, 3 * CONV_CH // 128 + 8
    ctx_blk0 = nb * L // CTX
    return pl.pallas_call(
        _natten_kernel,
        grid=(nb, NA_HEADS // 2, n_steps),
        in_specs=[
            pl.BlockSpec((NA_SUB * NA_NQ, 128), lambda b, h, r: (b * n_steps + r, qcol + h)),
            pl.BlockSpec((L, 128), lambda b, h, r: (b, kcol + h)),
            pl.BlockSpec((L, 128), lambda b, h, r: (b, vcol + h)),
            pl.BlockSpec((CTX, 128), lambda b, h, r: (ctx_blk0 + b, kcol + h)),
            pl.BlockSpec((CTX, 128), lambda b, h, r: (ctx_blk0 + b, vcol + h)),
            pl.BlockSpec((None, 3, 2, NA_NQ, NA_NK), lambda b, h, r: (h, 0, 0, 0, 0)),
        ],
        out_specs=pl.BlockSpec((NA_SUB * NA_NQ, 128), lambda b, h, r: (b * n_steps + r, h)),
        out_shape=jax.ShapeDtypeStruct((nb * L, NA_HEADS * NA_DH), BF16),
        compiler_params=_cp(("arbitrary", "arbitrary", "arbitrary")),
        name="natten",
    )(qkv, qkv, qkv, qkv, qkv, bias)


def _ctx_attn_kernel(q_ref, kc_ref, vc_ref, o_ref):
    q = q_ref[...] * (NA_DH ** -0.5)
    lane = lax.broadcasted_iota(jnp.int32, q.shape, 1)
    zero = jnp.zeros_like(q)
    o0 = _attend(jnp.where(lane < NA_DH, q, zero), None, None, kc_ref[...], vc_ref[...], None)
    o1 = _attend(jnp.where(lane >= NA_DH, q, zero), None, None, kc_ref[...], vc_ref[...], None)
    o_ref[...] = jnp.where(lane < NA_DH, o0, o1).astype(o_ref.dtype)


def _ctx_attn(qkv, nb):
    qcol, kcol, vcol = 3 * CONV_CH // 128, 3 * CONV_CH // 128 + 4, 3 * CONV_CH // 128 + 8
    ctx_blk0 = nb * L // CTX
    return pl.pallas_call(
        _ctx_attn_kernel,
        grid=(nb, NA_HEADS // 2),
        in_specs=[
            pl.BlockSpec((CTX, 128), lambda b, h: (ctx_blk0 + b, qcol + h)),
            pl.BlockSpec((CTX, 128), lambda b, h: (ctx_blk0 + b, kcol + h)),
            pl.BlockSpec((CTX, 128), lambda b, h: (ctx_blk0 + b, vcol + h)),
        ],
        out_specs=pl.BlockSpec((CTX, 128), lambda b, h: (b, h)),
        out_shape=jax.ShapeDtypeStruct((nb * CTX, NA_HEADS * NA_DH), BF16),
        compiler_params=_cp(("arbitrary", "arbitrary")),
        name="ctx_attn",
    )(qkv, qkv, qkv)


def _retention_kernel(lg_ref, q_ref, k_ref, v_ref, g_ref, kc_ref, vc_ref, o_ref, y_scr, stf_scr, stb_scr):
    h = pl.program_id(1)
    n_chunks = L // RET_C
    pos_r = lax.broadcasted_iota(jnp.int32, (RET_C, 1), 0).astype(F32)
    rel = pos_r - lax.broadcasted_iota(jnp.int32, (1, RET_C), 1).astype(F32)
    dn_t = (((1,), (1,)), ((), ()))
    dn_k = (((0,), (0,)), ((), ()))

    def decays(lg, fwd):
        if fwd:
            d_intra = jnp.where(rel >= 0, jnp.exp(lg * jnp.maximum(rel, 0.0)), 0.0)
            d_q = jnp.exp(lg * (pos_r + 1.0))
            d_k = jnp.exp(lg * (RET_C - 1.0 - pos_r))
        else:
            d_intra = jnp.where(rel <= 0, jnp.exp(lg * jnp.maximum(-rel, 0.0)), 0.0)
            d_q = jnp.exp(lg * (RET_C - pos_r))
            d_k = jnp.exp(lg * pos_r)
        d_c = jnp.exp(jnp.full((1, 1), RET_C, F32) * lg)
        return d_intra, d_q, d_k, d_c

    def ctx_state(st_scr, d_k, d_c, order):
        st_scr[...] = jnp.zeros_like(st_scr)
        for jc in order:
            kd = (kc_ref[jc * RET_C:(jc + 1) * RET_C, :].astype(F32) * d_k).astype(BF16)
            vc = vc_ref[jc * RET_C:(jc + 1) * RET_C, :]
            st_scr[...] = st_scr[...] * d_c + lax.dot_general(kd, vc, dn_k, preferred_element_type=F32)

    def chunk_out(st_scr, c, d_intra, d_q, d_k, d_c):
        r0 = pl.multiple_of(c * RET_C, RET_C)
        q = q_ref[pl.ds(r0, RET_C), :]
        k = k_ref[pl.ds(r0, RET_C), :]
        v = v_ref[pl.ds(r0, RET_C), :]
        s = lax.dot_general(q, k, dn_t, preferred_element_type=F32) * d_intra
        st = st_scr[...]
        o = (jnp.dot(s.astype(BF16), v, preferred_element_type=F32)
             + jnp.dot(q, st.astype(BF16), preferred_element_type=F32) * d_q)
        kd = (k.astype(F32) * d_k).astype(BF16)
        st_scr[...] = st * d_c + lax.dot_general(kd, v, dn_k, preferred_element_type=F32)
        return r0, o

    def finish(r0, y):
        y = y * lax.rsqrt(jnp.mean(y * y, axis=-1, keepdims=True) + EPS)
        g = g_ref[pl.ds(r0, RET_C), :].astype(F32)
        o_ref[pl.ds(r0, RET_C), :] = (_silu(g) * y).astype(o_ref.dtype)

    dec_f = decays(lg_ref[0, h], True)
    dec_b = decays(lg_ref[1, h], False)
    ctx_state(stf_scr, dec_f[2], dec_f[3], range(CTX // RET_C))
    ctx_state(stb_scr, dec_b[2], dec_b[3], range(CTX // RET_C - 1, -1, -1))

    def first_half(i, carry):
        r0, o = chunk_out(stf_scr, i, *dec_f)
        y_scr[pl.ds(r0, RET_C), :] = o
        r1, o = chunk_out(stb_scr, n_chunks - 1 - i, *dec_b)
        y_scr[pl.ds(r1, RET_C), :] = o
        return carry

    lax.fori_loop(0, n_chunks // 2, first_half, 0)

    def second_half(i, carry):
        r0, o = chunk_out(stf_scr, i, *dec_f)
        finish(r0, y_scr[pl.ds(r0, RET_C), :] + o)
        r1, o = chunk_out(stb_scr, n_chunks - 1 - i, *dec_b)
        finish(r1, y_scr[pl.ds(r1, RET_C), :] + o)
        return carry

    lax.fori_loop(n_chunks // 2, n_chunks, second_half, 0)


def _retention(qkvg, lg, nb):
    ctx_blk0 = nb * L // CTX
    kq = RET_HEADS
    vq = 2 * RET_HEADS * RET_DK // RET_DV
    gq = vq + RET_HEADS
    return pl.pallas_call(
        _retention_kernel,
        grid=(nb, RET_HEADS),
        in_specs=[
            pl.BlockSpec(memory_space=pltpu.SMEM),
            pl.BlockSpec((L, RET_DK), lambda b, h: (b, h)),
            pl.BlockSpec((L, RET_DK), lambda b, h: (b, kq + h)),
            pl.BlockSpec((L, RET_DV), lambda b, h: (b, vq + h)),
            pl.BlockSpec((L, RET_DV), lambda b, h: (b, gq + h)),
            pl.BlockSpec((CTX, RET_DK), lambda b, h: (ctx_blk0 + b, kq + h)),
            pl.BlockSpec((CTX, RET_DV), lambda b, h: (ctx_blk0 + b, vq + h)),
        ],
        out_specs=pl.BlockSpec((L, RET_DV), lambda b, h: (b, h)),
        out_shape=jax.ShapeDtypeStruct((nb * L, RET_HEADS * RET_DV), BF16),
        scratch_shapes=[pltpu.VMEM((L, RET_DV), F32), pltpu.VMEM((RET_DK, RET_DV), F32),
                        pltpu.VMEM((RET_DK, RET_DV), F32)],
        compiler_params=_cp(("arbitrary", "arbitrary")),
        name="retention",
    )(lg, qkvg, qkvg, qkvg, qkvg, qkvg, qkvg)


def _outproj_kernel(*refs, src_lens, lat_tiles):
    i = pl.program_id(0)
    srcs = []
    for n in src_lens:
        srcs.append(refs[:n])
        refs = refs[n:]
    n_act = len(src_lens) - 1
    w_refs, refs = refs[:n_act], refs[n_act:]
    g1_ref, sh2_ref, sc2_ref, n2_ref, wr_ref, br_ref, x1_ref, h2_ref, lg_ref = refs
    y = None
    for a_src, w_ref in zip(srcs[:n_act], w_refs):
        t = jnp.dot(_src_load(a_src, i, lat_tiles), w_ref[...], preferred_element_type=F32)
        y = t if y is None else y + t
    x1 = _src_load(srcs[n_act], i, lat_tiles) + g1_ref[...] * y
    x1_ref[...] = x1
    ms = jnp.mean(x1 * x1, axis=-1, keepdims=True)
    coef = n2_ref[...] * (1.0 + sc2_ref[...])
    h2 = x1 * lax.rsqrt(ms + EPS) * coef + sh2_ref[...]
    h2_ref[...] = h2
    h_hi = h2.astype(BF16)
    h_lo = (h2 - h_hi.astype(F32)).astype(BF16)
    p = jnp.dot(h_hi, wr_ref[...], preferred_element_type=F32)
    q = jnp.dot(h_lo, wr_ref[:, :128], preferred_element_type=F32)
    lg_ref[...] = p[:, :128] + p[:, 128:] + q + br_ref[...]


def _outproj(acts, ws, xsrc, g1, sh2, sc2, n2g, wr2, br, rows, n_lat_rows, n_mod, tm):
    lat_tiles = n_lat_rows // tm
    mod_idx = lambda i: (jnp.minimum((i * tm) // L, n_mod), 0, 0)
    in_specs, args = [], []
    for src in list(acts) + [xsrc]:
        in_specs += _src_specs(src, tm, lat_tiles)
        args += list(src)
    in_specs += [pl.BlockSpec(w.shape, lambda i: (0, 0)) for w in ws]
    in_specs += [pl.BlockSpec((None, 1, D), mod_idx),
                 pl.BlockSpec((None, 1, D), mod_idx),
                 pl.BlockSpec((None, 1, D), mod_idx),
                 pl.BlockSpec((1, D), lambda i: (0, 0)),
                 pl.BlockSpec((D, 256), lambda i: (0, 0)),
                 pl.BlockSpec((1, 128), lambda i: (0, 0))]
    args += list(ws) + [g1, sh2, sc2, n2g.reshape(1, D), wr2, br]
    return pl.pallas_call(
        functools.partial(_outproj_kernel, src_lens=tuple(len(s) for s in list(acts) + [xsrc]),
                          lat_tiles=lat_tiles),
        grid=(rows // tm,),
        in_specs=in_specs,
        out_specs=[pl.BlockSpec((tm, D), lambda i: (i, 0)),
                   pl.BlockSpec((tm, D), lambda i: (i, 0)),
                   pl.BlockSpec((tm, 128), lambda i: (i, 0))],
        out_shape=[jax.ShapeDtypeStruct((rows, D), F32),
                   jax.ShapeDtypeStruct((rows, D), F32),
                   jax.ShapeDtypeStruct((rows, 128), F32)],
        compiler_params=_cp(("arbitrary",)),
        name="outproj_router",
    )(*args)


def _router_kernel(lg_ref, tri_ref, o_ref, rt_ref, cnt_ref, carry_scr):
    i = pl.program_id(0)

    @pl.when(i == 0)
    def _():
        carry_scr[...] = jnp.zeros_like(carry_scr)

    lg = lg_ref[...]
    lane = lax.broadcasted_iota(jnp.int32, lg.shape, 1)
    lane_f = lane.astype(F32)
    big = jnp.float32(1e9)

    gmask = lane < N_GROUPS
    gl = jnp.where(gmask, lg, NEG)
    gm = jnp.max(gl, axis=1, keepdims=True)
    gidx = jnp.min(jnp.where(gl == gm, lane_f, big), axis=1, keepdims=True)
    g_val = 1.0 / jnp.sum(jnp.where(gmask, jnp.exp(gl - gm), 0.0), axis=1, keepdims=True)

    base = 8.0 + 8.0 * gidx
    emask = (lane_f >= base) & (lane_f < base + EPG)
    el = jnp.where(emask, lg, NEG)
    m1 = jnp.max(el, axis=1, keepdims=True)
    i1 = jnp.min(jnp.where(el == m1, lane_f, big), axis=1, keepdims=True)
    el2 = jnp.where(lane_f == i1, NEG, el)
    m2 = jnp.max(el2, axis=1, keepdims=True)
    i2 = jnp.min(jnp.where(el2 == m2, lane_f, big), axis=1, keepdims=True)
    r = jnp.exp(m2 - m1)
    w1 = g_val / (1.0 + r)
    w2 = w1 * r
    gates = jnp.where(lane_f == i1 - base, w1, jnp.where(lane_f == i2 - base, w2, 0.0))

    onehot = jnp.where(lane_f == gidx, 1.0, 0.0)
    before = jnp.dot(tri_ref[...], onehot.astype(BF16), preferred_element_type=F32)
    carry = carry_scr[0:1, :]
    rank = jnp.sum(jnp.where(lane_f == gidx, before + carry, 0.0), axis=1, keepdims=True)
    carry_scr[0:1, :] = carry + jnp.sum(onehot, axis=0, keepdims=True)

    out = jnp.where(lane < EPG, gates, jnp.where(lane == 8, gidx, jnp.where(lane == 9, rank, 0.0)))
    o_ref[...] = out
    rt_ref[...] = out.T[8:16, :]
    cnt_ref[...] = carry_scr[...]


def _router(logits, tm):
    rows = logits.shape[0]
    tri = jnp.asarray(np.tril(np.ones((tm, tm), np.float32), -1), BF16)
    return pl.pallas_call(
        _router_kernel,
        grid=(rows // tm,),
        in_specs=[pl.BlockSpec((tm, 128), lambda i: (i, 0)),
                  pl.BlockSpec((tm, tm), lambda i: (0, 0))],
        out_specs=[pl.BlockSpec((tm, 128), lambda i: (i, 0)),
                   pl.BlockSpec((None, 8, tm), lambda i: (i, 0, 0)),
                   pl.BlockSpec((8, 128), lambda i: (0, 0))],
        out_shape=[jax.ShapeDtypeStruct((rows, 128), F32),
                   jax.ShapeDtypeStruct((rows // tm, 8, tm), F32),
                   jax.ShapeDtypeStruct((8, 128), F32)],
        scratch_shapes=[pltpu.VMEM((8, 128), F32)],
        compiler_params=_cp(("arbitrary",)),
        name="router",
    )(logits, tri)


def _slot_kernel(rt_ref, st_ref, o_ref):
    g = rt_ref[0:1, :]
    slot = rt_ref[1:2, :]
    st = st_ref[...]
    for k in range(N_GROUPS):
        slot = slot + jnp.where(g == k, st[:, k:k + 1], 0.0)
    o_ref[...] = slot.astype(jnp.int32)


def _slots(route_t, starts_row):
    n, _, st = route_t.shape
    return pl.pallas_call(
        _slot_kernel,
        grid=(n,),
        in_specs=[pl.BlockSpec((None, 8, st), lambda i: (i, 0, 0)),
                  pl.BlockSpec((1, 128), lambda i: (0, 0))],
        out_specs=pl.BlockSpec((None, 1, st), lambda i: (i, 0, 0)),
        out_shape=jax.ShapeDtypeStruct((n, 1, st), jnp.int32),
        compiler_params=_cp(("arbitrary",)),
        name="moe_slots",
    )(route_t, starts_row)


def _slot_spec(slot3, tile, ahead=0):
    n, _, st = slot3.shape
    per = st // tile
    last = n * per - 1

    def idx(i, *_):
        t = jnp.minimum(i + ahead, last)
        return (t // per, 0, t % per)

    return pl.BlockSpec((None, 1, tile), idx, memory_space=pltpu.SMEM)


def _dispatch_kernel(slot_ref, pad_ref, h2_ref, gt_ref, xs_ref, gs_ref, zx_scr, zg_scr, sem):
    i = pl.program_id(0)
    n_pad = pad_ref.shape[1]

    @pl.when(i == 0)
    def _():
        zx_scr[...] = jnp.zeros_like(zx_scr)
        zg_scr[...] = jnp.zeros_like(zg_scr)

        def zbody(r, carry):
            s = pad_ref[0, r]
            pltpu.make_async_copy(zx_scr.at[pl.ds(0, 1)], xs_ref.at[pl.ds(s, 1)], sem.at[0]).start()
            pltpu.make_async_copy(zg_scr.at[pl.ds(0, 1)], gs_ref.at[pl.ds(s, 1)], sem.at[1]).start()
            return carry

        lax.fori_loop(0, n_pad, zbody, 0)
        pltpu.make_async_copy(xs_ref.at[pl.ds(0, n_pad)], xs_ref.at[pl.ds(0, n_pad)], sem.at[0]).wait()
        pltpu.make_async_copy(gs_ref.at[pl.ds(0, n_pad)], gs_ref.at[pl.ds(0, n_pad)], sem.at[1]).wait()

    def body(j, carry):
        for u in range(2):
            r = 2 * j + u
            s = slot_ref[0, r]
            pltpu.make_async_copy(h2_ref.at[pl.ds(r, 1)], xs_ref.at[pl.ds(s, 1)], sem.at[0]).start(priority=u)
            pltpu.make_async_copy(gt_ref.at[pl.ds(r, 1)], gs_ref.at[pl.ds(s, 1)], sem.at[1]).start(priority=1 - u)
        return carry

    lax.fori_loop(0, TD // 2, body, 0)
    pltpu.make_async_copy(h2_ref, xs_ref.at[pl.ds(0, TD)], sem.at[0]).wait()
    pltpu.make_async_copy(gt_ref, gs_ref.at[pl.ds(0, TD)], sem.at[1]).wait()


def _dispatch(slot3, pad_slots, h2, gates, n_sorted):
    rows = h2.shape[0]
    n_pad = pad_slots.shape[0]
    return pl.pallas_call(
        _dispatch_kernel,
        grid=(rows // TD,),
        in_specs=[_slot_spec(slot3, TD),
                  pl.BlockSpec((1, n_pad), lambda i: (0, 0), memory_space=pltpu.SMEM),
                  pl.BlockSpec((TD, D), lambda i: (i, 0)),
                  pl.BlockSpec((TD, 128), lambda i: (i, 0))],
        out_specs=[pl.BlockSpec(memory_space=pl.ANY), pl.BlockSpec(memory_space=pl.ANY)],
        out_shape=[jax.ShapeDtypeStruct((n_sorted, D), F32),
                   jax.ShapeDtypeStruct((n_sorted, 128), F32)],
        scratch_shapes=[pltpu.VMEM((8, D), F32), pltpu.VMEM((8, 128), F32),
                        pltpu.SemaphoreType.DMA((2,))],
        compiler_params=_cp(("arbitrary",)),
        name="moe_dispatch",
    )(slot3, pad_slots.reshape(1, n_pad), h2, gates)


def _moe_kernel(tg_ref, na_ref, x_ref, g_ref, w1_ref, w3_ref, w2_ref, o_ref, hid_scr, w1b_scr, w3b_scr, w2b_scr):
    i = pl.program_id(0)

    @pl.when((i == 0) | (tg_ref[i] != tg_ref[jnp.maximum(i - 1, 0)]))
    def _():
        for e in range(EPG):
            cols = slice(e * EH, (e + 1) * EH)
            w1b_scr[:, cols] = w1_ref[:, cols].astype(BF16)
            w3b_scr[:, cols] = w3_ref[:, cols].astype(BF16)
            w2b_scr[cols, :] = w2_ref[cols, :].astype(BF16)

    @pl.when(i < na_ref[0])
    def _():
        x = x_ref[...].astype(BF16)
        gt = g_ref[...]
        for e in range(EPG):
            cols = slice(e * EH, (e + 1) * EH)
            h1 = jnp.dot(x, w1b_scr[:, cols], preferred_element_type=F32)
            h3 = jnp.dot(x, w3b_scr[:, cols], preferred_element_type=F32)
            hid_scr[:, cols] = (_silu(h1) * h3 * gt[:, e:e + 1]).astype(BF16)
        o_ref[...] = jnp.dot(hid_scr[...], w2b_scr[...], preferred_element_type=F32)

    @pl.when(i >= na_ref[0])
    def _():
        o_ref[...] = jnp.zeros_like(o_ref)


def _moe(tile_group, n_active, xs, gs, w1, w3, w2, layer, n_tiles):
    x_idx = lambda i, tg, na: (jnp.minimum(i, na[0] - 1), 0)
    g_idx = x_idx
    out_idx = lambda i, tg, na: (i, 0)
    w_idx = lambda i, tg, na: (layer, tg[i], 0, 0)
    once = pl.Buffered(1)
    grid_spec = pltpu.PrefetchScalarGridSpec(
        num_scalar_prefetch=2,
        grid=(n_tiles,),
        in_specs=[pl.BlockSpec((TS, D), x_idx),
                  pl.BlockSpec((TS, 128), g_idx),
                  pl.BlockSpec((None, None, D, GH), w_idx, pipeline_mode=once),
                  pl.BlockSpec((None, None, D, GH), w_idx, pipeline_mode=once),
                  pl.BlockSpec((None, None, GH, D), w_idx, pipeline_mode=once)],
        out_specs=pl.BlockSpec((TS, D), out_idx),
        scratch_shapes=[pltpu.VMEM((TS, GH), BF16), pltpu.VMEM((D, GH), BF16), pltpu.VMEM((D, GH), BF16),
                        pltpu.VMEM((GH, D), BF16)],
    )
    return pl.pallas_call(
        _moe_kernel,
        grid_spec=grid_spec,
        out_shape=jax.ShapeDtypeStruct(xs.shape, F32),
        compiler_params=_cp(("arbitrary",)),
        name="moe_experts",
    )(tile_group, n_active, xs, gs, w1, w3, w2)


def _combine_kernel(*refs, final, n_steps):
    if final:
        slot_ref, slot_next_ref, x1_ref, g2_ref, fg_ref, ys_ref, o_ref, buf, sem = refs
    else:
        slot_ref, slot_next_ref, x1_ref, g2_ref, ys_ref, o_ref, buf, sem = refs
    i = pl.program_id(0)

    def issue(sref, b):
        def body(j, carry):
            for u in range(2):
                r = 2 * j + u
                pltpu.make_async_copy(ys_ref.at[pl.ds(sref[0, r], 1)], buf.at[b, pl.ds(r, 1)],
                                      sem.at[b]).start(priority=u)
            return carry

        lax.fori_loop(0, TD // 2, body, 0)

    @pl.when(i == 0)
    def _():
        issue(slot_ref, 0)

    @pl.when(i + 1 < n_steps)
    def _():
        issue(slot_next_ref, (i + 1) % 2)

    b = i % 2
    pltpu.make_async_copy(ys_ref.at[pl.ds(0, TD)], buf.at[b], sem.at[b]).wait()

    x2 = x1_ref[...] + g2_ref[...] * buf[b]
    if final:
        ms = jnp.mean(x2 * x2, axis=-1, keepdims=True)
        x2 = x2 * lax.rsqrt(ms + EPS) * fg_ref[...]
    o_ref[...] = x2


def _combine(slot3, x1, g2, ys, n_mod, final_g=None):
    rows = x1.shape[0]
    n_steps = rows // TD
    final = final_g is not None
    mod_idx = lambda i: (jnp.minimum((i * TD) // L, n_mod), 0, 0)
    in_specs = [_slot_spec(slot3, TD), _slot_spec(slot3, TD, ahead=1),
                pl.BlockSpec((TD, D), lambda i: (i, 0)),
                pl.BlockSpec((None, 1, D), mod_idx)]
    args = [slot3, slot3, x1, g2]
    if final:
        in_specs.append(pl.BlockSpec((1, D), lambda i: (0, 0)))
        args.append(final_g.reshape(1, D))
    in_specs.append(pl.BlockSpec(memory_space=pl.ANY))
    args.append(ys)
    return pl.pallas_call(
        functools.partial(_combine_kernel, final=final, n_steps=n_steps),
        grid=(n_steps,),
        in_specs=in_specs,
        out_specs=pl.BlockSpec((TD, D), lambda i: (i, 0)),
        out_shape=jax.ShapeDtypeStruct((rows, D), F32),
        scratch_shapes=[pltpu.VMEM((2, TD, D), F32), pltpu.SemaphoreType.DMA((2,))],
        compiler_params=_cp(("arbitrary",)),
        name="moe_combine_final" if final else "moe_combine",
    )(*args)


def _sort_plan(counts_row, rows):
    counts = counts_row[:N_GROUPS].astype(jnp.int32)
    padded = ((counts + TS - 1) // TS) * TS
    ends = jnp.cumsum(padded)
    starts = ends - padded
    starts_row = jnp.zeros((1, 128), F32).at[0, :N_GROUPS].set(starts.astype(F32))
    n_tiles = rows // TS + N_GROUPS
    n_active = (ends[-1] // TS).reshape(1)
    tile_start = jnp.arange(n_tiles, dtype=jnp.int32) * TS
    tile_group = jnp.minimum(jnp.sum(tile_start[:, None] >= ends[None, :], axis=1), N_GROUPS - 1).astype(jnp.int32)
    r = jnp.arange(TS, dtype=jnp.int32)
    pad_real = ((starts + counts)[:, None] + r[None, :]).reshape(-1)
    is_pad = pad_real < jnp.repeat(ends, TS)
    tail = ends[-1] + jnp.cumsum(jnp.logical_not(is_pad).astype(jnp.int32)) - 1
    pad_slots = jnp.where(is_pad, pad_real, tail)
    return starts_row, pad_slots, tile_group, n_active, n_tiles, n_tiles * TS


def _moe_sorted(h2, logits, w1, w3, w2, layer, tm):
    rows = h2.shape[0]
    route, route_t, cnt = _router(logits, tm)
    starts_row, pad_slots, tile_group, n_active, n_tiles, n_sorted = _sort_plan(cnt[0], rows)
    slot3 = _slots(route_t, starts_row)
    xs, gs = _dispatch(slot3, pad_slots, h2, route, n_sorted)
    ys = _moe(tile_group, n_active, xs, gs, w1, w3, w2, layer, n_tiles)
    return slot3, ys


def _router_weights(gw, gb, ew, eb):
    wr = jnp.zeros((D, 128), F32).at[:, :N_GROUPS].set(gw).at[:, 8:8 + N_GROUPS * EPG].set(ew)
    br = jnp.zeros((1, 128), F32).at[0, :N_GROUPS].set(gb).at[0, 8:8 + N_GROUPS * EPG].set(eb)
    hi = wr.astype(BF16)
    lo = (wr - hi.astype(F32)).astype(BF16)
    return jnp.concatenate([hi, lo], axis=1), br


def kernel(x, c, ctx, c_ctx, ada_w, ada_b, norm1_g, norm2_g, w_in_e, conv_w, na_rpb, w_out_e, w_in_o,
           ret_decay_f, ret_decay_b, w_out_o, router_gw, router_gb, router_ew, router_eb,
           moe_w1, moe_w3, moe_w2, final_g):
    nb = x.shape[0]
    assert x.shape[1:] == (L, D) and ctx.shape[1:] == (CTX, D) and nb < 16 and ada_w.shape[0] == 2
    t_lat = nb * L
    t_ctx = nb * CTX
    t_all = t_lat + t_ctx
    tm = 1024 if t_ctx % 1024 == 0 else 512

    xsrc = (x.reshape(t_lat, D), ctx.reshape(t_ctx, D))
    c16 = jnp.zeros((16, D), F32).at[:nb].set(c).at[nb].set(c_ctx)
    mod = _ada(c16, ada_w, ada_b).reshape(2, 16, 6, 1, D)
    mods = [[mod[l, :, k] for k in range(6)] for l in range(2)]

    sh1, sc1, g1, sh2, sc2, g2 = mods[0]
    qkv = _inproj(xsrc, norm1_g[0], sh1, sc1, w_in_e[0].astype(BF16), t_lat, nb, tm)
    y_conv = (_conv(qkv, conv_w[0], L, nb, 0), _conv(qkv, conv_w[0], CTX, nb, t_lat // CTX))
    y_attn = (_natten(qkv, _natten_bias(na_rpb[0]), nb), _ctx_attn(qkv, nb))
    w_out = w_out_e[0].astype(BF16)
    wr2, br = _router_weights(router_gw[0], router_gb[0], router_ew[0], router_eb[0])
    x1, h2, logits = _outproj([y_conv, y_attn], [w_out[:CONV_CH], w_out[CONV_CH:]], xsrc, g1, sh2, sc2,
                              norm2_g[0], wr2, br, t_all, t_lat, nb, tm)
    slot3, ys = _moe_sorted(h2, logits, moe_w1, moe_w3, moe_w2, 0, tm)

    sh1, sc1, g1n, sh2n, sc2n, g2n = mods[1]
    qkvg, xa = _inproj_comb(slot3, ys, x1, g2, norm1_g[1], sh1, sc1, w_in_o[0].astype(BF16), t_lat, nb, tm,
                            _rope_tables(tm))
    g1, sh2, sc2, g2 = g1n, sh2n, sc2n, g2n
    lg = jnp.stack([jnp.log1p(-jnp.exp(ret_decay_f[0].astype(F32))),
                    jnp.log1p(-jnp.exp(ret_decay_b[0].astype(F32)))])
    y_ret = _retention(qkvg, lg, nb)
    wr2, br = _router_weights(router_gw[1], router_gb[1], router_ew[1], router_eb[1])
    x1, h2, logits = _outproj([(y_ret,)], [w_out_o[0].astype(BF16)], (xa,), g1, sh2, sc2, norm2_g[1], wr2, br,
                              t_lat, t_lat, nb, tm)
    slot3, ys = _moe_sorted(h2, logits, moe_w1, moe_w3, moe_w2, 1, tm)
    out = _combine(slot3, x1, g2, ys, nb, final_g)
    return out.reshape(nb, L, D)
```

```python
import functools
import math

import numpy as np
import jax
import jax.numpy as jnp
from jax import lax
from jax.experimental import pallas as pl
from jax.experimental.pallas import tpu as pltpu

F32 = jnp.float32
BF16 = jnp.bfloat16
HIGHEST = lax.Precision.HIGHEST

D = 1024
L = 4096
GRID_W = 64
GRID_H = L // GRID_W
CTX = 256
EPS = 1e-6

CONV_CH = 512
NA_HEADS = 8
NA_DH = 64
WIN_H = 8
WIN_W = 16
EVEN_IN = 3072

RET_HEADS = 4
RET_DK = 256
RET_DV = 512
RET_C = 256
ODD_IN = 6144
ROPE_BASE = 10000.0

N_GROUPS = 4
EPG = 8
EH = 256
GH = EPG * EH

NEG = -1e30

NA_QR = 4
NA_KR = NA_QR + WIN_H - 1
NA_NQ = NA_QR * GRID_W
NA_NK = NA_KR * GRID_W
NA_WS_MAX = GRID_H - NA_KR

DX = D + 128
TS = 512
TD = 512

VMEM_LIMIT = 56 * 1024 * 1024


def _cp(sem, vmem=VMEM_LIMIT):
    return pltpu.CompilerParams(dimension_semantics=sem, vmem_limit_bytes=vmem)


def _silu(v):
    return v * (1.0 / (1.0 + jnp.exp(-v)))


def _src_specs(src, tm, lat_tiles):
    tail = tuple(src[0].shape[1:])
    zeros = (0,) * len(tail)
    if len(src) == 1:
        return [pl.BlockSpec((tm,) + tail, lambda i, *_: (i,) + zeros)]
    return [pl.BlockSpec((tm,) + tail, lambda i, *_: (jnp.minimum(i, lat_tiles - 1),) + zeros),
            pl.BlockSpec((tm,) + tail, lambda i, *_: (jnp.maximum(i - lat_tiles, 0),) + zeros)]


def _src_load(refs, i, lat_tiles):
    if len(refs) == 1:
        return refs[0][...]
    return jnp.where(i < lat_tiles, refs[0][...], refs[1][...])


def _ada_kernel(c_ref, w_ref, b_ref, o_ref):
    s = _silu(c_ref[...])
    o_ref[...] = jnp.dot(s, w_ref[...], precision=HIGHEST, preferred_element_type=F32) + b_ref[...]


def _ada(c16, ada_w, ada_b):
    depth = ada_w.shape[0]
    tn = 1536
    return pl.pallas_call(
        _ada_kernel,
        grid=(depth, 6 * D // tn),
        in_specs=[
            pl.BlockSpec((16, D), lambda l, j: (0, 0)),
            pl.BlockSpec((None, D, tn), lambda l, j: (l, 0, j)),
            pl.BlockSpec((None, 1, tn), lambda l, j: (l, 0, j)),
        ],
        out_specs=pl.BlockSpec((None, 16, tn), lambda l, j: (l, 0, j)),
        out_shape=jax.ShapeDtypeStruct((depth, 16, 6 * D), F32),
        compiler_params=_cp(("arbitrary", "arbitrary")),
        name="ada_mod",
    )(c16, ada_w, ada_b.reshape(depth, 1, 6 * D))


def _inproj_kernel(*refs, rope, tn, n_src, lat_tiles):
    x_refs, refs = refs[:n_src], refs[n_src:]
    if rope:
        g_ref, sh_ref, sc_ref, w_ref, cos_ref, sin_ref, o_ref, h_scr = refs
    else:
        g_ref, sh_ref, sc_ref, w_ref, o_ref, h_scr = refs
    i = pl.program_id(0)
    j = pl.program_id(1)

    @pl.when(j == 0)
    def _():
        x = _src_load(x_refs, i, lat_tiles)
        ms = jnp.mean(x * x, axis=-1, keepdims=True)
        y = x * lax.rsqrt(ms + EPS) * g_ref[...]
        h_scr[...] = (y * (1.0 + sc_ref[...]) + sh_ref[...]).astype(BF16)

    acc = jnp.dot(h_scr[...], w_ref[...], preferred_element_type=F32)
    if not rope:
        o_ref[...] = acc.astype(o_ref.dtype)
        return

    @pl.when(j < 2)
    def _():
        cos = cos_ref[...]
        sin = sin_ref[...]
        scale = jnp.where(j == 1, RET_DK ** -0.5, 1.0).astype(F32)
        pieces = []
        for cblk in range(tn // 128):
            t = acc[:, cblk * 128:(cblk + 1) * 128]
            half = (cblk % 2) * 128
            r = pltpu.roll(t, 64, axis=1)
            pieces.append(t * cos[:, half:half + 128] + r * sin[:, half:half + 128])
        o_ref[...] = (jnp.concatenate(pieces, axis=1) * scale).astype(o_ref.dtype)

    @pl.when(j >= 2)
    def _():
        o_ref[...] = acc.astype(o_ref.dtype)


def _inproj(xsrc, gain, sh, sc, w, n_lat_rows, n_mod, tm, rope_tables=None):
    rows = sum(a.shape[0] for a in xsrc)
    n = w.shape[1]
    tn = 1024
    lat_tiles = n_lat_rows // tm
    mod_idx = lambda i, j: (jnp.minimum((i * tm) // L, n_mod), 0, 0)
    in_specs = _src_specs(xsrc, tm, lat_tiles) + [
        pl.BlockSpec((1, D), lambda i, j: (0, 0)),
        pl.BlockSpec((None, 1, D), mod_idx),
        pl.BlockSpec((None, 1, D), mod_idx),
        pl.BlockSpec((D, tn), lambda i, j: (0, j)),
    ]
    args = list(xsrc) + [gain.reshape(1, D), sh, sc, w]
    rope = rope_tables is not None
    if rope:
        per_seq = L // tm
        tab_idx = lambda i, j: (jnp.where(i < lat_tiles, i % per_seq, per_seq), 0)
        in_specs += [pl.BlockSpec((tm, RET_DK), tab_idx), pl.BlockSpec((tm, RET_DK), tab_idx)]
        args += list(rope_tables)
    return pl.pallas_call(
        functools.partial(_inproj_kernel, rope=rope, tn=tn, n_src=len(xsrc), lat_tiles=lat_tiles),
        grid=(rows // tm, n // tn),
        in_specs=in_specs,
        out_specs=pl.BlockSpec((tm, tn), lambda i, j: (i, j)),
        out_shape=jax.ShapeDtypeStruct((rows, n), BF16),
        scratch_shapes=[pltpu.VMEM((tm, D), BF16)],
        compiler_params=_cp(("arbitrary", "arbitrary")),
        name="inproj_rope" if rope else "inproj",
    )(*args)


def _inproj_comb_kernel(slot_ref, slotn_ref, x1_ref, g2_ref, g_ref, sh_ref, sc_ref, w_ref, cos_ref, sin_ref, ys_ref,
                        o_ref, xa_ref, h_scr, buf, sem, *, tn, tm, n_tiles):
    i = pl.program_id(0)
    j = pl.program_id(1)

    def issue(sref, b):
        def body(jj, carry):
            for u in range(2):
                r = 2 * jj + u
                pltpu.make_async_copy(ys_ref.at[pl.ds(sref[0, r], 1)], buf.at[b, pl.ds(r, 1)],
                                      sem.at[b]).start(priority=u)
            return carry

        lax.fori_loop(0, tm // 2, body, 0)

    @pl.when(j == 0)
    def _():
        @pl.when(i == 0)
        def _():
            issue(slot_ref, 0)

        b = i % 2
        pltpu.make_async_copy(ys_ref.at[pl.ds(0, tm)], buf.at[b], sem.at[b]).wait()

        @pl.when(i + 1 < n_tiles)
        def _():
            issue(slotn_ref, (i + 1) % 2)

        x = x1_ref[...] + g2_ref[...] * buf[b]
        xa_ref[...] = x
        ms = jnp.mean(x * x, axis=-1, keepdims=True)
        coef = g_ref[...] * (1.0 + sc_ref[...])
        h_scr[...] = (x * lax.rsqrt(ms + EPS) * coef + sh_ref[...]).astype(BF16)

    acc = jnp.dot(h_scr[...], w_ref[...], preferred_element_type=F32)

    @pl.when(j < 2)
    def _():
        cos = cos_ref[...]
        sin = sin_ref[...]
        scale = jnp.where(j == 1, RET_DK ** -0.5, 1.0).astype(F32)
        pieces = []
        for cblk in range(tn // 128):
            t = acc[:, cblk * 128:(cblk + 1) * 128]
            half = (cblk % 2) * 128
            r = pltpu.roll(t, 64, axis=1)
            pieces.append(t * cos[:, half:half + 128] + r * sin[:, half:half + 128])
        o_ref[...] = (jnp.concatenate(pieces, axis=1) * scale).astype(o_ref.dtype)

    @pl.when(j >= 2)
    def _():
        o_ref[...] = acc.astype(o_ref.dtype)


def _inproj_comb(slot3, ys, x1, g2, gain, sh, sc, w, n_lat_rows, n_mod, tm, rope_tables):
    rows = x1.shape[0]
    n = w.shape[1]
    tn = 1024
    n_tiles = rows // tm
    lat_tiles = n_lat_rows // tm
    per_seq = L // tm
    mod_idx = lambda i, j: (jnp.minimum((i * tm) // L, n_mod), 0, 0)
    tab_idx = lambda i, j: (jnp.where(i < lat_tiles, i % per_seq, per_seq), 0)
    in_specs = [_slot_spec(slot3, tm), _slot_spec(slot3, tm, ahead=1),
                pl.BlockSpec((tm, D), lambda i, j: (i, 0)),
                pl.BlockSpec((None, 1, D), mod_idx),
                pl.BlockSpec((1, D), lambda i, j: (0, 0)),
                pl.BlockSpec((None, 1, D), mod_idx),
                pl.BlockSpec((None, 1, D), mod_idx),
                pl.BlockSpec((D, tn), lambda i, j: (0, j)),
                pl.BlockSpec((tm, RET_DK), tab_idx),
                pl.BlockSpec((tm, RET_DK), tab_idx),
                pl.BlockSpec(memory_space=pl.ANY)]
    return pl.pallas_call(
        functools.partial(_inproj_comb_kernel, tn=tn, tm=tm, n_tiles=n_tiles),
        grid=(n_tiles, n // tn),
        in_specs=in_specs,
        out_specs=[pl.BlockSpec((tm, tn), lambda i, j: (i, j)),
                   pl.BlockSpec((tm, D), lambda i, j: (i, 0))],
        out_shape=[jax.ShapeDtypeStruct((rows, n), BF16),
                   jax.ShapeDtypeStruct((rows, D), F32)],
        scratch_shapes=[pltpu.VMEM((tm, D), BF16), pltpu.VMEM((2, tm, D), F32),
                        pltpu.SemaphoreType.DMA((2,))],
        compiler_params=_cp(("arbitrary", "arbitrary")),
        name="inproj_rope_combine",
    )(slot3, slot3, x1, g2, gain.reshape(1, D), sh, sc, w, rope_tables[0], rope_tables[1], ys)


def _rope_tables(tm):
    pos = np.arange(L)
    half = RET_DK // 4
    freqs = ROPE_BASE ** (-np.arange(half, dtype=np.float64) / half)
    ang_r = (pos // GRID_W)[:, None] * freqs[None, :]
    ang_c = (pos % GRID_W)[:, None] * freqs[None, :]
    cos = np.concatenate([np.cos(ang_r), np.cos(ang_r), np.cos(ang_c), np.cos(ang_c)], axis=1)
    sin = np.concatenate([-np.sin(ang_r), np.sin(ang_r), -np.sin(ang_c), np.sin(ang_c)], axis=1)
    cos = np.concatenate([cos, np.ones((tm, RET_DK))], axis=0)
    sin = np.concatenate([sin, np.zeros((tm, RET_DK))], axis=0)
    return jnp.asarray(cos, F32), jnp.asarray(sin, F32)


def _conv_kernel(ab_ref, ac_ref, ax_ref, w_ref, o_ref, *, seq):
    u = ac_ref[...].astype(F32) * ax_ref[...].astype(F32)
    row = lax.broadcasted_iota(jnp.int32, u.shape, 0)
    up = jnp.where(row == 0, 0.0, pltpu.roll(u, 1, axis=0))
    un = jnp.where(row == seq - 1, 0.0, pltpu.roll(u, seq - 1, axis=0))
    w = w_ref[...]
    y = ab_ref[...].astype(F32) * (up * w[0:1, :] + u * w[1:2, :] + un * w[2:3, :])
    o_ref[...] = y.astype(o_ref.dtype)


def _conv(qkv, conv_w, seq, n_seq, row_blk0):
    nb = CONV_CH // 128
    return pl.pallas_call(
        functools.partial(_conv_kernel, seq=seq),
        grid=(n_seq, nb),
        in_specs=[
            pl.BlockSpec((seq, 128), lambda b, j: (row_blk0 + b, j)),
            pl.BlockSpec((seq, 128), lambda b, j: (row_blk0 + b, nb + j)),
            pl.BlockSpec((seq, 128), lambda b, j: (row_blk0 + b, 2 * nb + j)),
            pl.BlockSpec((3, 128), lambda b, j: (0, j)),
        ],
        out_specs=pl.BlockSpec((seq, 128), lambda b, j: (b, j)),
        out_shape=jax.ShapeDtypeStruct((n_seq * seq, CONV_CH), BF16),
        compiler_params=_cp(("arbitrary", "arbitrary")),
        name="short_conv",
    )(qkv, qkv, qkv, conv_w)


def _natten_bias(rpb):
    qc = np.arange(GRID_W)
    kc = np.arange(GRID_W)
    cs = np.clip(qc - WIN_W // 2, 0, GRID_W - WIN_W)
    col_ok = (kc[None, :] >= cs[:, None]) & (kc[None, :] < cs[:, None] + WIN_W)
    dx = np.clip(kc[None, :] - qc[:, None] + WIN_W - 1, 0, 2 * WIN_W - 2)
    tcol = jnp.where(col_ok[None, None], rpb.astype(F32)[:, :, dx], NEG)
    dys, oks = [], []
    for r0 in (0, 2 * NA_QR, GRID_H - NA_QR):
        ws = int(np.clip(r0 - WIN_H // 2, 0, NA_WS_MAX))
        qr = r0 + np.arange(NA_QR)
        kr = ws + np.arange(NA_KR)
        rs = np.clip(qr - WIN_H // 2, 0, GRID_H - WIN_H)
        oks.append((kr[None, :] >= rs[:, None]) & (kr[None, :] < rs[:, None] + WIN_H))
        dys.append(np.clip(kr[None, :] - qr[:, None] + WIN_H - 1, 0, 2 * WIN_H - 2))
    dy = np.stack(dys)
    ok = np.stack(oks)
    b = jnp.take(tcol, jnp.asarray(dy.reshape(-1)), axis=1)
    b = b.reshape(NA_HEADS, 3, NA_QR, NA_KR, GRID_W, GRID_W)
    b = jnp.where(ok[None, :, :, :, None, None], b, NEG)
    b = b.transpose(0, 1, 2, 4, 3, 5).reshape(NA_HEADS // 2, 2, 3, NA_NQ, NA_NK)
    return b.transpose(0, 2, 1, 3, 4)


def _attend(qa, kw, vw, kc, vc, bias):
    dn = (((1,), (1,)), ((), ()))
    s2 = lax.dot_general(qa, kc, dn, preferred_element_type=F32)
    m = jnp.max(s2, axis=1, keepdims=True)
    if kw is not None:
        s1 = lax.dot_general(qa, kw, dn, preferred_element_type=F32) + bias
        m = jnp.maximum(m, jnp.max(s1, axis=1, keepdims=True))
        p1 = jnp.exp(s1 - m)
    p2 = jnp.exp(s2 - m)
    den = jnp.sum(p2, axis=1, keepdims=True)
    o = jnp.dot(p2.astype(BF16), vc, preferred_element_type=F32)
    if kw is not None:
        den = den + jnp.sum(p1, axis=1, keepdims=True)
        o = o + jnp.dot(p1.astype(BF16), vw, preferred_element_type=F32)
    return o / den


NA_SUB = 4


def _natten_kernel(q_ref, k_ref, v_ref, kc_ref, vc_ref, bias_ref, o_ref):
    n_rb = GRID_H // NA_QR
    lane = lax.broadcasted_iota(jnp.int32, (NA_NQ, 128), 1)
    for sb in range(NA_SUB):
        rb = pl.program_id(2) * NA_SUB + sb
        ws = jnp.clip(rb * NA_QR - WIN_H // 2, 0, NA_WS_MAX)
        start = pl.multiple_of(ws * GRID_W, GRID_W)
        kw = k_ref[pl.ds(start, NA_NK), :]
        vw = v_ref[pl.ds(start, NA_NK), :]
        cls = jnp.where(rb == 0, 0, jnp.where(rb == n_rb - 1, 2, 1))
        q = q_ref[sb * NA_NQ:(sb + 1) * NA_NQ, :] * (NA_DH ** -0.5)
        zero = jnp.zeros_like(q)
        o0 = _attend(jnp.where(lane < NA_DH, q, zero), kw, vw, kc_ref[...], vc_ref[...], bias_ref[cls, 0])
        o1 = _attend(jnp.where(lane >= NA_DH, q, zero), kw, vw, kc_ref[...], vc_ref[...], bias_ref[cls, 1])
        o_ref[sb * NA_NQ:(sb + 1) * NA_NQ, :] = jnp.where(lane < NA_DH, o0, o1).astype(o_ref.dtype)


def _natten(qkv, bias, nb):
    n_steps = GRID_H // NA_QR // NA_SUB
    qcol, kcol, vcol = 3 * CONV_CH // 128, 3 * CONV_CH // 128 + 4, 3 * CONV_CH // 128 + 8
    ctx_blk0 = nb * L // CTX
    return pl.pallas_call(
        _natten_kernel,
        grid=(nb, NA_HEADS // 2, n_steps),
        in_specs=[
            pl.BlockSpec((NA_SUB * NA_NQ, 128), lambda b, h, r: (b * n_steps + r, qcol + h)),
            pl.BlockSpec((L, 128), lambda b, h, r: (b, kcol + h)),
            pl.BlockSpec((L, 128), lambda b, h, r: (b, vcol + h)),
            pl.BlockSpec((CTX, 128), lambda b, h, r: (ctx_blk0 + b, kcol + h)),
            pl.BlockSpec((CTX, 128), lambda b, h, r: (ctx_blk0 + b, vcol + h)),
            pl.BlockSpec((None, 3, 2, NA_NQ, NA_NK), lambda b, h, r: (h, 0, 0, 0, 0)),
        ],
        out_specs=pl.BlockSpec((NA_SUB * NA_NQ, 128), lambda b, h, r: (b * n_steps + r, h)),
        out_shape=jax.ShapeDtypeStruct((nb * L, NA_HEADS * NA_DH), BF16),
        compiler_params=_cp(("arbitrary", "arbitrary", "arbitrary")),
        name="natten",
    )(qkv, qkv, qkv, qkv, qkv, bias)


def _ctx_attn_kernel(q_ref, kc_ref, vc_ref, o_ref):
    q = q_ref[...] * (NA_DH ** -0.5)
    lane = lax.broadcasted_iota(jnp.int32, q.shape, 1)
    zero = jnp.zeros_like(q)
    o0 = _attend(jnp.where(lane < NA_DH, q, zero), None, None, kc_ref[...], vc_ref[...], None)
    o1 = _attend(jnp.where(lane >= NA_DH, q, zero), None, None, kc_ref[...], vc_ref[...], None)
    o_ref[...] = jnp.where(lane < NA_DH, o0, o1).astype(o_ref.dtype)


def _ctx_attn(qkv, nb):
    qcol, kcol, vcol = 3 * CONV_CH // 128, 3 * CONV_CH // 128 + 4, 3 * CONV_CH // 128 + 8
    ctx_blk0 = nb * L // CTX
    return pl.pallas_call(
        _ctx_attn_kernel,
        grid=(nb, NA_HEADS // 2),
        in_specs=[
            pl.BlockSpec((CTX, 128), lambda b, h: (ctx_blk0 + b, qcol + h)),
            pl.BlockSpec((CTX, 128), lambda b, h: (ctx_blk0 + b, kcol + h)),
            pl.BlockSpec((CTX, 128), lambda b, h: (ctx_blk0 + b, vcol + h)),
        ],
        out_specs=pl.BlockSpec((CTX, 128), lambda b, h: (b, h)),
        out_shape=jax.ShapeDtypeStruct((nb * CTX, NA_HEADS * NA_DH), BF16),
        compiler_params=_cp(("arbitrary", "arbitrary")),
        name="ctx_attn",
    )(qkv, qkv, qkv)


def _retention_kernel(lg_ref, q_ref, k_ref, v_ref, g_ref, kc_ref, vc_ref, o_ref, y_scr, stf_scr, stb_scr):
    h = pl.program_id(1)
    n_chunks = L // RET_C
    pos_r = lax.broadcasted_iota(jnp.int32, (RET_C, 1), 0).astype(F32)
    rel = pos_r - lax.broadcasted_iota(jnp.int32, (1, RET_C), 1).astype(F32)
    dn_t = (((1,), (1,)), ((), ()))
    dn_k = (((0,), (0,)), ((), ()))

    def decays(lg, fwd):
        if fwd:
            d_intra = jnp.where(rel >= 0, jnp.exp(lg * jnp.maximum(rel, 0.0)), 0.0)
            d_q = jnp.exp(lg * (pos_r + 1.0))
            d_k = jnp.exp(lg * (RET_C - 1.0 - pos_r))
        else:
            d_intra = jnp.where(rel <= 0, jnp.exp(lg * jnp.maximum(-rel, 0.0)), 0.0)
            d_q = jnp.exp(lg * (RET_C - pos_r))
            d_k = jnp.exp(lg * pos_r)
        d_c = jnp.exp(jnp.full((1, 1), RET_C, F32) * lg)
        return d_intra, d_q, d_k, d_c

    def ctx_state(st_scr, d_k, d_c, order):
        st_scr[...] = jnp.zeros_like(st_scr)
        for jc in order:
            kd = (kc_ref[jc * RET_C:(jc + 1) * RET_C, :].astype(F32) * d_k).astype(BF16)
            vc = vc_ref[jc * RET_C:(jc + 1) * RET_C, :]
            st_scr[...] = st_scr[...] * d_c + lax.dot_general(kd, vc, dn_k, preferred_element_type=F32)

    def chunk_out(st_scr, c, d_intra, d_q, d_k, d_c):
        r0 = pl.multiple_of(c * RET_C, RET_C)
        q = q_ref[pl.ds(r0, RET_C), :]
        k = k_ref[pl.ds(r0, RET_C), :]
        v = v_ref[pl.ds(r0, RET_C), :]
        s = lax.dot_general(q, k, dn_t, preferred_element_type=F32) * d_intra
        st = st_scr[...]
        o = (jnp.dot(s.astype(BF16), v, preferred_element_type=F32)
             + jnp.dot(q, st.astype(BF16), preferred_element_type=F32) * d_q)
        kd = (k.astype(F32) * d_k).astype(BF16)
        st_scr[...] = st * d_c + lax.dot_general(kd, v, dn_k, preferred_element_type=F32)
        return r0, o

    def finish(r0, y):
        y = y * lax.rsqrt(jnp.mean(y * y, axis=-1, keepdims=True) + EPS)
        g = g_ref[pl.ds(r0, RET_C), :].astype(F32)
        o_ref[pl.ds(r0, RET_C), :] = (_silu(g) * y).astype(o_ref.dtype)

    dec_f = decays(lg_ref[0, h], True)
    dec_b = decays(lg_ref[1, h], False)
    ctx_state(stf_scr, dec_f[2], dec_f[3], range(CTX // RET_C))
    ctx_state(stb_scr, dec_b[2], dec_b[3], range(CTX // RET_C - 1, -1, -1))

    def first_half(i, carry):
        r0, o = chunk_out(stf_scr, i, *dec_f)
        y_scr[pl.ds(r0, RET_C), :] = o
        r1, o = chunk_out(stb_scr, n_chunks - 1 - i, *dec_b)
        y_scr[pl.ds(r1, RET_C), :] = o
        return carry

    lax.fori_loop(0, n_chunks // 2, first_half, 0)

    def second_half(i, carry):
        r0, o = chunk_out(stf_scr, i, *dec_f)
        finish(r0, y_scr[pl.ds(r0, RET_C), :] + o)
        r1, o = chunk_out(stb_scr, n_chunks - 1 - i, *dec_b)
        finish(r1, y_scr[pl.ds(r1, RET_C), :] + o)
        return carry

    lax.fori_loop(n_chunks // 2, n_chunks, second_half, 0)


def _retention(qkvg, lg, nb):
    ctx_blk0 = nb * L // CTX
    kq = RET_HEADS
    vq = 2 * RET_HEADS * RET_DK // RET_DV
    gq = vq + RET_HEADS
    return pl.pallas_call(
        _retention_kernel,
        grid=(nb, RET_HEADS),
        in_specs=[
            pl.BlockSpec(memory_space=pltpu.SMEM),
            pl.BlockSpec((L, RET_DK), lambda b, h: (b, h)),
            pl.BlockSpec((L, RET_DK), lambda b, h: (b, kq + h)),
            pl.BlockSpec((L, RET_DV), lambda b, h: (b, vq + h)),
            pl.BlockSpec((L, RET_DV), lambda b, h: (b, gq + h)),
            pl.BlockSpec((CTX, RET_DK), lambda b, h: (ctx_blk0 + b, kq + h)),
            pl.BlockSpec((CTX, RET_DV), lambda b, h: (ctx_blk0 + b, vq + h)),
        ],
        out_specs=pl.BlockSpec((L, RET_DV), lambda b, h: (b, h)),
        out_shape=jax.ShapeDtypeStruct((nb * L, RET_HEADS * RET_DV), BF16),
        scratch_shapes=[pltpu.VMEM((L, RET_DV), F32), pltpu.VMEM((RET_DK, RET_DV), F32),
                        pltpu.VMEM((RET_DK, RET_DV), F32)],
        compiler_params=_cp(("arbitrary", "arbitrary")),
        name="retention",
    )(lg, qkvg, qkvg, qkvg, qkvg, qkvg, qkvg)


def _outproj_kernel(*refs, src_lens, lat_tiles):
    i = pl.program_id(0)
    srcs = []
    for n in src_lens:
        srcs.append(refs[:n])
        refs = refs[n:]
    n_act = len(src_lens) - 1
    w_refs, refs = refs[:n_act], refs[n_act:]
    (g1_ref, sh2_ref, sc2_ref, n2_ref, wr_ref, br_ref, tri_ref,
     x1_ref, h2x_ref, rt_ref, cnt_ref, carry_scr) = refs

    @pl.when(i == 0)
    def _():
        carry_scr[...] = jnp.zeros_like(carry_scr)

    y = None
    for a_src, w_ref in zip(srcs[:n_act], w_refs):
        t = jnp.dot(_src_load(a_src, i, lat_tiles), w_ref[...], preferred_element_type=F32)
        y = t if y is None else y + t
    x1 = _src_load(srcs[n_act], i, lat_tiles) + g1_ref[...] * y
    x1_ref[...] = x1
    ms = jnp.mean(x1 * x1, axis=-1, keepdims=True)
    coef = n2_ref[...] * (1.0 + sc2_ref[...])
    h2 = x1 * lax.rsqrt(ms + EPS) * coef + sh2_ref[...]
    h2x_ref[:, :D] = h2
    h_hi = h2.astype(BF16)
    h_lo = (h2 - h_hi.astype(F32)).astype(BF16)
    p = jnp.dot(h_hi, wr_ref[...], preferred_element_type=F32)
    q = jnp.dot(h_lo, wr_ref[:, :128], preferred_element_type=F32)
    route = _route(p[:, :128] + p[:, 128:] + q + br_ref[...], tri_ref, carry_scr)
    h2x_ref[:, D:] = route
    rt_ref[...] = route.T[8:16, :]
    cnt_ref[...] = carry_scr[...]


def _outproj(acts, ws, xsrc, g1, sh2, sc2, n2g, wr2, br, rows, n_lat_rows, n_mod, tm):
    lat_tiles = n_lat_rows // tm
    tri = jnp.asarray(np.tril(np.ones((tm, tm), np.float32), -1), BF16)
    mod_idx = lambda i: (jnp.minimum((i * tm) // L, n_mod), 0, 0)
    in_specs, args = [], []
    for src in list(acts) + [xsrc]:
        in_specs += _src_specs(src, tm, lat_tiles)
        args += list(src)
    in_specs += [pl.BlockSpec(w.shape, lambda i: (0, 0)) for w in ws]
    in_specs += [pl.BlockSpec((None, 1, D), mod_idx),
                 pl.BlockSpec((None, 1, D), mod_idx),
                 pl.BlockSpec((None, 1, D), mod_idx),
                 pl.BlockSpec((1, D), lambda i: (0, 0)),
                 pl.BlockSpec((D, 256), lambda i: (0, 0)),
                 pl.BlockSpec((1, 128), lambda i: (0, 0)),
                 pl.BlockSpec((tm, tm), lambda i: (0, 0))]
    args += list(ws) + [g1, sh2, sc2, n2g.reshape(1, D), wr2, br, tri]
    return pl.pallas_call(
        functools.partial(_outproj_kernel, src_lens=tuple(len(s) for s in list(acts) + [xsrc]),
                          lat_tiles=lat_tiles),
        grid=(rows // tm,),
        in_specs=in_specs,
        out_specs=[pl.BlockSpec((tm, D), lambda i: (i, 0)),
                   pl.BlockSpec((tm, DX), lambda i: (i, 0)),
                   pl.BlockSpec((None, 8, tm), lambda i: (i, 0, 0)),
                   pl.BlockSpec((8, 128), lambda i: (0, 0))],
        out_shape=[jax.ShapeDtypeStruct((rows, D), F32),
                   jax.ShapeDtypeStruct((rows, DX), F32),
                   jax.ShapeDtypeStruct((rows // tm, 8, tm), F32),
                   jax.ShapeDtypeStruct((8, 128), F32)],
        scratch_shapes=[pltpu.VMEM((8, 128), F32)],
        compiler_params=_cp(("arbitrary",)),
        name="outproj_router",
    )(*args)


def _route(lg, tri_ref, carry_scr):
    lane = lax.broadcasted_iota(jnp.int32, lg.shape, 1)
    lane_f = lane.astype(F32)
    big = jnp.float32(1e9)

    gmask = lane < N_GROUPS
    gl = jnp.where(gmask, lg, NEG)
    gm = jnp.max(gl, axis=1, keepdims=True)
    gidx = jnp.min(jnp.where(gl == gm, lane_f, big), axis=1, keepdims=True)
    g_val = 1.0 / jnp.sum(jnp.where(gmask, jnp.exp(gl - gm), 0.0), axis=1, keepdims=True)

    base = 8.0 + 8.0 * gidx
    emask = (lane_f >= base) & (lane_f < base + EPG)
    el = jnp.where(emask, lg, NEG)
    m1 = jnp.max(el, axis=1, keepdims=True)
    i1 = jnp.min(jnp.where(el == m1, lane_f, big), axis=1, keepdims=True)
    el2 = jnp.where(lane_f == i1, NEG, el)
    m2 = jnp.max(el2, axis=1, keepdims=True)
    i2 = jnp.min(jnp.where(el2 == m2, lane_f, big), axis=1, keepdims=True)
    r = jnp.exp(m2 - m1)
    w1 = g_val / (1.0 + r)
    w2 = w1 * r
    gates = jnp.where(lane_f == i1 - base, w1, jnp.where(lane_f == i2 - base, w2, 0.0))

    onehot = jnp.where(lane_f == gidx, 1.0, 0.0)
    before = jnp.dot(tri_ref[...], onehot.astype(BF16), preferred_element_type=F32)
    carry = carry_scr[0:1, :]
    rank = jnp.sum(jnp.where(lane_f == gidx, before + carry, 0.0), axis=1, keepdims=True)
    carry_scr[0:1, :] = carry + jnp.sum(onehot, axis=0, keepdims=True)

    return jnp.where(lane < EPG, gates, jnp.where(lane == 8, gidx, jnp.where(lane == 9, rank, 0.0)))


def _slot_kernel(rt_ref, st_ref, o_ref):
    g = rt_ref[0:1, :]
    slot = rt_ref[1:2, :]
    st = st_ref[...]
    for k in range(N_GROUPS):
        slot = slot + jnp.where(g == k, st[:, k:k + 1], 0.0)
    o_ref[...] = slot.astype(jnp.int32)


def _slots(route_t, starts_row):
    n, _, st = route_t.shape
    return pl.pallas_call(
        _slot_kernel,
        grid=(n,),
        in_specs=[pl.BlockSpec((None, 8, st), lambda i: (i, 0, 0)),
                  pl.BlockSpec((1, 128), lambda i: (0, 0))],
        out_specs=pl.BlockSpec((None, 1, st), lambda i: (i, 0, 0)),
        out_shape=jax.ShapeDtypeStruct((n, 1, st), jnp.int32),
        compiler_params=_cp(("arbitrary",)),
        name="moe_slots",
    )(route_t, starts_row)


def _slot_spec(slot3, tile, ahead=0):
    n, _, st = slot3.shape
    per = st // tile
    last = n * per - 1

    def idx(i, *_):
        t = jnp.minimum(i + ahead, last)
        return (t // per, 0, t % per)

    return pl.BlockSpec((None, 1, tile), idx, memory_space=pltpu.SMEM)


def _dispatch_kernel(slot_ref, pad_ref, h2x_ref, xs_ref, zx_scr, sem):
    i = pl.program_id(0)
    n_pad = pad_ref.shape[1]

    @pl.when(i == 0)
    def _():
        zx_scr[...] = jnp.zeros_like(zx_scr)

        def zbody(r, carry):
            pltpu.make_async_copy(zx_scr.at[pl.ds(0, 1)], xs_ref.at[pl.ds(pad_ref[0, r], 1)], sem.at[0]).start()
            return carry

        lax.fori_loop(0, n_pad, zbody, 0)
        pltpu.make_async_copy(xs_ref.at[pl.ds(0, n_pad)], xs_ref.at[pl.ds(0, n_pad)], sem.at[0]).wait()

    def body(j, carry):
        for u in range(2):
            r = 2 * j + u
            pltpu.make_async_copy(h2x_ref.at[pl.ds(r, 1)], xs_ref.at[pl.ds(slot_ref[0, r], 1)],
                                  sem.at[0]).start(priority=u)
        return carry

    lax.fori_loop(0, TD // 2, body, 0)
    pltpu.make_async_copy(h2x_ref, xs_ref.at[pl.ds(0, TD)], sem.at[0]).wait()


def _dispatch(slot3, pad_slots, h2x, n_sorted):
    rows = h2x.shape[0]
    n_pad = pad_slots.shape[0]
    return pl.pallas_call(
        _dispatch_kernel,
        grid=(rows // TD,),
        in_specs=[_slot_spec(slot3, TD),
                  pl.BlockSpec((1, n_pad), lambda i: (0, 0), memory_space=pltpu.SMEM),
                  pl.BlockSpec((TD, DX), lambda i: (i, 0))],
        out_specs=pl.BlockSpec(memory_space=pl.ANY),
        out_shape=jax.ShapeDtypeStruct((n_sorted, DX), F32),
        scratch_shapes=[pltpu.VMEM((8, DX), F32), pltpu.SemaphoreType.DMA((1,))],
        compiler_params=_cp(("arbitrary",)),
        name="moe_dispatch",
    )(slot3, pad_slots.reshape(1, n_pad), h2x)


def _moe_kernel(tg_ref, na_ref, x_ref, w1_ref, w3_ref, w2_ref, o_ref, hid_scr, w1b_scr, w3b_scr, w2b_scr):
    i = pl.program_id(0)

    @pl.when((i == 0) | (tg_ref[i] != tg_ref[jnp.maximum(i - 1, 0)]))
    def _():
        for e in range(EPG):
            cols = slice(e * EH, (e + 1) * EH)
            w1b_scr[:, cols] = w1_ref[:, cols].astype(BF16)
            w3b_scr[:, cols] = w3_ref[:, cols].astype(BF16)
            w2b_scr[cols, :] = w2_ref[cols, :].astype(BF16)

    @pl.when(i < na_ref[0])
    def _():
        x = x_ref[:, :D].astype(BF16)
        gt = x_ref[:, D:]
        for e in range(EPG):
            cols = slice(e * EH, (e + 1) * EH)
            h1 = jnp.dot(x, w1b_scr[:, cols], preferred_element_type=F32)
            h3 = jnp.dot(x, w3b_scr[:, cols], preferred_element_type=F32)
            hid_scr[:, cols] = (_silu(h1) * h3 * gt[:, e:e + 1]).astype(BF16)
        o_ref[...] = jnp.dot(hid_scr[...], w2b_scr[...], preferred_element_type=F32)

    @pl.when(i >= na_ref[0])
    def _():
        o_ref[...] = jnp.zeros_like(o_ref)


def _moe(tile_group, n_active, xs, w1, w3, w2, layer, n_tiles):
    x_idx = lambda i, tg, na: (jnp.minimum(i, na[0] - 1), 0)
    out_idx = lambda i, tg, na: (i, 0)
    w_idx = lambda i, tg, na: (layer, tg[i], 0, 0)
    once = pl.Buffered(1)
    grid_spec = pltpu.PrefetchScalarGridSpec(
        num_scalar_prefetch=2,
        grid=(n_tiles,),
        in_specs=[pl.BlockSpec((TS, DX), x_idx),
                  pl.BlockSpec((None, None, D, GH), w_idx, pipeline_mode=once),
                  pl.BlockSpec((None, None, D, GH), w_idx, pipeline_mode=once),
                  pl.BlockSpec((None, None, GH, D), w_idx, pipeline_mode=once)],
        out_specs=pl.BlockSpec((TS, D), out_idx),
        scratch_shapes=[pltpu.VMEM((TS, GH), BF16), pltpu.VMEM((D, GH), BF16), pltpu.VMEM((D, GH), BF16),
                        pltpu.VMEM((GH, D), BF16)],
    )
    return pl.pallas_call(
        _moe_kernel,
        grid_spec=grid_spec,
        out_shape=jax.ShapeDtypeStruct((xs.shape[0], D), F32),
        compiler_params=_cp(("arbitrary",)),
        name="moe_experts",
    )(tile_group, n_active, xs, w1, w3, w2)


def _combine_kernel(*refs, final, n_steps):
    if final:
        slot_ref, slot_next_ref, x1_ref, g2_ref, fg_ref, ys_ref, o_ref, buf, sem = refs
    else:
        slot_ref, slot_next_ref, x1_ref, g2_ref, ys_ref, o_ref, buf, sem = refs
    i = pl.program_id(0)

    def issue(sref, b):
        def body(j, carry):
            for u in range(2):
                r = 2 * j + u
                pltpu.make_async_copy(ys_ref.at[pl.ds(sref[0, r], 1)], buf.at[b, pl.ds(r, 1)],
                                      sem.at[b]).start(priority=u)
            return carry

        lax.fori_loop(0, TD // 2, body, 0)

    @pl.when(i == 0)
    def _():
        issue(slot_ref, 0)

    @pl.when(i + 1 < n_steps)
    def _():
        issue(slot_next_ref, (i + 1) % 2)

    b = i % 2
    pltpu.make_async_copy(ys_ref.at[pl.ds(0, TD)], buf.at[b], sem.at[b]).wait()

    x2 = x1_ref[...] + g2_ref[...] * buf[b]
    if final:
        ms = jnp.mean(x2 * x2, axis=-1, keepdims=True)
        x2 = x2 * lax.rsqrt(ms + EPS) * fg_ref[...]
    o_ref[...] = x2


def _combine(slot3, x1, g2, ys, n_mod, final_g=None):
    rows = x1.shape[0]
    n_steps = rows // TD
    final = final_g is not None
    mod_idx = lambda i: (jnp.minimum((i * TD) // L, n_mod), 0, 0)
    in_specs = [_slot_spec(slot3, TD), _slot_spec(slot3, TD, ahead=1),
                pl.BlockSpec((TD, D), lambda i: (i, 0)),
                pl.BlockSpec((None, 1, D), mod_idx)]
    args = [slot3, slot3, x1, g2]
    if final:
        in_specs.append(pl.BlockSpec((1, D), lambda i: (0, 0)))
        args.append(final_g.reshape(1, D))
    in_specs.append(pl.BlockSpec(memory_space=pl.ANY))
    args.append(ys)
    return pl.pallas_call(
        functools.partial(_combine_kernel, final=final, n_steps=n_steps),
        grid=(n_steps,),
        in_specs=in_specs,
        out_specs=pl.BlockSpec((TD, D), lambda i: (i, 0)),
        out_shape=jax.ShapeDtypeStruct((rows, D), F32),
        scratch_shapes=[pltpu.VMEM((2, TD, D), F32), pltpu.SemaphoreType.DMA((2,))],
        compiler_params=_cp(("arbitrary",)),
        name="moe_combine_final" if final else "moe_combine",
    )(*args)


def _sort_plan(counts_row, rows):
    counts = counts_row[:N_GROUPS].astype(jnp.int32)
    padded = ((counts + TS - 1) // TS) * TS
    ends = jnp.cumsum(padded)
    starts = ends - padded
    starts_row = jnp.zeros((1, 128), F32).at[0, :N_GROUPS].set(starts.astype(F32))
    n_tiles = rows // TS + N_GROUPS
    n_active = (ends[-1] // TS).reshape(1)
    tile_start = jnp.arange(n_tiles, dtype=jnp.int32) * TS
    tile_group = jnp.minimum(jnp.sum(tile_start[:, None] >= ends[None, :], axis=1), N_GROUPS - 1).astype(jnp.int32)
    r = jnp.arange(TS, dtype=jnp.int32)
    pad_real = ((starts + counts)[:, None] + r[None, :]).reshape(-1)
    is_pad = pad_real < jnp.repeat(ends, TS)
    tail = ends[-1] + jnp.cumsum(jnp.logical_not(is_pad).astype(jnp.int32)) - 1
    pad_slots = jnp.where(is_pad, pad_real, tail)
    return starts_row, pad_slots, tile_group, n_active, n_tiles, n_tiles * TS


def _moe_sorted(h2x, route_t, cnt, w1, w3, w2, layer):
    rows = h2x.shape[0]
    starts_row, pad_slots, tile_group, n_active, n_tiles, n_sorted = _sort_plan(cnt[0], rows)
    slot3 = _slots(route_t, starts_row)
    xs = _dispatch(slot3, pad_slots, h2x, n_sorted)
    ys = _moe(tile_group, n_active, xs, w1, w3, w2, layer, n_tiles)
    return slot3, ys


def _router_weights(gw, gb, ew, eb):
    wr = jnp.zeros((D, 128), F32).at[:, :N_GROUPS].set(gw).at[:, 8:8 + N_GROUPS * EPG].set(ew)
    br = jnp.zeros((1, 128), F32).at[0, :N_GROUPS].set(gb).at[0, 8:8 + N_GROUPS * EPG].set(eb)
    hi = wr.astype(BF16)
    lo = (wr - hi.astype(F32)).astype(BF16)
    return jnp.concatenate([hi, lo], axis=1), br


def kernel(x, c, ctx, c_ctx, ada_w, ada_b, norm1_g, norm2_g, w_in_e, conv_w, na_rpb, w_out_e, w_in_o,
           ret_decay_f, ret_decay_b, w_out_o, router_gw, router_gb, router_ew, router_eb,
           moe_w1, moe_w3, moe_w2, final_g):
    nb = x.shape[0]
    assert x.shape[1:] == (L, D) and ctx.shape[1:] == (CTX, D) and nb < 16 and ada_w.shape[0] == 2
    t_lat = nb * L
    t_ctx = nb * CTX
    t_all = t_lat + t_ctx
    tm = 1024 if t_ctx % 1024 == 0 else 512

    xsrc = (x.reshape(t_lat, D), ctx.reshape(t_ctx, D))
    c16 = jnp.zeros((16, D), F32).at[:nb].set(c).at[nb].set(c_ctx)
    mod = _ada(c16, ada_w, ada_b).reshape(2, 16, 6, 1, D)
    mods = [[mod[l, :, k] for k in range(6)] for l in range(2)]

    sh1, sc1, g1, sh2, sc2, g2 = mods[0]
    qkv = _inproj(xsrc, norm1_g[0], sh1, sc1, w_in_e[0].astype(BF16), t_lat, nb, tm)
    y_conv = (_conv(qkv, conv_w[0], L, nb, 0), _conv(qkv, conv_w[0], CTX, nb, t_lat // CTX))
    y_attn = (_natten(qkv, _natten_bias(na_rpb[0]), nb), _ctx_attn(qkv, nb))
    w_out = w_out_e[0].astype(BF16)
    wr2, br = _router_weights(router_gw[0], router_gb[0], router_ew[0], router_eb[0])
    x1, h2x, route_t, cnt = _outproj([y_conv, y_attn], [w_out[:CONV_CH], w_out[CONV_CH:]], xsrc, g1, sh2, sc2,
                                     norm2_g[0], wr2, br, t_all, t_lat, nb, tm)
    slot3, ys = _moe_sorted(h2x, route_t, cnt, moe_w1, moe_w3, moe_w2, 0)

    sh1, sc1, g1n, sh2n, sc2n, g2n = mods[1]
    qkvg, xa = _inproj_comb(slot3, ys, x1, g2, norm1_g[1], sh1, sc1, w_in_o[0].astype(BF16), t_lat, nb, tm,
                            _rope_tables(tm))
    g1, sh2, sc2, g2 = g1n, sh2n, sc2n, g2n
    lg = jnp.stack([jnp.log1p(-jnp.exp(ret_decay_f[0].astype(F32))),
                    jnp.log1p(-jnp.exp(ret_decay_b[0].astype(F32)))])
    y_ret = _retention(qkvg, lg, nb)
    wr2, br = _router_weights(router_gw[1], router_gb[1], router_ew[1], router_eb[1])
    x1, h2x, route_t, cnt = _outproj([(y_ret,)], [w_out_o[0].astype(BF16)], (xa,), g1, sh2, sc2, norm2_g[1],
                                     wr2, br, t_lat, t_lat, nb, tm)
    slot3, ys = _moe_sorted(h2x, route_t, cnt, moe_w1, moe_w3, moe_w2, 1)
    out = _combine(slot3, x1, g2, ys, nb, final_g)
    return out.reshape(nb, L, D)
```

```python
import functools
import math

import numpy as np
import jax
import jax.numpy as jnp
from jax import lax
from jax.experimental import pallas as pl
from jax.experimental.pallas import tpu as pltpu

F32 = jnp.float32
BF16 = jnp.bfloat16
HIGHEST = lax.Precision.HIGHEST

D = 1024
L = 4096
GRID_W = 64
GRID_H = L // GRID_W
CTX = 256
EPS = 1e-6

CONV_CH = 512
NA_HEADS = 8
NA_DH = 64
WIN_H = 8
WIN_W = 16
EVEN_IN = 3072

RET_HEADS = 4
RET_DK = 256
RET_DV = 512
RET_C = 256
ODD_IN = 6144
ROPE_BASE = 10000.0

N_GROUPS = 4
EPG = 8
EH = 256
GH = EPG * EH

NEG = -1e30

NA_QR = 4
NA_KR = NA_QR + WIN_H - 1
NA_NQ = NA_QR * GRID_W
NA_NK = NA_KR * GRID_W
NA_WS_MAX = GRID_H - NA_KR

DX = D + 128
TS = 512
TD = 512

VMEM_LIMIT = 56 * 1024 * 1024


def _cp(sem, vmem=VMEM_LIMIT):
    return pltpu.CompilerParams(dimension_semantics=sem, vmem_limit_bytes=vmem)


def _silu(v):
    return v * (1.0 / (1.0 + jnp.exp(-v)))


def _src_specs(src, tm, lat_tiles):
    tail = tuple(src[0].shape[1:])
    zeros = (0,) * len(tail)
    if len(src) == 1:
        return [pl.BlockSpec((tm,) + tail, lambda i, *_: (i,) + zeros)]
    return [pl.BlockSpec((tm,) + tail, lambda i, *_: (jnp.minimum(i, lat_tiles - 1),) + zeros),
            pl.BlockSpec((tm,) + tail, lambda i, *_: (jnp.maximum(i - lat_tiles, 0),) + zeros)]


def _src_load(refs, i, lat_tiles):
    if len(refs) == 1:
        return refs[0][...]
    return jnp.where(i < lat_tiles, refs[0][...], refs[1][...])


def _ada_kernel(c_ref, w_ref, b_ref, o_ref):
    s = _silu(c_ref[...])
    o_ref[...] = jnp.dot(s, w_ref[...], precision=HIGHEST, preferred_element_type=F32) + b_ref[...]


def _ada(c16, ada_w, ada_b):
    depth = ada_w.shape[0]
    tn = 1536
    return pl.pallas_call(
        _ada_kernel,
        grid=(depth, 6 * D // tn),
        in_specs=[
            pl.BlockSpec((16, D), lambda l, j: (0, 0)),
            pl.BlockSpec((None, D, tn), lambda l, j: (l, 0, j)),
            pl.BlockSpec((None, 1, tn), lambda l, j: (l, 0, j)),
        ],
        out_specs=pl.BlockSpec((None, 16, tn), lambda l, j: (l, 0, j)),
        out_shape=jax.ShapeDtypeStruct((depth, 16, 6 * D), F32),
        compiler_params=_cp(("arbitrary", "arbitrary")),
        name="ada_mod",
    )(c16, ada_w, ada_b.reshape(depth, 1, 6 * D))


def _inproj_kernel(*refs, rope, tn, n_src, lat_tiles):
    x_refs, refs = refs[:n_src], refs[n_src:]
    if rope:
        g_ref, sh_ref, sc_ref, w_ref, cos_ref, sin_ref, o_ref, h_scr = refs
    else:
        g_ref, sh_ref, sc_ref, w_ref, o_ref, h_scr = refs
    i = pl.program_id(0)
    j = pl.program_id(1)

    @pl.when(j == 0)
    def _():
        x = _src_load(x_refs, i, lat_tiles)
        ms = jnp.mean(x * x, axis=-1, keepdims=True)
        y = x * lax.rsqrt(ms + EPS) * g_ref[...]
        h_scr[...] = (y * (1.0 + sc_ref[...]) + sh_ref[...]).astype(BF16)

    acc = jnp.dot(h_scr[...], w_ref[...], preferred_element_type=F32)
    if not rope:
        o_ref[...] = acc.astype(o_ref.dtype)
        return

    @pl.when(j < 2)
    def _():
        cos = cos_ref[...]
        sin = sin_ref[...]
        scale = jnp.where(j == 1, RET_DK ** -0.5, 1.0).astype(F32)
        pieces = []
        for cblk in range(tn // 128):
            t = acc[:, cblk * 128:(cblk + 1) * 128]
            half = (cblk % 2) * 128
            r = pltpu.roll(t, 64, axis=1)
            pieces.append(t * cos[:, half:half + 128] + r * sin[:, half:half + 128])
        o_ref[...] = (jnp.concatenate(pieces, axis=1) * scale).astype(o_ref.dtype)

    @pl.when(j >= 2)
    def _():
        o_ref[...] = acc.astype(o_ref.dtype)


def _inproj(xsrc, gain, sh, sc, w, n_lat_rows, n_mod, tm, rope_tables=None):
    rows = sum(a.shape[0] for a in xsrc)
    n = w.shape[1]
    tn = 1024
    lat_tiles = n_lat_rows // tm
    mod_idx = lambda i, j: (jnp.minimum((i * tm) // L, n_mod), 0, 0)
    in_specs = _src_specs(xsrc, tm, lat_tiles) + [
        pl.BlockSpec((1, D), lambda i, j: (0, 0)),
        pl.BlockSpec((None, 1, D), mod_idx),
        pl.BlockSpec((None, 1, D), mod_idx),
        pl.BlockSpec((D, tn), lambda i, j: (0, j)),
    ]
    args = list(xsrc) + [gain.reshape(1, D), sh, sc, w]
    rope = rope_tables is not None
    if rope:
        per_seq = L // tm
        tab_idx = lambda i, j: (jnp.where(i < lat_tiles, i % per_seq, per_seq), 0)
        in_specs += [pl.BlockSpec((tm, RET_DK), tab_idx), pl.BlockSpec((tm, RET_DK), tab_idx)]
        args += list(rope_tables)
    return pl.pallas_call(
        functools.partial(_inproj_kernel, rope=rope, tn=tn, n_src=len(xsrc), lat_tiles=lat_tiles),
        grid=(rows // tm, n // tn),
        in_specs=in_specs,
        out_specs=pl.BlockSpec((tm, tn), lambda i, j: (i, j)),
        out_shape=jax.ShapeDtypeStruct((rows, n), BF16),
        scratch_shapes=[pltpu.VMEM((tm, D), BF16)],
        compiler_params=_cp(("arbitrary", "arbitrary")),
        name="inproj_rope" if rope else "inproj",
    )(*args)


def _inproj_comb_kernel(slot_ref, slotn_ref, x1_ref, g2_ref, g_ref, sh_ref, sc_ref, w_ref, cos_ref, sin_ref, ys_ref,
                        o_ref, xa_ref, h_scr, buf, sem, *, tn, tm, n_tiles, n_j, ch):
    i = pl.program_id(0)
    j = pl.program_id(1)
    rows_buf = ch * n_j

    def start_row(sref, b, r, u):
        src = sref[0, jnp.minimum(r, tm - 1)]
        pltpu.make_async_copy(ys_ref.at[pl.ds(src, 1)], buf.at[b, pl.ds(r, 1)], sem.at[b]).start(priority=u)

    @pl.when(j == 0)
    def _():
        @pl.when(i == 0)
        def _():
            def body(jj, carry):
                for u in range(2):
                    start_row(slot_ref, 0, 2 * jj + u, u)
                return carry

            lax.fori_loop(0, rows_buf // 2, body, 0)

        b = i % 2
        pltpu.make_async_copy(ys_ref.at[pl.ds(0, rows_buf)], buf.at[b], sem.at[b]).wait()
        x = x1_ref[...] + g2_ref[...] * buf[b, :tm, :]
        xa_ref[...] = x
        ms = jnp.mean(x * x, axis=-1, keepdims=True)
        coef = g_ref[...] * (1.0 + sc_ref[...])
        h_scr[...] = (x * lax.rsqrt(ms + EPS) * coef + sh_ref[...]).astype(BF16)

    bn = (i + 1) % 2
    for u in range(ch):
        start_row(slotn_ref, bn, j * ch + u, u % 2)

    acc = jnp.dot(h_scr[...], w_ref[...], preferred_element_type=F32)

    @pl.when(j < 2)
    def _():
        cos = cos_ref[...]
        sin = sin_ref[...]
        scale = jnp.where(j == 1, RET_DK ** -0.5, 1.0).astype(F32)
        pieces = []
        for cblk in range(tn // 128):
            t = acc[:, cblk * 128:(cblk + 1) * 128]
            half = (cblk % 2) * 128
            r = pltpu.roll(t, 64, axis=1)
            pieces.append(t * cos[:, half:half + 128] + r * sin[:, half:half + 128])
        o_ref[...] = (jnp.concatenate(pieces, axis=1) * scale).astype(o_ref.dtype)

    @pl.when(j >= 2)
    def _():
        o_ref[...] = acc.astype(o_ref.dtype)

    @pl.when((i == n_tiles - 1) & (j == n_j - 1))
    def _():
        pltpu.make_async_copy(ys_ref.at[pl.ds(0, rows_buf)], buf.at[bn], sem.at[bn]).wait()


def _inproj_comb(slot3, ys, x1, g2, gain, sh, sc, w, n_lat_rows, n_mod, tm, rope_tables):
    rows = x1.shape[0]
    n = w.shape[1]
    tn = 1024
    n_tiles = rows // tm
    lat_tiles = n_lat_rows // tm
    per_seq = L // tm
    mod_idx = lambda i, j: (jnp.minimum((i * tm) // L, n_mod), 0, 0)
    tab_idx = lambda i, j: (jnp.where(i < lat_tiles, i % per_seq, per_seq), 0)
    n_j = n // tn
    ch = -(-tm // n_j)
    while (ch * n_j) % 8:
        ch += 1
    in_specs = [_slot_spec(slot3, tm), _slot_spec(slot3, tm, ahead=1),
                pl.BlockSpec((tm, D), lambda i, j: (i, 0)),
                pl.BlockSpec((None, 1, D), mod_idx),
                pl.BlockSpec((1, D), lambda i, j: (0, 0)),
                pl.BlockSpec((None, 1, D), mod_idx),
                pl.BlockSpec((None, 1, D), mod_idx),
                pl.BlockSpec((D, tn), lambda i, j: (0, j)),
                pl.BlockSpec((tm, RET_DK), tab_idx),
                pl.BlockSpec((tm, RET_DK), tab_idx),
                pl.BlockSpec(memory_space=pl.ANY)]
    return pl.pallas_call(
        functools.partial(_inproj_comb_kernel, tn=tn, tm=tm, n_tiles=n_tiles, n_j=n_j, ch=ch),
        grid=(n_tiles, n_j),
        in_specs=in_specs,
        out_specs=[pl.BlockSpec((tm, tn), lambda i, j: (i, j)),
                   pl.BlockSpec((tm, D), lambda i, j: (i, 0))],
        out_shape=[jax.ShapeDtypeStruct((rows, n), BF16),
                   jax.ShapeDtypeStruct((rows, D), F32)],
        scratch_shapes=[pltpu.VMEM((tm, D), BF16), pltpu.VMEM((2, ch * n_j, D), F32),
                        pltpu.SemaphoreType.DMA((2,))],
        compiler_params=_cp(("arbitrary", "arbitrary")),
        name="inproj_rope_combine",
    )(slot3, slot3, x1, g2, gain.reshape(1, D), sh, sc, w, rope_tables[0], rope_tables[1], ys)


def _rope_tables(tm):
    pos = np.arange(L)
    half = RET_DK // 4
    freqs = ROPE_BASE ** (-np.arange(half, dtype=np.float64) / half)
    ang_r = (pos // GRID_W)[:, None] * freqs[None, :]
    ang_c = (pos % GRID_W)[:, None] * freqs[None, :]
    cos = np.concatenate([np.cos(ang_r), np.cos(ang_r), np.cos(ang_c), np.cos(ang_c)], axis=1)
    sin = np.concatenate([-np.sin(ang_r), np.sin(ang_r), -np.sin(ang_c), np.sin(ang_c)], axis=1)
    cos = np.concatenate([cos, np.ones((tm, RET_DK))], axis=0)
    sin = np.concatenate([sin, np.zeros((tm, RET_DK))], axis=0)
    return jnp.asarray(cos, F32), jnp.asarray(sin, F32)


def _conv_kernel(ab_ref, ac_ref, ax_ref, w_ref, o_ref, *, seq):
    u = ac_ref[...].astype(F32) * ax_ref[...].astype(F32)
    row = lax.broadcasted_iota(jnp.int32, u.shape, 0)
    up = jnp.where(row == 0, 0.0, pltpu.roll(u, 1, axis=0))
    un = jnp.where(row == seq - 1, 0.0, pltpu.roll(u, seq - 1, axis=0))
    w = w_ref[...]
    y = ab_ref[...].astype(F32) * (up * w[0:1, :] + u * w[1:2, :] + un * w[2:3, :])
    o_ref[...] = y.astype(o_ref.dtype)


def _conv(qkv, conv_w, seq, n_seq, row_blk0):
    nb = CONV_CH // 128
    return pl.pallas_call(
        functools.partial(_conv_kernel, seq=seq),
        grid=(n_seq, nb),
        in_specs=[
            pl.BlockSpec((seq, 128), lambda b, j: (row_blk0 + b, j)),
            pl.BlockSpec((seq, 128), lambda b, j: (row_blk0 + b, nb + j)),
            pl.BlockSpec((seq, 128), lambda b, j: (row_blk0 + b, 2 * nb + j)),
            pl.BlockSpec((3, 128), lambda b, j: (0, j)),
        ],
        out_specs=pl.BlockSpec((seq, 128), lambda b, j: (b, j)),
        out_shape=jax.ShapeDtypeStruct((n_seq * seq, CONV_CH), BF16),
        compiler_params=_cp(("arbitrary", "arbitrary")),
        name="short_conv",
    )(qkv, qkv, qkv, conv_w)


def _natten_tcol(rpb):
    qc = np.arange(GRID_W)
    kc = np.arange(GRID_W)
    cs = np.clip(qc - WIN_W // 2, 0, GRID_W - WIN_W)
    col_ok = (kc[None, :] >= cs[:, None]) & (kc[None, :] < cs[:, None] + WIN_W)
    dx = np.clip(kc[None, :] - qc[:, None] + WIN_W - 1, 0, 2 * WIN_W - 2)
    tcol = jnp.where(col_ok[None, None], rpb.astype(F32)[:, :, dx], NEG)
    return tcol.reshape(NA_HEADS // 2, 2, 2 * WIN_H - 1, GRID_W, GRID_W)


def _natten_row_classes():
    table = []
    for r0 in (0, 2 * NA_QR, GRID_H - NA_QR):
        ws = int(np.clip(r0 - WIN_H // 2, 0, NA_WS_MAX))
        rows = []
        for q in range(NA_QR):
            qr = r0 + q
            rs = int(np.clip(qr - WIN_H // 2, 0, GRID_H - WIN_H))
            rows.append([(ws + k) - qr + WIN_H - 1 if rs <= ws + k < rs + WIN_H else None for k in range(NA_KR)])
        table.append(rows)
    return table


def _attend(qa, kw, vw, kc, vc, bias):
    dn = (((1,), (1,)), ((), ()))
    s2 = lax.dot_general(qa, kc, dn, preferred_element_type=F32)
    m = jnp.max(s2, axis=1, keepdims=True)
    if kw is not None:
        s1 = lax.dot_general(qa, kw, dn, preferred_element_type=F32) + bias
        m = jnp.maximum(m, jnp.max(s1, axis=1, keepdims=True))
        p1 = jnp.exp(s1 - m)
    p2 = jnp.exp(s2 - m)
    den = jnp.sum(p2, axis=1, keepdims=True)
    o = jnp.dot(p2.astype(BF16), vc, preferred_element_type=F32)
    if kw is not None:
        den = den + jnp.sum(p1, axis=1, keepdims=True)
        o = o + jnp.dot(p1.astype(BF16), vw, preferred_element_type=F32)
    return o / den


NA_SUB = 4


def _natten_kernel(q_ref, k_ref, v_ref, kc_ref, vc_ref, tcol_ref, o_ref, bias_ref):
    n_rb = GRID_H // NA_QR

    @pl.when((pl.program_id(1) == 0) & (pl.program_id(2) == 0))
    def _():
        neg = jnp.full((GRID_W, GRID_W), NEG, F32)
        for cls, rows in enumerate(_natten_row_classes()):
            for a in range(2):
                for qr, dys in enumerate(rows):
                    blocks = [neg if dy is None else tcol_ref[a, dy] for dy in dys]
                    bias_ref[cls, a, qr * GRID_W:(qr + 1) * GRID_W, :] = jnp.concatenate(blocks, axis=1)

    lane = lax.broadcasted_iota(jnp.int32, (NA_NQ, 128), 1)
    for sb in range(NA_SUB):
        rb = pl.program_id(2) * NA_SUB + sb
        ws = jnp.clip(rb * NA_QR - WIN_H // 2, 0, NA_WS_MAX)
        start = pl.multiple_of(ws * GRID_W, GRID_W)
        kw = k_ref[pl.ds(start, NA_NK), :]
        vw = v_ref[pl.ds(start, NA_NK), :]
        cls = jnp.where(rb == 0, 0, jnp.where(rb == n_rb - 1, 2, 1))
        q = q_ref[sb * NA_NQ:(sb + 1) * NA_NQ, :] * (NA_DH ** -0.5)
        zero = jnp.zeros_like(q)
        o0 = _attend(jnp.where(lane < NA_DH, q, zero), kw, vw, kc_ref[...], vc_ref[...], bias_ref[cls, 0])
        o1 = _attend(jnp.where(lane >= NA_DH, q, zero), kw, vw, kc_ref[...], vc_ref[...], bias_ref[cls, 1])
        o_ref[sb * NA_NQ:(sb + 1) * NA_NQ, :] = jnp.where(lane < NA_DH, o0, o1).astype(o_ref.dtype)


def _natten(qkv, tcol, nb):
    n_steps = GRID_H // NA_QR // NA_SUB
    qcol, kcol, vcol = 3 * CONV_CH // 128, 3 * CONV_CH // 128 + 4, 3 * CONV_CH // 128 + 8
    ctx_blk0 = nb * L // CTX
    return pl.pallas_call(
        _natten_kernel,
        grid=(NA_HEADS // 2, nb, n_steps),
        in_specs=[
            pl.BlockSpec((NA_SUB * NA_NQ, 128), lambda h, b, r: (b * n_steps + r, qcol + h)),
            pl.BlockSpec((L, 128), lambda h, b, r: (b, kcol + h)),
            pl.BlockSpec((L, 128), lambda h, b, r: (b, vcol + h)),
            pl.BlockSpec((CTX, 128), lambda h, b, r: (ctx_blk0 + b, kcol + h)),
            pl.BlockSpec((CTX, 128), lambda h, b, r: (ctx_blk0 + b, vcol + h)),
            pl.BlockSpec((None, 2, 2 * WIN_H - 1, GRID_W, GRID_W), lambda h, b, r: (h, 0, 0, 0, 0)),
        ],
        out_specs=pl.BlockSpec((NA_SUB * NA_NQ, 128), lambda h, b, r: (b * n_steps + r, h)),
        out_shape=jax.ShapeDtypeStruct((nb * L, NA_HEADS * NA_DH), BF16),
        scratch_shapes=[pltpu.VMEM((3, 2, NA_NQ, NA_NK), F32)],
        compiler_params=_cp(("arbitrary", "arbitrary", "arbitrary")),
        name="natten",
    )(qkv, qkv, qkv, qkv, qkv, tcol)


def _ctx_attn_kernel(q_ref, kc_ref, vc_ref, o_ref):
    q = q_ref[...] * (NA_DH ** -0.5)
    lane = lax.broadcasted_iota(jnp.int32, q.shape, 1)
    zero = jnp.zeros_like(q)
    o0 = _attend(jnp.where(lane < NA_DH, q, zero), None, None, kc_ref[...], vc_ref[...], None)
    o1 = _attend(jnp.where(lane >= NA_DH, q, zero), None, None, kc_ref[...], vc_ref[...], None)
    o_ref[...] = jnp.where(lane < NA_DH, o0, o1).astype(o_ref.dtype)


def _ctx_attn(qkv, nb):
    qcol, kcol, vcol = 3 * CONV_CH // 128, 3 * CONV_CH // 128 + 4, 3 * CONV_CH // 128 + 8
    ctx_blk0 = nb * L // CTX
    return pl.pallas_call(
        _ctx_attn_kernel,
        grid=(nb, NA_HEADS // 2),
        in_specs=[
            pl.BlockSpec((CTX, 128), lambda b, h: (ctx_blk0 + b, qcol + h)),
            pl.BlockSpec((CTX, 128), lambda b, h: (ctx_blk0 + b, kcol + h)),
            pl.BlockSpec((CTX, 128), lambda b, h: (ctx_blk0 + b, vcol + h)),
        ],
        out_specs=pl.BlockSpec((CTX, 128), lambda b, h: (b, h)),
        out_shape=jax.ShapeDtypeStruct((nb * CTX, NA_HEADS * NA_DH), BF16),
        compiler_params=_cp(("arbitrary", "arbitrary")),
        name="ctx_attn",
    )(qkv, qkv, qkv)


def _retention_kernel(lg_ref, q_ref, k_ref, v_ref, g_ref, kc_ref, vc_ref, o_ref, y_scr, stf_scr, stb_scr):
    h = pl.program_id(1)
    n_chunks = L // RET_C
    pos_r = lax.broadcasted_iota(jnp.int32, (RET_C, 1), 0).astype(F32)
    rel = pos_r - lax.broadcasted_iota(jnp.int32, (1, RET_C), 1).astype(F32)
    dn_t = (((1,), (1,)), ((), ()))
    dn_k = (((0,), (0,)), ((), ()))

    def decays(lg, fwd):
        if fwd:
            d_intra = jnp.where(rel >= 0, jnp.exp(lg * jnp.maximum(rel, 0.0)), 0.0)
            d_q = jnp.exp(lg * (pos_r + 1.0))
            d_k = jnp.exp(lg * (RET_C - 1.0 - pos_r))
        else:
            d_intra = jnp.where(rel <= 0, jnp.exp(lg * jnp.maximum(-rel, 0.0)), 0.0)
            d_q = jnp.exp(lg * (RET_C - pos_r))
            d_k = jnp.exp(lg * pos_r)
        d_c = jnp.exp(jnp.full((1, 1), RET_C, F32) * lg)
        return d_intra, d_q, d_k, d_c

    def ctx_state(st_scr, d_k, d_c, order):
        st_scr[...] = jnp.zeros_like(st_scr)
        for jc in order:
            kd = (kc_ref[jc * RET_C:(jc + 1) * RET_C, :].astype(F32) * d_k).astype(BF16)
            vc = vc_ref[jc * RET_C:(jc + 1) * RET_C, :]
            st_scr[...] = st_scr[...] * d_c + lax.dot_general(kd, vc, dn_k, preferred_element_type=F32)

    def chunk_out(st_scr, c, d_intra, d_q, d_k, d_c):
        r0 = pl.multiple_of(c * RET_C, RET_C)
        q = q_ref[pl.ds(r0, RET_C), :]
        k = k_ref[pl.ds(r0, RET_C), :]
        v = v_ref[pl.ds(r0, RET_C), :]
        s = lax.dot_general(q, k, dn_t, preferred_element_type=F32) * d_intra
        st = st_scr[...]
        o = (jnp.dot(s.astype(BF16), v, preferred_element_type=F32)
             + jnp.dot(q, st.astype(BF16), preferred_element_type=F32) * d_q)
        kd = (k.astype(F32) * d_k).astype(BF16)
        st_scr[...] = st * d_c + lax.dot_general(kd, v, dn_k, preferred_element_type=F32)
        return r0, o

    def finish(r0, y):
        y = y * lax.rsqrt(jnp.mean(y * y, axis=-1, keepdims=True) + EPS)
        g = g_ref[pl.ds(r0, RET_C), :].astype(F32)
        o_ref[pl.ds(r0, RET_C), :] = (_silu(g) * y).astype(o_ref.dtype)

    dec_f = decays(lg_ref[0, h], True)
    dec_b = decays(lg_ref[1, h], False)
    ctx_state(stf_scr, dec_f[2], dec_f[3], range(CTX // RET_C))
    ctx_state(stb_scr, dec_b[2], dec_b[3], range(CTX // RET_C - 1, -1, -1))

    def first_half(i, carry):
        r0, o = chunk_out(stf_scr, i, *dec_f)
        y_scr[pl.ds(r0, RET_C), :] = o
        r1, o = chunk_out(stb_scr, n_chunks - 1 - i, *dec_b)
        y_scr[pl.ds(r1, RET_C), :] = o
        return carry

    lax.fori_loop(0, n_chunks // 2, first_half, 0)

    def second_half(i, carry):
        r0, o = chunk_out(stf_scr, i, *dec_f)
        finish(r0, y_scr[pl.ds(r0, RET_C), :] + o)
        r1, o = chunk_out(stb_scr, n_chunks - 1 - i, *dec_b)
        finish(r1, y_scr[pl.ds(r1, RET_C), :] + o)
        return carry

    lax.fori_loop(n_chunks // 2, n_chunks, second_half, 0)


def _retention(qkvg, lg, nb):
    ctx_blk0 = nb * L // CTX
    kq = RET_HEADS
    vq = 2 * RET_HEADS * RET_DK // RET_DV
    gq = vq + RET_HEADS
    return pl.pallas_call(
        _retention_kernel,
        grid=(nb, RET_HEADS),
        in_specs=[
            pl.BlockSpec(memory_space=pltpu.SMEM),
            pl.BlockSpec((L, RET_DK), lambda b, h: (b, h)),
            pl.BlockSpec((L, RET_DK), lambda b, h: (b, kq + h)),
            pl.BlockSpec((L, RET_DV), lambda b, h: (b, vq + h)),
            pl.BlockSpec((L, RET_DV), lambda b, h: (b, gq + h)),
            pl.BlockSpec((CTX, RET_DK), lambda b, h: (ctx_blk0 + b, kq + h)),
            pl.BlockSpec((CTX, RET_DV), lambda b, h: (ctx_blk0 + b, vq + h)),
        ],
        out_specs=pl.BlockSpec((L, RET_DV), lambda b, h: (b, h)),
        out_shape=jax.ShapeDtypeStruct((nb * L, RET_HEADS * RET_DV), BF16),
        scratch_shapes=[pltpu.VMEM((L, RET_DV), F32), pltpu.VMEM((RET_DK, RET_DV), F32),
                        pltpu.VMEM((RET_DK, RET_DV), F32)],
        compiler_params=_cp(("arbitrary", "arbitrary")),
        name="retention",
    )(lg, qkvg, qkvg, qkvg, qkvg, qkvg, qkvg)


def _outproj_kernel(*refs, src_lens, lat_tiles):
    i = pl.program_id(0)
    srcs = []
    for n in src_lens:
        srcs.append(refs[:n])
        refs = refs[n:]
    n_act = len(src_lens) - 1
    w_refs, refs = refs[:n_act], refs[n_act:]
    (g1_ref, sh2_ref, sc2_ref, n2_ref, wr_ref, br_ref, tri_ref,
     x1_ref, h2x_ref, rt_ref, cnt_ref, carry_scr) = refs

    @pl.when(i == 0)
    def _():
        carry_scr[...] = jnp.zeros_like(carry_scr)

    y = None
    for a_src, w_ref in zip(srcs[:n_act], w_refs):
        t = jnp.dot(_src_load(a_src, i, lat_tiles), w_ref[...], preferred_element_type=F32)
        y = t if y is None else y + t
    x1 = _src_load(srcs[n_act], i, lat_tiles) + g1_ref[...] * y
    x1_ref[...] = x1
    ms = jnp.mean(x1 * x1, axis=-1, keepdims=True)
    coef = n2_ref[...] * (1.0 + sc2_ref[...])
    h2 = x1 * lax.rsqrt(ms + EPS) * coef + sh2_ref[...]
    h2x_ref[:, :D] = h2
    h_hi = h2.astype(BF16)
    h_lo = (h2 - h_hi.astype(F32)).astype(BF16)
    p = jnp.dot(h_hi, wr_ref[...], preferred_element_type=F32)
    q = jnp.dot(h_lo, wr_ref[:, :128], preferred_element_type=F32)
    route = _route(p[:, :128] + p[:, 128:] + q + br_ref[...], tri_ref, carry_scr)
    h2x_ref[:, D:] = route
    rt_ref[...] = route.T[8:16, :]
    cnt_ref[...] = carry_scr[...]


def _outproj(acts, ws, xsrc, g1, sh2, sc2, n2g, wr2, br, rows, n_lat_rows, n_mod, tm):
    lat_tiles = n_lat_rows // tm
    tri = jnp.asarray(np.tril(np.ones((tm, tm), np.float32), -1), BF16)
    mod_idx = lambda i: (jnp.minimum((i * tm) // L, n_mod), 0, 0)
    in_specs, args = [], []
    for src in list(acts) + [xsrc]:
        in_specs += _src_specs(src, tm, lat_tiles)
        args += list(src)
    in_specs += [pl.BlockSpec(w.shape, lambda i: (0, 0)) for w in ws]
    in_specs += [pl.BlockSpec((None, 1, D), mod_idx),
                 pl.BlockSpec((None, 1, D), mod_idx),
                 pl.BlockSpec((None, 1, D), mod_idx),
                 pl.BlockSpec((1, D), lambda i: (0, 0)),
                 pl.BlockSpec((D, 256), lambda i: (0, 0)),
                 pl.BlockSpec((1, 128), lambda i: (0, 0)),
                 pl.BlockSpec((tm, tm), lambda i: (0, 0))]
    args += list(ws) + [g1, sh2, sc2, n2g.reshape(1, D), wr2, br, tri]
    return pl.pallas_call(
        functools.partial(_outproj_kernel, src_lens=tuple(len(s) for s in list(acts) + [xsrc]),
                          lat_tiles=lat_tiles),
        grid=(rows // tm,),
        in_specs=in_specs,
        out_specs=[pl.BlockSpec((tm, D), lambda i: (i, 0)),
                   pl.BlockSpec((tm, DX), lambda i: (i, 0)),
                   pl.BlockSpec((None, 8, tm), lambda i: (i, 0, 0)),
                   pl.BlockSpec((8, 128), lambda i: (0, 0))],
        out_shape=[jax.ShapeDtypeStruct((rows, D), F32),
                   jax.ShapeDtypeStruct((rows, DX), F32),
                   jax.ShapeDtypeStruct((rows // tm, 8, tm), F32),
                   jax.ShapeDtypeStruct((8, 128), F32)],
        scratch_shapes=[pltpu.VMEM((8, 128), F32)],
        compiler_params=_cp(("arbitrary",)),
        name="outproj_router",
    )(*args)


def _route(lg, tri_ref, carry_scr):
    lane = lax.broadcasted_iota(jnp.int32, lg.shape, 1)
    lane_f = lane.astype(F32)
    big = jnp.float32(1e9)

    gmask = lane < N_GROUPS
    gl = jnp.where(gmask, lg, NEG)
    gm = jnp.max(gl, axis=1, keepdims=True)
    gidx = jnp.min(jnp.where(gl == gm, lane_f, big), axis=1, keepdims=True)
    g_val = 1.0 / jnp.sum(jnp.where(gmask, jnp.exp(gl - gm), 0.0), axis=1, keepdims=True)

    base = 8.0 + 8.0 * gidx
    emask = (lane_f >= base) & (lane_f < base + EPG)
    el = jnp.where(emask, lg, NEG)
    m1 = jnp.max(el, axis=1, keepdims=True)
    i1 = jnp.min(jnp.where(el == m1, lane_f, big), axis=1, keepdims=True)
    el2 = jnp.where(lane_f == i1, NEG, el)
    m2 = jnp.max(el2, axis=1, keepdims=True)
    i2 = jnp.min(jnp.where(el2 == m2, lane_f, big), axis=1, keepdims=True)
    r = jnp.exp(m2 - m1)
    w1 = g_val / (1.0 + r)
    w2 = w1 * r
    gates = jnp.where(lane_f == i1 - base, w1, jnp.where(lane_f == i2 - base, w2, 0.0))

    onehot = jnp.where(lane_f == gidx, 1.0, 0.0)
    before = jnp.dot(tri_ref[...], onehot.astype(BF16), preferred_element_type=F32)
    carry = carry_scr[0:1, :]
    rank = jnp.sum(jnp.where(lane_f == gidx, before + carry, 0.0), axis=1, keepdims=True)
    carry_scr[0:1, :] = carry + jnp.sum(onehot, axis=0, keepdims=True)

    return jnp.where(lane < EPG, gates, jnp.where(lane == 8, gidx, jnp.where(lane == 9, rank, 0.0)))


def _slot_kernel(rt_ref, st_ref, o_ref):
    g = rt_ref[0:1, :]
    slot = rt_ref[1:2, :]
    st = st_ref[...]
    for k in range(N_GROUPS):
        slot = slot + jnp.where(g == k, st[:, k:k + 1], 0.0)
    o_ref[...] = slot.astype(jnp.int32)


def _slots(route_t, starts_row):
    n, _, st = route_t.shape
    return pl.pallas_call(
        _slot_kernel,
        grid=(n,),
        in_specs=[pl.BlockSpec((None, 8, st), lambda i: (i, 0, 0)),
                  pl.BlockSpec((1, 128), lambda i: (0, 0))],
        out_specs=pl.BlockSpec((None, 1, st), lambda i: (i, 0, 0)),
        out_shape=jax.ShapeDtypeStruct((n, 1, st), jnp.int32),
        compiler_params=_cp(("arbitrary",)),
        name="moe_slots",
    )(route_t, starts_row)


def _slot_spec(slot3, tile, ahead=0):
    n, _, st = slot3.shape
    per = st // tile
    last = n * per - 1

    def idx(i, *_):
        t = jnp.minimum(i + ahead, last)
        return (t // per, 0, t % per)

    return pl.BlockSpec((None, 1, tile), idx, memory_space=pltpu.SMEM)


def _dispatch_kernel(slot_ref, pad_ref, h2x_ref, xs_ref, zx_scr, sem):
    i = pl.program_id(0)
    n_pad = pad_ref.shape[1]

    @pl.when(i == 0)
    def _():
        zx_scr[...] = jnp.zeros_like(zx_scr)

        def zbody(r, carry):
            pltpu.make_async_copy(zx_scr.at[pl.ds(0, 1)], xs_ref.at[pl.ds(pad_ref[0, r], 1)], sem.at[0]).start()
            return carry

        lax.fori_loop(0, n_pad, zbody, 0)
        pltpu.make_async_copy(xs_ref.at[pl.ds(0, n_pad)], xs_ref.at[pl.ds(0, n_pad)], sem.at[0]).wait()

    def body(j, carry):
        for u in range(2):
            r = 2 * j + u
            pltpu.make_async_copy(h2x_ref.at[pl.ds(r, 1)], xs_ref.at[pl.ds(slot_ref[0, r], 1)],
                                  sem.at[0]).start(priority=u)
        return carry

    lax.fori_loop(0, TD // 2, body, 0)
    pltpu.make_async_copy(h2x_ref, xs_ref.at[pl.ds(0, TD)], sem.at[0]).wait()


def _dispatch(slot3, pad_slots, h2x, n_sorted):
    rows = h2x.shape[0]
    n_pad = pad_slots.shape[0]
    return pl.pallas_call(
        _dispatch_kernel,
        grid=(rows // TD,),
        in_specs=[_slot_spec(slot3, TD),
                  pl.BlockSpec((1, n_pad), lambda i: (0, 0), memory_space=pltpu.SMEM),
                  pl.BlockSpec((TD, DX), lambda i: (i, 0))],
        out_specs=pl.BlockSpec(memory_space=pl.ANY),
        out_shape=jax.ShapeDtypeStruct((n_sorted, DX), F32),
        scratch_shapes=[pltpu.VMEM((8, DX), F32), pltpu.SemaphoreType.DMA((1,))],
        compiler_params=_cp(("arbitrary",)),
        name="moe_dispatch",
    )(slot3, pad_slots.reshape(1, n_pad), h2x)


def _moe_kernel(tg_ref, na_ref, x_ref, w1_ref, w3_ref, w2_ref, o_ref, hid_scr, w1b_scr, w3b_scr, w2b_scr):
    i = pl.program_id(0)

    @pl.when((i == 0) | (tg_ref[i] != tg_ref[jnp.maximum(i - 1, 0)]))
    def _():
        for e in range(EPG):
            cols = slice(e * EH, (e + 1) * EH)
            w1b_scr[:, cols] = w1_ref[:, cols].astype(BF16)
            w3b_scr[:, cols] = w3_ref[:, cols].astype(BF16)
            w2b_scr[cols, :] = w2_ref[cols, :].astype(BF16)

    @pl.when(i < na_ref[0])
    def _():
        x = x_ref[:, :D].astype(BF16)
        gt = x_ref[:, D:]
        for e in range(EPG):
            cols = slice(e * EH, (e + 1) * EH)
            h1 = jnp.dot(x, w1b_scr[:, cols], preferred_element_type=F32)
            h3 = jnp.dot(x, w3b_scr[:, cols], preferred_element_type=F32)
            hid_scr[:, cols] = (_silu(h1) * h3 * gt[:, e:e + 1]).astype(BF16)
        o_ref[...] = jnp.dot(hid_scr[...], w2b_scr[...], preferred_element_type=F32)

    @pl.when(i >= na_ref[0])
    def _():
        o_ref[...] = jnp.zeros_like(o_ref)


def _moe(tile_group, n_active, xs, w1, w3, w2, layer, n_tiles):
    x_idx = lambda i, tg, na: (jnp.minimum(i, na[0] - 1), 0)
    out_idx = lambda i, tg, na: (i, 0)
    w_idx = lambda i, tg, na: (layer, tg[i], 0, 0)
    once = pl.Buffered(1)
    grid_spec = pltpu.PrefetchScalarGridSpec(
        num_scalar_prefetch=2,
        grid=(n_tiles,),
        in_specs=[pl.BlockSpec((TS, DX), x_idx),
                  pl.BlockSpec((None, None, D, GH), w_idx, pipeline_mode=once),
                  pl.BlockSpec((None, None, D, GH), w_idx, pipeline_mode=once),
                  pl.BlockSpec((None, None, GH, D), w_idx, pipeline_mode=once)],
        out_specs=pl.BlockSpec((TS, D), out_idx),
        scratch_shapes=[pltpu.VMEM((TS, GH), BF16), pltpu.VMEM((D, GH), BF16), pltpu.VMEM((D, GH), BF16),
                        pltpu.VMEM((GH, D), BF16)],
    )
    return pl.pallas_call(
        _moe_kernel,
        grid_spec=grid_spec,
        out_shape=jax.ShapeDtypeStruct((xs.shape[0], D), F32),
        compiler_params=_cp(("arbitrary",)),
        name="moe_experts",
    )(tile_group, n_active, xs, w1, w3, w2)


def _combine_kernel(*refs, final, n_steps):
    if final:
        slot_ref, slot_next_ref, x1_ref, g2_ref, fg_ref, ys_ref, o_ref, buf, sem = refs
    else:
        slot_ref, slot_next_ref, x1_ref, g2_ref, ys_ref, o_ref, buf, sem = refs
    i = pl.program_id(0)

    def issue(sref, b):
        def body(j, carry):
            for u in range(2):
                r = 2 * j + u
                pltpu.make_async_copy(ys_ref.at[pl.ds(sref[0, r], 1)], buf.at[b, pl.ds(r, 1)],
                                      sem.at[b]).start(priority=u)
            return carry

        lax.fori_loop(0, TD // 2, body, 0)

    @pl.when(i == 0)
    def _():
        issue(slot_ref, 0)

    @pl.when(i + 1 < n_steps)
    def _():
        issue(slot_next_ref, (i + 1) % 2)

    b = i % 2
    pltpu.make_async_copy(ys_ref.at[pl.ds(0, TD)], buf.at[b], sem.at[b]).wait()

    x2 = x1_ref[...] + g2_ref[...] * buf[b]
    if final:
        ms = jnp.mean(x2 * x2, axis=-1, keepdims=True)
        x2 = x2 * lax.rsqrt(ms + EPS) * fg_ref[...]
    o_ref[...] = x2


def _combine(slot3, x1, g2, ys, n_mod, final_g=None):
    rows = x1.shape[0]
    n_steps = rows // TD
    final = final_g is not None
    mod_idx = lambda i: (jnp.minimum((i * TD) // L, n_mod), 0, 0)
    in_specs = [_slot_spec(slot3, TD), _slot_spec(slot3, TD, ahead=1),
                pl.BlockSpec((TD, D), lambda i: (i, 0)),
                pl.BlockSpec((None, 1, D), mod_idx)]
    args = [slot3, slot3, x1, g2]
    if final:
        in_specs.append(pl.BlockSpec((1, D), lambda i: (0, 0)))
        args.append(final_g.reshape(1, D))
    in_specs.append(pl.BlockSpec(memory_space=pl.ANY))
    args.append(ys)
    return pl.pallas_call(
        functools.partial(_combine_kernel, final=final, n_steps=n_steps),
        grid=(n_steps,),
        in_specs=in_specs,
        out_specs=pl.BlockSpec((TD, D), lambda i: (i, 0)),
        out_shape=jax.ShapeDtypeStruct((rows, D), F32),
        scratch_shapes=[pltpu.VMEM((2, TD, D), F32), pltpu.SemaphoreType.DMA((2,))],
        compiler_params=_cp(("arbitrary",)),
        name="moe_combine_final" if final else "moe_combine",
    )(*args)


def _sort_plan(counts_row, rows):
    counts = counts_row[:N_GROUPS].astype(jnp.int32)
    padded = ((counts + TS - 1) // TS) * TS
    ends = jnp.cumsum(padded)
    starts = ends - padded
    starts_row = jnp.zeros((1, 128), F32).at[0, :N_GROUPS].set(starts.astype(F32))
    n_tiles = rows // TS + N_GROUPS
    n_active = (ends[-1] // TS).reshape(1)
    tile_start = jnp.arange(n_tiles, dtype=jnp.int32) * TS
    tile_group = jnp.minimum(jnp.sum(tile_start[:, None] >= ends[None, :], axis=1), N_GROUPS - 1).astype(jnp.int32)
    r = jnp.arange(TS, dtype=jnp.int32)
    pad_real = ((starts + counts)[:, None] + r[None, :]).reshape(-1)
    is_pad = pad_real < jnp.repeat(ends, TS)
    tail = ends[-1] + jnp.cumsum(jnp.logical_not(is_pad).astype(jnp.int32)) - 1
    pad_slots = jnp.where(is_pad, pad_real, tail)
    return starts_row, pad_slots, tile_group, n_active, n_tiles, n_tiles * TS


def _moe_sorted(h2x, route_t, cnt, w1, w3, w2, layer):
    rows = h2x.shape[0]
    starts_row, pad_slots, tile_group, n_active, n_tiles, n_sorted = _sort_plan(cnt[0], rows)
    slot3 = _slots(route_t, starts_row)
    xs = _dispatch(slot3, pad_slots, h2x, n_sorted)
    ys = _moe(tile_group, n_active, xs, w1, w3, w2, layer, n_tiles)
    return slot3, ys


def _router_weights(gw, gb, ew, eb):
    wr = jnp.zeros((D, 128), F32).at[:, :N_GROUPS].set(gw).at[:, 8:8 + N_GROUPS * EPG].set(ew)
    br = jnp.zeros((1, 128), F32).at[0, :N_GROUPS].set(gb).at[0, 8:8 + N_GROUPS * EPG].set(eb)
    hi = wr.astype(BF16)
    lo = (wr - hi.astype(F32)).astype(BF16)
    return jnp.concatenate([hi, lo], axis=1), br


def kernel(x, c, ctx, c_ctx, ada_w, ada_b, norm1_g, norm2_g, w_in_e, conv_w, na_rpb, w_out_e, w_in_o,
           ret_decay_f, ret_decay_b, w_out_o, router_gw, router_gb, router_ew, router_eb,
           moe_w1, moe_w3, moe_w2, final_g):
    nb = x.shape[0]
    assert x.shape[1:] == (L, D) and ctx.shape[1:] == (CTX, D) and nb < 16 and ada_w.shape[0] == 2
    t_lat = nb * L
    t_ctx = nb * CTX
    t_all = t_lat + t_ctx
    tm = 1024 if t_ctx % 1024 == 0 else 512

    xsrc = (x.reshape(t_lat, D), ctx.reshape(t_ctx, D))
    c16 = jnp.zeros((16, D), F32).at[:nb].set(c).at[nb].set(c_ctx)
    mod = _ada(c16, ada_w, ada_b).reshape(2, 16, 6, 1, D)
    mods = [[mod[l, :, k] for k in range(6)] for l in range(2)]

    sh1, sc1, g1, sh2, sc2, g2 = mods[0]
    qkv = _inproj(xsrc, norm1_g[0], sh1, sc1, w_in_e[0].astype(BF16), t_lat, nb, tm)
    y_conv = (_conv(qkv, conv_w[0], L, nb, 0), _conv(qkv, conv_w[0], CTX, nb, t_lat // CTX))
    y_attn = (_natten(qkv, _natten_tcol(na_rpb[0]), nb), _ctx_attn(qkv, nb))
    w_out = w_out_e[0].astype(BF16)
    wr2, br = _router_weights(router_gw[0], router_gb[0], router_ew[0], router_eb[0])
    x1, h2x, route_t, cnt = _outproj([y_conv, y_attn], [w_out[:CONV_CH], w_out[CONV_CH:]], xsrc, g1, sh2, sc2,
                                     norm2_g[0], wr2, br, t_all, t_lat, nb, tm)
    slot3, ys = _moe_sorted(h2x, route_t, cnt, moe_w1, moe_w3, moe_w2, 0)

    sh1, sc1, g1n, sh2n, sc2n, g2n = mods[1]
    qkvg, xa = _inproj_comb(slot3, ys, x1, g2, norm1_g[1], sh1, sc1, w_in_o[0].astype(BF16), t_lat, nb, tm,
                            _rope_tables(tm))
    g1, sh2, sc2, g2 = g1n, sh2n, sc2n, g2n
    lg = jnp.stack([jnp.log1p(-jnp.exp(ret_decay_f[0].astype(F32))),
                    jnp.log1p(-jnp.exp(ret_decay_b[0].astype(F32)))])
    y_ret = _retention(qkvg, lg, nb)
    wr2, br = _router_weights(router_gw[1], router_gb[1], router_ew[1], router_eb[1])
    x1, h2x, route_t, cnt = _outproj([(y_ret,)], [w_out_o[0].astype(BF16)], (xa,), g1, sh2, sc2, norm2_g[1],
                                     wr2, br, t_lat, t_lat, nb, tm)
    slot3, ys = _moe_sorted(h2x, route_t, cnt, moe_w1, moe_w3, moe_w2, 1)
    out = _combine(slot3, x1, g2, ys, nb, final_g)
    return out.reshape(nb, L, D)
```

```python
import functools
import math

import numpy as np
import jax
import jax.numpy as jnp
from jax import lax
from jax.experimental import pallas as pl
from jax.experimental.pallas import tpu as pltpu

F32 = jnp.float32
BF16 = jnp.bfloat16
HIGHEST = lax.Precision.HIGHEST

D = 1024
L = 4096
GRID_W = 64
GRID_H = L // GRID_W
CTX = 256
EPS = 1e-6

CONV_CH = 512
NA_HEADS = 8
NA_DH = 64
WIN_H = 8
WIN_W = 16
EVEN_IN = 3072

RET_HEADS = 4
RET_DK = 256
RET_DV = 512
RET_C = 256
ODD_IN = 6144
ROPE_BASE = 10000.0

N_GROUPS = 4
EPG = 8
EH = 256
GH = EPG * EH

NEG = -1e30

NA_QR = 4
NA_KR = NA_QR + WIN_H - 1
NA_NQ = NA_QR * GRID_W
NA_NK = NA_KR * GRID_W
NA_WS_MAX = GRID_H - NA_KR

DX = D + 128
TS = 512
TD = 512

VMEM_LIMIT = 56 * 1024 * 1024


def _cp(sem, vmem=VMEM_LIMIT):
    return pltpu.CompilerParams(dimension_semantics=sem, vmem_limit_bytes=vmem)


def _silu(v):
    return v * (1.0 / (1.0 + jnp.exp(-v)))


def _src_specs(src, tm, lat_tiles):
    tail = tuple(src[0].shape[1:])
    zeros = (0,) * len(tail)
    if len(src) == 1:
        return [pl.BlockSpec((tm,) + tail, lambda i, *_: (i,) + zeros)]
    return [pl.BlockSpec((tm,) + tail, lambda i, *_: (jnp.minimum(i, lat_tiles - 1),) + zeros),
            pl.BlockSpec((tm,) + tail, lambda i, *_: (jnp.maximum(i - lat_tiles, 0),) + zeros)]


def _src_load(refs, i, lat_tiles):
    if len(refs) == 1:
        return refs[0][...]
    return jnp.where(i < lat_tiles, refs[0][...], refs[1][...])


def _ada_kernel(c_ref, w_ref, b_ref, o_ref):
    s = _silu(c_ref[...])
    o_ref[...] = jnp.dot(s, w_ref[...], precision=HIGHEST, preferred_element_type=F32) + b_ref[...]


def _ada(c16, ada_w, ada_b):
    depth = ada_w.shape[0]
    tn = 1536
    return pl.pallas_call(
        _ada_kernel,
        grid=(depth, 6 * D // tn),
        in_specs=[
            pl.BlockSpec((16, D), lambda l, j: (0, 0)),
            pl.BlockSpec((None, D, tn), lambda l, j: (l, 0, j)),
            pl.BlockSpec((None, 1, tn), lambda l, j: (l, 0, j)),
        ],
        out_specs=pl.BlockSpec((None, 16, tn), lambda l, j: (l, 0, j)),
        out_shape=jax.ShapeDtypeStruct((depth, 16, 6 * D), F32),
        compiler_params=_cp(("arbitrary", "arbitrary")),
        name="ada_mod",
    )(c16, ada_w, ada_b.reshape(depth, 1, 6 * D))


def _inproj_kernel(*refs, rope, tn, n_src, lat_tiles):
    x_refs, refs = refs[:n_src], refs[n_src:]
    if rope:
        g_ref, sh_ref, sc_ref, w_ref, cos_ref, sin_ref, o_ref, h_scr = refs
    else:
        g_ref, sh_ref, sc_ref, w_ref, o_ref, h_scr = refs
    i = pl.program_id(0)
    j = pl.program_id(1)

    @pl.when(j == 0)
    def _():
        x = _src_load(x_refs, i, lat_tiles)
        ms = jnp.mean(x * x, axis=-1, keepdims=True)
        y = x * lax.rsqrt(ms + EPS) * g_ref[...]
        h_scr[...] = (y * (1.0 + sc_ref[...]) + sh_ref[...]).astype(BF16)

    acc = jnp.dot(h_scr[...], w_ref[...], preferred_element_type=F32)
    if not rope:
        o_ref[...] = acc.astype(o_ref.dtype)
        return

    @pl.when(j < 2)
    def _():
        cos = cos_ref[...]
        sin = sin_ref[...]
        scale = jnp.where(j == 1, RET_DK ** -0.5, 1.0).astype(F32)
        pieces = []
        for cblk in range(tn // 128):
            t = acc[:, cblk * 128:(cblk + 1) * 128]
            half = (cblk % 2) * 128
            r = pltpu.roll(t, 64, axis=1)
            pieces.append(t * cos[:, half:half + 128] + r * sin[:, half:half + 128])
        o_ref[...] = (jnp.concatenate(pieces, axis=1) * scale).astype(o_ref.dtype)

    @pl.when(j >= 2)
    def _():
        o_ref[...] = acc.astype(o_ref.dtype)


def _inproj(xsrc, gain, sh, sc, w, n_lat_rows, n_mod, tm, rope_tables=None):
    rows = sum(a.shape[0] for a in xsrc)
    n = w.shape[1]
    tn = 1024 if rope_tables is not None else n
    lat_tiles = n_lat_rows // tm
    mod_idx = lambda i, j: (jnp.minimum((i * tm) // L, n_mod), 0, 0)
    in_specs = _src_specs(xsrc, tm, lat_tiles) + [
        pl.BlockSpec((1, D), lambda i, j: (0, 0)),
        pl.BlockSpec((None, 1, D), mod_idx),
        pl.BlockSpec((None, 1, D), mod_idx),
        pl.BlockSpec((D, tn), lambda i, j: (0, j)),
    ]
    args = list(xsrc) + [gain.reshape(1, D), sh, sc, w]
    rope = rope_tables is not None
    if rope:
        per_seq = L // tm
        tab_idx = lambda i, j: (jnp.where(i < lat_tiles, i % per_seq, per_seq), 0)
        in_specs += [pl.BlockSpec((tm, RET_DK), tab_idx), pl.BlockSpec((tm, RET_DK), tab_idx)]
        args += list(rope_tables)
    return pl.pallas_call(
        functools.partial(_inproj_kernel, rope=rope, tn=tn, n_src=len(xsrc), lat_tiles=lat_tiles),
        grid=(rows // tm, n // tn),
        in_specs=in_specs,
        out_specs=pl.BlockSpec((tm, tn), lambda i, j: (i, j)),
        out_shape=jax.ShapeDtypeStruct((rows, n), BF16),
        scratch_shapes=[pltpu.VMEM((tm, D), BF16)],
        compiler_params=_cp(("arbitrary", "arbitrary")),
        name="inproj_rope" if rope else "inproj",
    )(*args)


def _inproj_comb_kernel(slot_ref, slotn_ref, x1_ref, g2_ref, g_ref, sh_ref, sc_ref, w_ref, cos_ref, sin_ref, ys_ref,
                        o_ref, xa_ref, h_scr, buf, sem, *, tn, tm, n_tiles, n_j, ch):
    i = pl.program_id(0)
    j = pl.program_id(1)
    rows_buf = ch * n_j

    def start_row(sref, b, r, u):
        src = sref[0, jnp.minimum(r, tm - 1)]
        pltpu.make_async_copy(ys_ref.at[pl.ds(src, 1)], buf.at[b, pl.ds(r, 1)], sem.at[b]).start(priority=u)

    @pl.when(j == 0)
    def _():
        @pl.when(i == 0)
        def _():
            def body(jj, carry):
                for u in range(2):
                    start_row(slot_ref, 0, 2 * jj + u, u)
                return carry

            lax.fori_loop(0, rows_buf // 2, body, 0)

        b = i % 2
        pltpu.make_async_copy(ys_ref.at[pl.ds(0, rows_buf)], buf.at[b], sem.at[b]).wait()
        x = x1_ref[...] + g2_ref[...] * buf[b, :tm, :]
        xa_ref[...] = x
        ms = jnp.mean(x * x, axis=-1, keepdims=True)
        coef = g_ref[...] * (1.0 + sc_ref[...])
        h_scr[...] = (x * lax.rsqrt(ms + EPS) * coef + sh_ref[...]).astype(BF16)

    bn = (i + 1) % 2
    for u in range(ch):
        start_row(slotn_ref, bn, j * ch + u, u % 2)

    acc = jnp.dot(h_scr[...], w_ref[...], preferred_element_type=F32)

    qk_tiles = 2 * RET_HEADS * RET_DK // tn

    @pl.when(j < qk_tiles)
    def _():
        cos = cos_ref[...]
        sin = sin_ref[...]
        pieces = []
        for cblk in range(tn // 128):
            t = acc[:, cblk * 128:(cblk + 1) * 128]
            half = (cblk % 2) * 128
            r = pltpu.roll(t, 64, axis=1)
            is_k = j * tn + cblk * 128 >= RET_HEADS * RET_DK
            scale = jnp.where(is_k, RET_DK ** -0.5, 1.0).astype(F32)
            pieces.append((t * cos[:, half:half + 128] + r * sin[:, half:half + 128]) * scale)
        o_ref[...] = jnp.concatenate(pieces, axis=1).astype(o_ref.dtype)

    @pl.when(j >= qk_tiles)
    def _():
        o_ref[...] = acc.astype(o_ref.dtype)

    @pl.when((i == n_tiles - 1) & (j == n_j - 1))
    def _():
        pltpu.make_async_copy(ys_ref.at[pl.ds(0, rows_buf)], buf.at[bn], sem.at[bn]).wait()


def _inproj_comb(slot3, ys, x1, g2, gain, sh, sc, w, n_lat_rows, n_mod, tm, rope_tables):
    rows = x1.shape[0]
    n = w.shape[1]
    tn = 2048
    n_tiles = rows // tm
    lat_tiles = n_lat_rows // tm
    per_seq = L // tm
    mod_idx = lambda i, j: (jnp.minimum((i * tm) // L, n_mod), 0, 0)
    tab_idx = lambda i, j: (jnp.where(i < lat_tiles, i % per_seq, per_seq), 0)
    n_j = n // tn
    ch = -(-tm // n_j)
    while (ch * n_j) % 8:
        ch += 1
    in_specs = [_slot_spec(slot3, tm), _slot_spec(slot3, tm, ahead=1),
                pl.BlockSpec((tm, D), lambda i, j: (i, 0)),
                pl.BlockSpec((None, 1, D), mod_idx),
                pl.BlockSpec((1, D), lambda i, j: (0, 0)),
                pl.BlockSpec((None, 1, D), mod_idx),
                pl.BlockSpec((None, 1, D), mod_idx),
                pl.BlockSpec((D, tn), lambda i, j: (0, j)),
                pl.BlockSpec((tm, RET_DK), tab_idx),
                pl.BlockSpec((tm, RET_DK), tab_idx),
                pl.BlockSpec(memory_space=pl.ANY)]
    return pl.pallas_call(
        functools.partial(_inproj_comb_kernel, tn=tn, tm=tm, n_tiles=n_tiles, n_j=n_j, ch=ch),
        grid=(n_tiles, n_j),
        in_specs=in_specs,
        out_specs=[pl.BlockSpec((tm, tn), lambda i, j: (i, j)),
                   pl.BlockSpec((tm, D), lambda i, j: (i, 0))],
        out_shape=[jax.ShapeDtypeStruct((rows, n), BF16),
                   jax.ShapeDtypeStruct((rows, D), F32)],
        scratch_shapes=[pltpu.VMEM((tm, D), BF16), pltpu.VMEM((2, ch * n_j, D), F32),
                        pltpu.SemaphoreType.DMA((2,))],
        compiler_params=_cp(("arbitrary", "arbitrary")),
        name="inproj_rope_combine",
    )(slot3, slot3, x1, g2, gain.reshape(1, D), sh, sc, w, rope_tables[0], rope_tables[1], ys)


def _rope_tables(tm):
    pos = np.arange(L)
    half = RET_DK // 4
    freqs = ROPE_BASE ** (-np.arange(half, dtype=np.float64) / half)
    ang_r = (pos // GRID_W)[:, None] * freqs[None, :]
    ang_c = (pos % GRID_W)[:, None] * freqs[None, :]
    cos = np.concatenate([np.cos(ang_r), np.cos(ang_r), np.cos(ang_c), np.cos(ang_c)], axis=1)
    sin = np.concatenate([-np.sin(ang_r), np.sin(ang_r), -np.sin(ang_c), np.sin(ang_c)], axis=1)
    cos = np.concatenate([cos, np.ones((tm, RET_DK))], axis=0)
    sin = np.concatenate([sin, np.zeros((tm, RET_DK))], axis=0)
    return jnp.asarray(cos, F32), jnp.asarray(sin, F32)


def _conv_kernel(ab_ref, ac_ref, ax_ref, w_ref, o_ref, *, seq):
    u = ac_ref[...].astype(F32) * ax_ref[...].astype(F32)
    row = lax.broadcasted_iota(jnp.int32, u.shape, 0)
    up = jnp.where(row == 0, 0.0, pltpu.roll(u, 1, axis=0))
    un = jnp.where(row == seq - 1, 0.0, pltpu.roll(u, seq - 1, axis=0))
    w = w_ref[...]
    y = ab_ref[...].astype(F32) * (up * w[0:1, :] + u * w[1:2, :] + un * w[2:3, :])
    o_ref[...] = y.astype(o_ref.dtype)


def _conv(qkv, conv_w, seq, n_seq, row_blk0):
    nb = CONV_CH // 128
    return pl.pallas_call(
        functools.partial(_conv_kernel, seq=seq),
        grid=(n_seq, nb),
        in_specs=[
            pl.BlockSpec((seq, 128), lambda b, j: (row_blk0 + b, j)),
            pl.BlockSpec((seq, 128), lambda b, j: (row_blk0 + b, nb + j)),
            pl.BlockSpec((seq, 128), lambda b, j: (row_blk0 + b, 2 * nb + j)),
            pl.BlockSpec((3, 128), lambda b, j: (0, j)),
        ],
        out_specs=pl.BlockSpec((seq, 128), lambda b, j: (b, j)),
        out_shape=jax.ShapeDtypeStruct((n_seq * seq, CONV_CH), BF16),
        compiler_params=_cp(("arbitrary", "arbitrary")),
        name="short_conv",
    )(qkv, qkv, qkv, conv_w)


def _natten_tcol(rpb):
    qc = np.arange(GRID_W)
    kc = np.arange(GRID_W)
    cs = np.clip(qc - WIN_W // 2, 0, GRID_W - WIN_W)
    col_ok = (kc[None, :] >= cs[:, None]) & (kc[None, :] < cs[:, None] + WIN_W)
    dx = np.clip(kc[None, :] - qc[:, None] + WIN_W - 1, 0, 2 * WIN_W - 2)
    tcol = jnp.where(col_ok[None, None], rpb.astype(F32)[:, :, dx], NEG)
    return tcol.reshape(NA_HEADS // 2, 2, 2 * WIN_H - 1, GRID_W, GRID_W)


def _natten_row_classes():
    table = []
    for r0 in (0, 2 * NA_QR, GRID_H - NA_QR):
        ws = int(np.clip(r0 - WIN_H // 2, 0, NA_WS_MAX))
        rows = []
        for q in range(NA_QR):
            qr = r0 + q
            rs = int(np.clip(qr - WIN_H // 2, 0, GRID_H - WIN_H))
            rows.append([(ws + k) - qr + WIN_H - 1 if rs <= ws + k < rs + WIN_H else None for k in range(NA_KR)])
        table.append(rows)
    return table


def _attend(qa, kw, vw, kc, vc, bias):
    dn = (((1,), (1,)), ((), ()))
    s2 = lax.dot_general(qa, kc, dn, preferred_element_type=F32)
    m = jnp.max(s2, axis=1, keepdims=True)
    if kw is not None:
        s1 = lax.dot_general(qa, kw, dn, preferred_element_type=F32) + bias
        m = jnp.maximum(m, jnp.max(s1, axis=1, keepdims=True))
        p1 = jnp.exp(s1 - m)
    p2 = jnp.exp(s2 - m)
    den = jnp.sum(p2, axis=1, keepdims=True)
    o = jnp.dot(p2.astype(BF16), vc, preferred_element_type=F32)
    if kw is not None:
        den = den + jnp.sum(p1, axis=1, keepdims=True)
        o = o + jnp.dot(p1.astype(BF16), vw, preferred_element_type=F32)
    return o / den


NA_SUB = 4


def _natten_kernel(q_ref, k_ref, v_ref, kc_ref, vc_ref, tcol_ref, o_ref, bias_ref):
    n_rb = GRID_H // NA_QR

    @pl.when((pl.program_id(1) == 0) & (pl.program_id(2) == 0))
    def _():
        neg = jnp.full((GRID_W, GRID_W), NEG, F32)
        for cls, rows in enumerate(_natten_row_classes()):
            for a in range(2):
                for qr, dys in enumerate(rows):
                    blocks = [neg if dy is None else tcol_ref[a, dy] for dy in dys]
                    bias_ref[cls, a, qr * GRID_W:(qr + 1) * GRID_W, :] = jnp.concatenate(blocks, axis=1)

    lane = lax.broadcasted_iota(jnp.int32, (NA_NQ, 128), 1)
    for sb in range(NA_SUB):
        rb = pl.program_id(2) * NA_SUB + sb
        ws = jnp.clip(rb * NA_QR - WIN_H // 2, 0, NA_WS_MAX)
        start = pl.multiple_of(ws * GRID_W, GRID_W)
        kw = k_ref[pl.ds(start, NA_NK), :]
        vw = v_ref[pl.ds(start, NA_NK), :]
        cls = jnp.where(rb == 0, 0, jnp.where(rb == n_rb - 1, 2, 1))
        q = q_ref[sb * NA_NQ:(sb + 1) * NA_NQ, :] * (NA_DH ** -0.5)
        zero = jnp.zeros_like(q)
        o0 = _attend(jnp.where(lane < NA_DH, q, zero), kw, vw, kc_ref[...], vc_ref[...], bias_ref[cls, 0])
        o1 = _attend(jnp.where(lane >= NA_DH, q, zero), kw, vw, kc_ref[...], vc_ref[...], bias_ref[cls, 1])
        o_ref[sb * NA_NQ:(sb + 1) * NA_NQ, :] = jnp.where(lane < NA_DH, o0, o1).astype(o_ref.dtype)


def _natten(qkv, tcol, nb):
    n_steps = GRID_H // NA_QR // NA_SUB
    qcol, kcol, vcol = 3 * CONV_CH // 128, 3 * CONV_CH // 128 + 4, 3 * CONV_CH // 128 + 8
    ctx_blk0 = nb * L // CTX
    return pl.pallas_call(
        _natten_kernel,
        grid=(NA_HEADS // 2, nb, n_steps),
        in_specs=[
            pl.BlockSpec((NA_SUB * NA_NQ, 128), lambda h, b, r: (b * n_steps + r, qcol + h)),
            pl.BlockSpec((L, 128), lambda h, b, r: (b, kcol + h)),
            pl.BlockSpec((L, 128), lambda h, b, r: (b, vcol + h)),
            pl.BlockSpec((CTX, 128), lambda h, b, r: (ctx_blk0 + b, kcol + h)),
            pl.BlockSpec((CTX, 128), lambda h, b, r: (ctx_blk0 + b, vcol + h)),
            pl.BlockSpec((None, 2, 2 * WIN_H - 1, GRID_W, GRID_W), lambda h, b, r: (h, 0, 0, 0, 0)),
        ],
        out_specs=pl.BlockSpec((NA_SUB * NA_NQ, 128), lambda h, b, r: (b * n_steps + r, h)),
        out_shape=jax.ShapeDtypeStruct((nb * L, NA_HEADS * NA_DH), BF16),
        scratch_shapes=[pltpu.VMEM((3, 2, NA_NQ, NA_NK), F32)],
        compiler_params=_cp(("arbitrary", "arbitrary", "arbitrary")),
        name="natten",
    )(qkv, qkv, qkv, qkv, qkv, tcol)


def _ctx_attn_kernel(q_ref, kc_ref, vc_ref, o_ref):
    q = q_ref[...] * (NA_DH ** -0.5)
    lane = lax.broadcasted_iota(jnp.int32, q.shape, 1)
    zero = jnp.zeros_like(q)
    o0 = _attend(jnp.where(lane < NA_DH, q, zero), None, None, kc_ref[...], vc_ref[...], None)
    o1 = _attend(jnp.where(lane >= NA_DH, q, zero), None, None, kc_ref[...], vc_ref[...], None)
    o_ref[...] = jnp.where(lane < NA_DH, o0, o1).astype(o_ref.dtype)


def _ctx_attn(qkv, nb):
    qcol, kcol, vcol = 3 * CONV_CH // 128, 3 * CONV_CH // 128 + 4, 3 * CONV_CH // 128 + 8
    ctx_blk0 = nb * L // CTX
    return pl.pallas_call(
        _ctx_attn_kernel,
        grid=(nb, NA_HEADS // 2),
        in_specs=[
            pl.BlockSpec((CTX, 128), lambda b, h: (ctx_blk0 + b, qcol + h)),
            pl.BlockSpec((CTX, 128), lambda b, h: (ctx_blk0 + b, kcol + h)),
            pl.BlockSpec((CTX, 128), lambda b, h: (ctx_blk0 + b, vcol + h)),
        ],
        out_specs=pl.BlockSpec((CTX, 128), lambda b, h: (b, h)),
        out_shape=jax.ShapeDtypeStruct((nb * CTX, NA_HEADS * NA_DH), BF16),
        compiler_params=_cp(("arbitrary", "arbitrary")),
        name="ctx_attn",
    )(qkv, qkv, qkv)


def _retention_kernel(lg_ref, q_ref, k_ref, v_ref, g_ref, kc_ref, vc_ref, o_ref, y_scr, stf_scr, stb_scr):
    h = pl.program_id(1)
    n_chunks = L // RET_C
    pos_r = lax.broadcasted_iota(jnp.int32, (RET_C, 1), 0).astype(F32)
    rel = pos_r - lax.broadcasted_iota(jnp.int32, (1, RET_C), 1).astype(F32)
    dn_t = (((1,), (1,)), ((), ()))
    dn_k = (((0,), (0,)), ((), ()))

    def decays(lg, fwd):
        if fwd:
            d_intra = jnp.where(rel >= 0, jnp.exp(lg * jnp.maximum(rel, 0.0)), 0.0)
            d_q = jnp.exp(lg * (pos_r + 1.0))
            d_k = jnp.exp(lg * (RET_C - 1.0 - pos_r))
        else:
            d_intra = jnp.where(rel <= 0, jnp.exp(lg * jnp.maximum(-rel, 0.0)), 0.0)
            d_q = jnp.exp(lg * (RET_C - pos_r))
            d_k = jnp.exp(lg * pos_r)
        d_c = jnp.exp(jnp.full((1, 1), RET_C, F32) * lg)
        return d_intra, d_q, d_k, d_c

    def ctx_state(st_scr, d_k, d_c, order):
        st_scr[...] = jnp.zeros_like(st_scr)
        for jc in order:
            kd = (kc_ref[jc * RET_C:(jc + 1) * RET_C, :].astype(F32) * d_k).astype(BF16)
            vc = vc_ref[jc * RET_C:(jc + 1) * RET_C, :]
            st_scr[...] = st_scr[...] * d_c + lax.dot_general(kd, vc, dn_k, preferred_element_type=F32)

    def chunk_out(st_scr, c, d_intra, d_q, d_k, d_c):
        r0 = pl.multiple_of(c * RET_C, RET_C)
        q = q_ref[pl.ds(r0, RET_C), :]
        k = k_ref[pl.ds(r0, RET_C), :]
        v = v_ref[pl.ds(r0, RET_C), :]
        s = lax.dot_general(q, k, dn_t, preferred_element_type=F32) * d_intra
        st = st_scr[...]
        o = (jnp.dot(s.astype(BF16), v, preferred_element_type=F32)
             + jnp.dot(q, st.astype(BF16), preferred_element_type=F32) * d_q)
        kd = (k.astype(F32) * d_k).astype(BF16)
        st_scr[...] = st * d_c + lax.dot_general(kd, v, dn_k, preferred_element_type=F32)
        return r0, o

    def finish(r0, y):
        y = y * lax.rsqrt(jnp.mean(y * y, axis=-1, keepdims=True) + EPS)
        g = g_ref[pl.ds(r0, RET_C), :].astype(F32)
        o_ref[pl.ds(r0, RET_C), :] = (_silu(g) * y).astype(o_ref.dtype)

    dec_f = decays(lg_ref[0, h], True)
    dec_b = decays(lg_ref[1, h], False)
    ctx_state(stf_scr, dec_f[2], dec_f[3], range(CTX // RET_C))
    ctx_state(stb_scr, dec_b[2], dec_b[3], range(CTX // RET_C - 1, -1, -1))

    def first_half(i, carry):
        r0, o = chunk_out(stf_scr, i, *dec_f)
        y_scr[pl.ds(r0, RET_C), :] = o
        r1, o = chunk_out(stb_scr, n_chunks - 1 - i, *dec_b)
        y_scr[pl.ds(r1, RET_C), :] = o
        return carry

    lax.fori_loop(0, n_chunks // 2, first_half, 0)

    def second_half(i, carry):
        r0, o = chunk_out(stf_scr, i, *dec_f)
        finish(r0, y_scr[pl.ds(r0, RET_C), :] + o)
        r1, o = chunk_out(stb_scr, n_chunks - 1 - i, *dec_b)
        finish(r1, y_scr[pl.ds(r1, RET_C), :] + o)
        return carry

    lax.fori_loop(n_chunks // 2, n_chunks, second_half, 0)


def _retention(qkvg, lg, nb):
    ctx_blk0 = nb * L // CTX
    kq = RET_HEADS
    vq = 2 * RET_HEADS * RET_DK // RET_DV
    gq = vq + RET_HEADS
    return pl.pallas_call(
        _retention_kernel,
        grid=(nb, RET_HEADS),
        in_specs=[
            pl.BlockSpec(memory_space=pltpu.SMEM),
            pl.BlockSpec((L, RET_DK), lambda b, h: (b, h)),
            pl.BlockSpec((L, RET_DK), lambda b, h: (b, kq + h)),
            pl.BlockSpec((L, RET_DV), lambda b, h: (b, vq + h)),
            pl.BlockSpec((L, RET_DV), lambda b, h: (b, gq + h)),
            pl.BlockSpec((CTX, RET_DK), lambda b, h: (ctx_blk0 + b, kq + h)),
            pl.BlockSpec((CTX, RET_DV), lambda b, h: (ctx_blk0 + b, vq + h)),
        ],
        out_specs=pl.BlockSpec((L, RET_DV), lambda b, h: (b, h)),
        out_shape=jax.ShapeDtypeStruct((nb * L, RET_HEADS * RET_DV), BF16),
        scratch_shapes=[pltpu.VMEM((L, RET_DV), F32), pltpu.VMEM((RET_DK, RET_DV), F32),
                        pltpu.VMEM((RET_DK, RET_DV), F32)],
        compiler_params=_cp(("arbitrary", "arbitrary")),
        name="retention",
    )(lg, qkvg, qkvg, qkvg, qkvg, qkvg, qkvg)


def _outproj_kernel(*refs, src_lens, lat_tiles):
    i = pl.program_id(0)
    srcs = []
    for n in src_lens:
        srcs.append(refs[:n])
        refs = refs[n:]
    n_act = len(src_lens) - 1
    w_refs, refs = refs[:n_act], refs[n_act:]
    (g1_ref, sh2_ref, sc2_ref, n2_ref, wr_ref, br_ref, tri_ref,
     x1_ref, h2x_ref, rt_ref, cnt_ref, carry_scr) = refs

    @pl.when(i == 0)
    def _():
        carry_scr[...] = jnp.zeros_like(carry_scr)

    y = None
    for a_src, w_ref in zip(srcs[:n_act], w_refs):
        t = jnp.dot(_src_load(a_src, i, lat_tiles), w_ref[...], preferred_element_type=F32)
        y = t if y is None else y + t
    x1 = _src_load(srcs[n_act], i, lat_tiles) + g1_ref[...] * y
    x1_ref[...] = x1
    ms = jnp.mean(x1 * x1, axis=-1, keepdims=True)
    coef = n2_ref[...] * (1.0 + sc2_ref[...])
    h2 = x1 * lax.rsqrt(ms + EPS) * coef + sh2_ref[...]
    h2x_ref[:, :D] = h2
    h_hi = h2.astype(BF16)
    h_lo = (h2 - h_hi.astype(F32)).astype(BF16)
    p = jnp.dot(h_hi, wr_ref[...], preferred_element_type=F32)
    q = jnp.dot(h_lo, wr_ref[:, :128], preferred_element_type=F32)
    route = _route(p[:, :128] + p[:, 128:] + q + br_ref[...], tri_ref, carry_scr)
    h2x_ref[:, D:] = route
    rt_ref[...] = route.T[8:16, :]
    cnt_ref[...] = carry_scr[...]


def _outproj(acts, ws, xsrc, g1, sh2, sc2, n2g, wr2, br, rows, n_lat_rows, n_mod, tm):
    lat_tiles = n_lat_rows // tm
    tri = jnp.asarray(np.tril(np.ones((tm, tm), np.float32), -1), BF16)
    mod_idx = lambda i: (jnp.minimum((i * tm) // L, n_mod), 0, 0)
    in_specs, args = [], []
    for src in list(acts) + [xsrc]:
        in_specs += _src_specs(src, tm, lat_tiles)
        args += list(src)
    in_specs += [pl.BlockSpec(w.shape, lambda i: (0, 0)) for w in ws]
    in_specs += [pl.BlockSpec((None, 1, D), mod_idx),
                 pl.BlockSpec((None, 1, D), mod_idx),
                 pl.BlockSpec((None, 1, D), mod_idx),
                 pl.BlockSpec((1, D), lambda i: (0, 0)),
                 pl.BlockSpec((D, 256), lambda i: (0, 0)),
                 pl.BlockSpec((1, 128), lambda i: (0, 0)),
                 pl.BlockSpec((tm, tm), lambda i: (0, 0))]
    args += list(ws) + [g1, sh2, sc2, n2g.reshape(1, D), wr2, br, tri]
    return pl.pallas_call(
        functools.partial(_outproj_kernel, src_lens=tuple(len(s) for s in list(acts) + [xsrc]),
                          lat_tiles=lat_tiles),
        grid=(rows // tm,),
        in_specs=in_specs,
        out_specs=[pl.BlockSpec((tm, D), lambda i: (i, 0)),
                   pl.BlockSpec((tm, DX), lambda i: (i, 0)),
                   pl.BlockSpec((None, 8, tm), lambda i: (i, 0, 0)),
                   pl.BlockSpec((8, 128), lambda i: (0, 0))],
        out_shape=[jax.ShapeDtypeStruct((rows, D), F32),
                   jax.ShapeDtypeStruct((rows, DX), F32),
                   jax.ShapeDtypeStruct((rows // tm, 8, tm), F32),
                   jax.ShapeDtypeStruct((8, 128), F32)],
        scratch_shapes=[pltpu.VMEM((8, 128), F32)],
        compiler_params=_cp(("arbitrary",)),
        name="outproj_router",
    )(*args)


def _route(lg, tri_ref, carry_scr):
    lane = lax.broadcasted_iota(jnp.int32, lg.shape, 1)
    lane_f = lane.astype(F32)
    big = jnp.float32(1e9)

    gmask = lane < N_GROUPS
    gl = jnp.where(gmask, lg, NEG)
    gm = jnp.max(gl, axis=1, keepdims=True)
    gidx = jnp.min(jnp.where(gl == gm, lane_f, big), axis=1, keepdims=True)
    g_val = 1.0 / jnp.sum(jnp.where(gmask, jnp.exp(gl - gm), 0.0), axis=1, keepdims=True)

    base = 8.0 + 8.0 * gidx
    emask = (lane_f >= base) & (lane_f < base + EPG)
    el = jnp.where(emask, lg, NEG)
    m1 = jnp.max(el, axis=1, keepdims=True)
    i1 = jnp.min(jnp.where(el == m1, lane_f, big), axis=1, keepdims=True)
    el2 = jnp.where(lane_f == i1, NEG, el)
    m2 = jnp.max(el2, axis=1, keepdims=True)
    i2 = jnp.min(jnp.where(el2 == m2, lane_f, big), axis=1, keepdims=True)
    r = jnp.exp(m2 - m1)
    w1 = g_val / (1.0 + r)
    w2 = w1 * r
    gates = jnp.where(lane_f == i1 - base, w1, jnp.where(lane_f == i2 - base, w2, 0.0))

    onehot = jnp.where(lane_f == gidx, 1.0, 0.0)
    before = jnp.dot(tri_ref[...], onehot.astype(BF16), preferred_element_type=F32)
    carry = carry_scr[0:1, :]
    rank = jnp.sum(jnp.where(lane_f == gidx, before + carry, 0.0), axis=1, keepdims=True)
    carry_scr[0:1, :] = carry + jnp.sum(onehot, axis=0, keepdims=True)

    return jnp.where(lane < EPG, gates, jnp.where(lane == 8, gidx, jnp.where(lane == 9, rank, 0.0)))


def _slot_kernel(rt_ref, st_ref, o_ref):
    g = rt_ref[0:1, :]
    slot = rt_ref[1:2, :]
    st = st_ref[...]
    for k in range(N_GROUPS):
        slot = slot + jnp.where(g == k, st[:, k:k + 1], 0.0)
    o_ref[...] = slot.astype(jnp.int32)


def _slots(route_t, starts_row):
    n, _, st = route_t.shape
    return pl.pallas_call(
        _slot_kernel,
        grid=(n,),
        in_specs=[pl.BlockSpec((None, 8, st), lambda i: (i, 0, 0)),
                  pl.BlockSpec((1, 128), lambda i: (0, 0))],
        out_specs=pl.BlockSpec((None, 1, st), lambda i: (i, 0, 0)),
        out_shape=jax.ShapeDtypeStruct((n, 1, st), jnp.int32),
        compiler_params=_cp(("arbitrary",)),
        name="moe_slots",
    )(route_t, starts_row)


def _slot_spec(slot3, tile, ahead=0):
    n, _, st = slot3.shape
    per = st // tile
    last = n * per - 1

    def idx(i, *_):
        t = jnp.minimum(i + ahead, last)
        return (t // per, 0, t % per)

    return pl.BlockSpec((None, 1, tile), idx, memory_space=pltpu.SMEM)


def _dispatch_kernel(slot_ref, pad_ref, h2x_ref, xs_ref, zx_scr, sem):
    i = pl.program_id(0)
    n_pad = pad_ref.shape[1]

    @pl.when(i == 0)
    def _():
        zx_scr[...] = jnp.zeros_like(zx_scr)

        def zbody(r, carry):
            pltpu.make_async_copy(zx_scr.at[pl.ds(0, 1)], xs_ref.at[pl.ds(pad_ref[0, r], 1)], sem.at[0]).start()
            return carry

        lax.fori_loop(0, n_pad, zbody, 0)
        pltpu.make_async_copy(xs_ref.at[pl.ds(0, n_pad)], xs_ref.at[pl.ds(0, n_pad)], sem.at[0]).wait()

    def body(j, carry):
        for u in range(2):
            r = 2 * j + u
            pltpu.make_async_copy(h2x_ref.at[pl.ds(r, 1)], xs_ref.at[pl.ds(slot_ref[0, r], 1)],
                                  sem.at[0]).start(priority=u)
        return carry

    lax.fori_loop(0, TD // 2, body, 0)
    pltpu.make_async_copy(h2x_ref, xs_ref.at[pl.ds(0, TD)], sem.at[0]).wait()


def _dispatch(slot3, pad_slots, h2x, n_sorted):
    rows = h2x.shape[0]
    n_pad = pad_slots.shape[0]
    return pl.pallas_call(
        _dispatch_kernel,
        grid=(rows // TD,),
        in_specs=[_slot_spec(slot3, TD),
                  pl.BlockSpec((1, n_pad), lambda i: (0, 0), memory_space=pltpu.SMEM),
                  pl.BlockSpec((TD, DX), lambda i: (i, 0))],
        out_specs=pl.BlockSpec(memory_space=pl.ANY),
        out_shape=jax.ShapeDtypeStruct((n_sorted, DX), F32),
        scratch_shapes=[pltpu.VMEM((8, DX), F32), pltpu.SemaphoreType.DMA((1,))],
        compiler_params=_cp(("arbitrary",)),
        name="moe_dispatch",
    )(slot3, pad_slots.reshape(1, n_pad), h2x)


def _moe_kernel(tg_ref, na_ref, x_ref, w1_ref, w3_ref, w2_ref, o_ref, hid_scr, w1b_scr, w3b_scr, w2b_scr):
    i = pl.program_id(0)

    @pl.when((i == 0) | (tg_ref[i] != tg_ref[jnp.maximum(i - 1, 0)]))
    def _():
        for e in range(EPG):
            cols = slice(e * EH, (e + 1) * EH)
            w1b_scr[:, cols] = w1_ref[:, cols].astype(BF16)
            w3b_scr[:, cols] = w3_ref[:, cols].astype(BF16)
            w2b_scr[cols, :] = w2_ref[cols, :].astype(BF16)

    @pl.when(i < na_ref[0])
    def _():
        x = x_ref[:, :D].astype(BF16)
        gt = x_ref[:, D:]
        for e in range(EPG):
            cols = slice(e * EH, (e + 1) * EH)
            h1 = jnp.dot(x, w1b_scr[:, cols], preferred_element_type=F32)
            h3 = jnp.dot(x, w3b_scr[:, cols], preferred_element_type=F32)
            hid_scr[:, cols] = (_silu(h1) * h3 * gt[:, e:e + 1]).astype(BF16)
        o_ref[...] = jnp.dot(hid_scr[...], w2b_scr[...], preferred_element_type=F32)

    @pl.when(i >= na_ref[0])
    def _():
        o_ref[...] = jnp.zeros_like(o_ref)


def _moe(tile_group, n_active, xs, w1, w3, w2, layer, n_tiles):
    x_idx = lambda i, tg, na: (jnp.minimum(i, na[0] - 1), 0)
    out_idx = lambda i, tg, na: (i, 0)
    w_idx = lambda i, tg, na: (layer, tg[i], 0, 0)
    once = pl.Buffered(1)
    grid_spec = pltpu.PrefetchScalarGridSpec(
        num_scalar_prefetch=2,
        grid=(n_tiles,),
        in_specs=[pl.BlockSpec((TS, DX), x_idx),
                  pl.BlockSpec((None, None, D, GH), w_idx, pipeline_mode=once),
                  pl.BlockSpec((None, None, D, GH), w_idx, pipeline_mode=once),
                  pl.BlockSpec((None, None, GH, D), w_idx, pipeline_mode=once)],
        out_specs=pl.BlockSpec((TS, D), out_idx),
        scratch_shapes=[pltpu.VMEM((TS, GH), BF16), pltpu.VMEM((D, GH), BF16), pltpu.VMEM((D, GH), BF16),
                        pltpu.VMEM((GH, D), BF16)],
    )
    return pl.pallas_call(
        _moe_kernel,
        grid_spec=grid_spec,
        out_shape=jax.ShapeDtypeStruct((xs.shape[0], D), F32),
        compiler_params=_cp(("arbitrary",)),
        name="moe_experts",
    )(tile_group, n_active, xs, w1, w3, w2)


def _combine_kernel(*refs, final, n_steps):
    if final:
        slot_ref, slot_next_ref, x1_ref, g2_ref, fg_ref, ys_ref, o_ref, buf, sem = refs
    else:
        slot_ref, slot_next_ref, x1_ref, g2_ref, ys_ref, o_ref, buf, sem = refs
    i = pl.program_id(0)

    def issue(sref, b):
        def body(j, carry):
            for u in range(2):
                r = 2 * j + u
                pltpu.make_async_copy(ys_ref.at[pl.ds(sref[0, r], 1)], buf.at[b, pl.ds(r, 1)],
                                      sem.at[b]).start(priority=u)
            return carry

        lax.fori_loop(0, TD // 2, body, 0)

    @pl.when(i == 0)
    def _():
        issue(slot_ref, 0)

    @pl.when(i + 1 < n_steps)
    def _():
        issue(slot_next_ref, (i + 1) % 2)

    b = i % 2
    pltpu.make_async_copy(ys_ref.at[pl.ds(0, TD)], buf.at[b], sem.at[b]).wait()

    x2 = x1_ref[...] + g2_ref[...] * buf[b]
    if final:
        ms = jnp.mean(x2 * x2, axis=-1, keepdims=True)
        x2 = x2 * lax.rsqrt(ms + EPS) * fg_ref[...]
    o_ref[...] = x2


def _combine(slot3, x1, g2, ys, n_mod, final_g=None):
    rows = x1.shape[0]
    n_steps = rows // TD
    final = final_g is not None
    mod_idx = lambda i: (jnp.minimum((i * TD) // L, n_mod), 0, 0)
    in_specs = [_slot_spec(slot3, TD), _slot_spec(slot3, TD, ahead=1),
                pl.BlockSpec((TD, D), lambda i: (i, 0)),
                pl.BlockSpec((None, 1, D), mod_idx)]
    args = [slot3, slot3, x1, g2]
    if final:
        in_specs.append(pl.BlockSpec((1, D), lambda i: (0, 0)))
        args.append(final_g.reshape(1, D))
    in_specs.append(pl.BlockSpec(memory_space=pl.ANY))
    args.append(ys)
    return pl.pallas_call(
        functools.partial(_combine_kernel, final=final, n_steps=n_steps),
        grid=(n_steps,),
        in_specs=in_specs,
        out_specs=pl.BlockSpec((TD, D), lambda i: (i, 0)),
        out_shape=jax.ShapeDtypeStruct((rows, D), F32),
        scratch_shapes=[pltpu.VMEM((2, TD, D), F32), pltpu.SemaphoreType.DMA((2,))],
        compiler_params=_cp(("arbitrary",)),
        name="moe_combine_final" if final else "moe_combine",
    )(*args)


def _sort_plan(counts_row, rows):
    counts = counts_row[:N_GROUPS].astype(jnp.int32)
    padded = ((counts + TS - 1) // TS) * TS
    ends = jnp.cumsum(padded)
    starts = ends - padded
    starts_row = jnp.zeros((1, 128), F32).at[0, :N_GROUPS].set(starts.astype(F32))
    n_tiles = rows // TS + N_GROUPS
    n_active = (ends[-1] // TS).reshape(1)
    tile_start = jnp.arange(n_tiles, dtype=jnp.int32) * TS
    tile_group = jnp.minimum(jnp.sum(tile_start[:, None] >= ends[None, :], axis=1), N_GROUPS - 1).astype(jnp.int32)
    r = jnp.arange(TS, dtype=jnp.int32)
    pad_real = ((starts + counts)[:, None] + r[None, :]).reshape(-1)
    is_pad = pad_real < jnp.repeat(ends, TS)
    tail = ends[-1] + jnp.cumsum(jnp.logical_not(is_pad).astype(jnp.int32)) - 1
    pad_slots = jnp.where(is_pad, pad_real, tail)
    return starts_row, pad_slots, tile_group, n_active, n_tiles, n_tiles * TS


def _moe_sorted(h2x, route_t, cnt, w1, w3, w2, layer):
    rows = h2x.shape[0]
    starts_row, pad_slots, tile_group, n_active, n_tiles, n_sorted = _sort_plan(cnt[0], rows)
    slot3 = _slots(route_t, starts_row)
    xs = _dispatch(slot3, pad_slots, h2x, n_sorted)
    ys = _moe(tile_group, n_active, xs, w1, w3, w2, layer, n_tiles)
    return slot3, ys


def _router_weights(gw, gb, ew, eb):
    wr = jnp.zeros((D, 128), F32).at[:, :N_GROUPS].set(gw).at[:, 8:8 + N_GROUPS * EPG].set(ew)
    br = jnp.zeros((1, 128), F32).at[0, :N_GROUPS].set(gb).at[0, 8:8 + N_GROUPS * EPG].set(eb)
    hi = wr.astype(BF16)
    lo = (wr - hi.astype(F32)).astype(BF16)
    return jnp.concatenate([hi, lo], axis=1), br


def kernel(x, c, ctx, c_ctx, ada_w, ada_b, norm1_g, norm2_g, w_in_e, conv_w, na_rpb, w_out_e, w_in_o,
           ret_decay_f, ret_decay_b, w_out_o, router_gw, router_gb, router_ew, router_eb,
           moe_w1, moe_w3, moe_w2, final_g):
    nb = x.shape[0]
    assert x.shape[1:] == (L, D) and ctx.shape[1:] == (CTX, D) and nb < 16 and ada_w.shape[0] == 2
    t_lat = nb * L
    t_ctx = nb * CTX
    t_all = t_lat + t_ctx
    tm = 1024 if t_ctx % 1024 == 0 else 512

    xsrc = (x.reshape(t_lat, D), ctx.reshape(t_ctx, D))
    c16 = jnp.zeros((16, D), F32).at[:nb].set(c).at[nb].set(c_ctx)
    mod = _ada(c16, ada_w, ada_b).reshape(2, 16, 6, 1, D)
    mods = [[mod[l, :, k] for k in range(6)] for l in range(2)]

    sh1, sc1, g1, sh2, sc2, g2 = mods[0]
    qkv = _inproj(xsrc, norm1_g[0], sh1, sc1, w_in_e[0].astype(BF16), t_lat, nb, tm)
    y_conv = (_conv(qkv, conv_w[0], L, nb, 0), _conv(qkv, conv_w[0], CTX, nb, t_lat // CTX))
    y_attn = (_natten(qkv, _natten_tcol(na_rpb[0]), nb), _ctx_attn(qkv, nb))
    w_out = w_out_e[0].astype(BF16)
    wr2, br = _router_weights(router_gw[0], router_gb[0], router_ew[0], router_eb[0])
    x1, h2x, route_t, cnt = _outproj([y_conv, y_attn], [w_out[:CONV_CH], w_out[CONV_CH:]], xsrc, g1, sh2, sc2,
                                     norm2_g[0], wr2, br, t_all, t_lat, nb, tm)
    slot3, ys = _moe_sorted(h2x, route_t, cnt, moe_w1, moe_w3, moe_w2, 0)

    sh1, sc1, g1n, sh2n, sc2n, g2n = mods[1]
    qkvg, xa = _inproj_comb(slot3, ys, x1, g2, norm1_g[1], sh1, sc1, w_in_o[0].astype(BF16), t_lat, nb, tm,
                            _rope_tables(tm))
    g1, sh2, sc2, g2 = g1n, sh2n, sc2n, g2n
    lg = jnp.stack([jnp.log1p(-jnp.exp(ret_decay_f[0].astype(F32))),
                    jnp.log1p(-jnp.exp(ret_decay_b[0].astype(F32)))])
    y_ret = _retention(qkvg, lg, nb)
    wr2, br = _router_weights(router_gw[1], router_gb[1], router_ew[1], router_eb[1])
    x1, h2x, route_t, cnt = _outproj([(y_ret,)], [w_out_o[0].astype(BF16)], (xa,), g1, sh2, sc2, norm2_g[1],
                                     wr2, br, t_lat, t_lat, nb, tm)
    slot3, ys = _moe_sorted(h2x, route_t, cnt, moe_w1, moe_w3, moe_w2, 1)
    out = _combine(slot3, x1, g2, ys, nb, final_g)
    return out.reshape(nb, L, D)
```

```python
import functools
import math

import numpy as np
import jax
import jax.numpy as jnp
from jax import lax
from jax.experimental import pallas as pl
from jax.experimental.pallas import tpu as pltpu

F32 = jnp.float32
BF16 = jnp.bfloat16
HIGHEST = lax.Precision.HIGHEST

D = 1024
L = 4096
GRID_W = 64
GRID_H = L // GRID_W
CTX = 256
EPS = 1e-6

CONV_CH = 512
NA_HEADS = 8
NA_DH = 64
WIN_H = 8
WIN_W = 16
EVEN_IN = 3072

RET_HEADS = 4
RET_DK = 256
RET_DV = 512
RET_C = 256
ODD_IN = 6144
ROPE_BASE = 10000.0

N_GROUPS = 4
EPG = 8
EH = 256
GH = EPG * EH

NEG = -1e30

NA_QR = 4
NA_KR = NA_QR + WIN_H - 1
NA_NQ = NA_QR * GRID_W
NA_NK = NA_KR * GRID_W
NA_WS_MAX = GRID_H - NA_KR

DX = D + 128
TS = 512
TD = 512

VMEM_LIMIT = 56 * 1024 * 1024


def _cp(sem, vmem=VMEM_LIMIT):
    return pltpu.CompilerParams(dimension_semantics=sem, vmem_limit_bytes=vmem)


def _silu(v):
    return v * (1.0 / (1.0 + jnp.exp(-v)))


def _src_specs(src, tm, lat_tiles):
    tail = tuple(src[0].shape[1:])
    zeros = (0,) * len(tail)
    if len(src) == 1:
        return [pl.BlockSpec((tm,) + tail, lambda i, *_: (i,) + zeros)]
    return [pl.BlockSpec((tm,) + tail, lambda i, *_: (jnp.minimum(i, lat_tiles - 1),) + zeros),
            pl.BlockSpec((tm,) + tail, lambda i, *_: (jnp.maximum(i - lat_tiles, 0),) + zeros)]


def _src_load(refs, i, lat_tiles, rows=slice(None)):
    if len(refs) == 1:
        return refs[0][rows, :]
    return jnp.where(i < lat_tiles, refs[0][rows, :], refs[1][rows, :])


def _ada_kernel(c_ref, w_ref, b_ref, o_ref):
    s = _silu(c_ref[...])
    o_ref[...] = jnp.dot(s, w_ref[...], precision=HIGHEST, preferred_element_type=F32) + b_ref[...]


def _ada(c16, ada_w, ada_b):
    depth = ada_w.shape[0]
    tn = 1536
    return pl.pallas_call(
        _ada_kernel,
        grid=(depth, 6 * D // tn),
        in_specs=[
            pl.BlockSpec((16, D), lambda l, j: (0, 0)),
            pl.BlockSpec((None, D, tn), lambda l, j: (l, 0, j)),
            pl.BlockSpec((None, 1, tn), lambda l, j: (l, 0, j)),
        ],
        out_specs=pl.BlockSpec((None, 16, tn), lambda l, j: (l, 0, j)),
        out_shape=jax.ShapeDtypeStruct((depth, 16, 6 * D), F32),
        compiler_params=_cp(("arbitrary", "arbitrary")),
        name="ada_mod",
    )(c16, ada_w, ada_b.reshape(depth, 1, 6 * D))


def _inproj_kernel(*refs, rope, tn, n_src, lat_tiles):
    x_refs, refs = refs[:n_src], refs[n_src:]
    if rope:
        g_ref, sh_ref, sc_ref, w_ref, cos_ref, sin_ref, o_ref, h_scr = refs
    else:
        g_ref, sh_ref, sc_ref, w_ref, o_ref, h_scr = refs
    i = pl.program_id(0)
    j = pl.program_id(1)

    if not rope and tn == w_ref.shape[1] == o_ref.shape[1]:
        coef = g_ref[...] * (1.0 + sc_ref[...])
        rs = 256
        for sbk in range(o_ref.shape[0] // rs):
            rows = slice(sbk * rs, (sbk + 1) * rs)
            x = _src_load(x_refs, i, lat_tiles, rows)
            ms = jnp.mean(x * x, axis=-1, keepdims=True)
            h = (x * lax.rsqrt(ms + EPS) * coef + sh_ref[...]).astype(BF16)
            o_ref[rows, :] = jnp.dot(h, w_ref[...], preferred_element_type=F32).astype(o_ref.dtype)
        return

    @pl.when(j == 0)
    def _():
        x = _src_load(x_refs, i, lat_tiles)
        ms = jnp.mean(x * x, axis=-1, keepdims=True)
        y = x * lax.rsqrt(ms + EPS) * g_ref[...]
        h_scr[...] = (y * (1.0 + sc_ref[...]) + sh_ref[...]).astype(BF16)

    acc = jnp.dot(h_scr[...], w_ref[...], preferred_element_type=F32)
    if not rope:
        o_ref[...] = acc.astype(o_ref.dtype)
        return

    @pl.when(j < 2)
    def _():
        cos = cos_ref[...]
        sin = sin_ref[...]
        scale = jnp.where(j == 1, RET_DK ** -0.5, 1.0).astype(F32)
        pieces = []
        for cblk in range(tn // 128):
            t = acc[:, cblk * 128:(cblk + 1) * 128]
            half = (cblk % 2) * 128
            r = pltpu.roll(t, 64, axis=1)
            pieces.append(t * cos[:, half:half + 128] + r * sin[:, half:half + 128])
        o_ref[...] = (jnp.concatenate(pieces, axis=1) * scale).astype(o_ref.dtype)

    @pl.when(j >= 2)
    def _():
        o_ref[...] = acc.astype(o_ref.dtype)


def _inproj(xsrc, gain, sh, sc, w, n_lat_rows, n_mod, tm, rope_tables=None):
    rows = sum(a.shape[0] for a in xsrc)
    n = w.shape[1]
    tn = 1024 if rope_tables is not None else n
    lat_tiles = n_lat_rows // tm
    mod_idx = lambda i, j: (jnp.minimum((i * tm) // L, n_mod), 0, 0)
    in_specs = _src_specs(xsrc, tm, lat_tiles) + [
        pl.BlockSpec((1, D), lambda i, j: (0, 0)),
        pl.BlockSpec((None, 1, D), mod_idx),
        pl.BlockSpec((None, 1, D), mod_idx),
        pl.BlockSpec((D, tn), lambda i, j: (0, j)),
    ]
    args = list(xsrc) + [gain.reshape(1, D), sh, sc, w]
    rope = rope_tables is not None
    if rope:
        per_seq = L // tm
        tab_idx = lambda i, j: (jnp.where(i < lat_tiles, i % per_seq, per_seq), 0)
        in_specs += [pl.BlockSpec((tm, RET_DK), tab_idx), pl.BlockSpec((tm, RET_DK), tab_idx)]
        args += list(rope_tables)
    return pl.pallas_call(
        functools.partial(_inproj_kernel, rope=rope, tn=tn, n_src=len(xsrc), lat_tiles=lat_tiles),
        grid=(rows // tm, n // tn),
        in_specs=in_specs,
        out_specs=pl.BlockSpec((tm, tn), lambda i, j: (i, j)),
        out_shape=jax.ShapeDtypeStruct((rows, n), BF16),
        scratch_shapes=[pltpu.VMEM((tm, D), BF16)],
        compiler_params=_cp(("arbitrary", "arbitrary")),
        name="inproj_rope" if rope else "inproj",
    )(*args)


def _inproj_comb_kernel(slot_ref, slotn_ref, x1_ref, g2_ref, g_ref, sh_ref, sc_ref, w_ref, cos_ref, sin_ref, ys_ref,
                        o_ref, xa_ref, h_scr, buf, sem, *, tn, tm, n_tiles, n_j, ch):
    i = pl.program_id(0)
    j = pl.program_id(1)
    rows_buf = ch * n_j

    def start_row(sref, b, r, u):
        src = sref[0, jnp.minimum(r, tm - 1)]
        pltpu.make_async_copy(ys_ref.at[pl.ds(src, 1)], buf.at[b, pl.ds(r, 1)], sem.at[b]).start(priority=u)

    @pl.when(j == 0)
    def _():
        @pl.when(i == 0)
        def _():
            def body(jj, carry):
                for u in range(2):
                    start_row(slot_ref, 0, 2 * jj + u, u)
                return carry

            lax.fori_loop(0, rows_buf // 2, body, 0)

        b = i % 2
        pltpu.make_async_copy(ys_ref.at[pl.ds(0, rows_buf)], buf.at[b], sem.at[b]).wait()
        x = x1_ref[...] + g2_ref[...] * buf[b, :tm, :]
        xa_ref[...] = x
        ms = jnp.mean(x * x, axis=-1, keepdims=True)
        coef = g_ref[...] * (1.0 + sc_ref[...])
        h_scr[...] = (x * lax.rsqrt(ms + EPS) * coef + sh_ref[...]).astype(BF16)

    bn = (i + 1) % 2
    for u in range(ch):
        start_row(slotn_ref, bn, j * ch + u, u % 2)

    acc = jnp.dot(h_scr[...], w_ref[...], preferred_element_type=F32)

    qk_tiles = 2 * RET_HEADS * RET_DK // tn

    @pl.when(j < qk_tiles)
    def _():
        cos = cos_ref[...]
        sin = sin_ref[...]
        pieces = []
        for cblk in range(tn // 128):
            t = acc[:, cblk * 128:(cblk + 1) * 128]
            half = (cblk % 2) * 128
            r = pltpu.roll(t, 64, axis=1)
            is_k = j * tn + cblk * 128 >= RET_HEADS * RET_DK
            scale = jnp.where(is_k, RET_DK ** -0.5, 1.0).astype(F32)
            pieces.append((t * cos[:, half:half + 128] + r * sin[:, half:half + 128]) * scale)
        o_ref[...] = jnp.concatenate(pieces, axis=1).astype(o_ref.dtype)

    @pl.when(j >= qk_tiles)
    def _():
        o_ref[...] = acc.astype(o_ref.dtype)

    @pl.when((i == n_tiles - 1) & (j == n_j - 1))
    def _():
        pltpu.make_async_copy(ys_ref.at[pl.ds(0, rows_buf)], buf.at[bn], sem.at[bn]).wait()


def _inproj_comb(slot3, ys, x1, g2, gain, sh, sc, w, n_lat_rows, n_mod, tm, rope_tables):
    rows = x1.shape[0]
    n = w.shape[1]
    tn = 2048
    n_tiles = rows // tm
    lat_tiles = n_lat_rows // tm
    per_seq = L // tm
    mod_idx = lambda i, j: (jnp.minimum((i * tm) // L, n_mod), 0, 0)
    tab_idx = lambda i, j: (jnp.where(i < lat_tiles, i % per_seq, per_seq), 0)
    n_j = n // tn
    ch = -(-tm // n_j)
    while (ch * n_j) % 8:
        ch += 1
    in_specs = [_slot_spec(slot3, tm), _slot_spec(slot3, tm, ahead=1),
                pl.BlockSpec((tm, D), lambda i, j: (i, 0)),
                pl.BlockSpec((None, 1, D), mod_idx),
                pl.BlockSpec((1, D), lambda i, j: (0, 0)),
                pl.BlockSpec((None, 1, D), mod_idx),
                pl.BlockSpec((None, 1, D), mod_idx),
                pl.BlockSpec((D, tn), lambda i, j: (0, j)),
                pl.BlockSpec((tm, RET_DK), tab_idx),
                pl.BlockSpec((tm, RET_DK), tab_idx),
                pl.BlockSpec(memory_space=pl.ANY)]
    return pl.pallas_call(
        functools.partial(_inproj_comb_kernel, tn=tn, tm=tm, n_tiles=n_tiles, n_j=n_j, ch=ch),
        grid=(n_tiles, n_j),
        in_specs=in_specs,
        out_specs=[pl.BlockSpec((tm, tn), lambda i, j: (i, j)),
                   pl.BlockSpec((tm, D), lambda i, j: (i, 0))],
        out_shape=[jax.ShapeDtypeStruct((rows, n), BF16),
                   jax.ShapeDtypeStruct((rows, D), F32)],
        scratch_shapes=[pltpu.VMEM((tm, D), BF16), pltpu.VMEM((2, ch * n_j, D), F32),
                        pltpu.SemaphoreType.DMA((2,))],
        compiler_params=_cp(("arbitrary", "arbitrary")),
        name="inproj_rope_combine",
    )(slot3, slot3, x1, g2, gain.reshape(1, D), sh, sc, w, rope_tables[0], rope_tables[1], ys)


def _rope_tables(tm):
    pos = np.arange(L)
    half = RET_DK // 4
    freqs = ROPE_BASE ** (-np.arange(half, dtype=np.float64) / half)
    ang_r = (pos // GRID_W)[:, None] * freqs[None, :]
    ang_c = (pos % GRID_W)[:, None] * freqs[None, :]
    cos = np.concatenate([np.cos(ang_r), np.cos(ang_r), np.cos(ang_c), np.cos(ang_c)], axis=1)
    sin = np.concatenate([-np.sin(ang_r), np.sin(ang_r), -np.sin(ang_c), np.sin(ang_c)], axis=1)
    cos = np.concatenate([cos, np.ones((tm, RET_DK))], axis=0)
    sin = np.concatenate([sin, np.zeros((tm, RET_DK))], axis=0)
    return jnp.asarray(cos, F32), jnp.asarray(sin, F32)


def _conv_kernel(ab_ref, ac_ref, ax_ref, w_ref, o_ref, *, seq):
    u = ac_ref[...].astype(F32) * ax_ref[...].astype(F32)
    row = lax.broadcasted_iota(jnp.int32, u.shape, 0)
    up = jnp.where(row == 0, 0.0, pltpu.roll(u, 1, axis=0))
    un = jnp.where(row == seq - 1, 0.0, pltpu.roll(u, seq - 1, axis=0))
    w = w_ref[...]
    y = ab_ref[...].astype(F32) * (up * w[0:1, :] + u * w[1:2, :] + un * w[2:3, :])
    o_ref[...] = y.astype(o_ref.dtype)


def _conv(qkv, conv_w, seq, n_seq, row_blk0):
    nb = CONV_CH // 128
    return pl.pallas_call(
        functools.partial(_conv_kernel, seq=seq),
        grid=(n_seq, nb),
        in_specs=[
            pl.BlockSpec((seq, 128), lambda b, j: (row_blk0 + b, j)),
            pl.BlockSpec((seq, 128), lambda b, j: (row_blk0 + b, nb + j)),
            pl.BlockSpec((seq, 128), lambda b, j: (row_blk0 + b, 2 * nb + j)),
            pl.BlockSpec((3, 128), lambda b, j: (0, j)),
        ],
        out_specs=pl.BlockSpec((seq, 128), lambda b, j: (b, j)),
        out_shape=jax.ShapeDtypeStruct((n_seq * seq, CONV_CH), BF16),
        compiler_params=_cp(("arbitrary", "arbitrary")),
        name="short_conv",
    )(qkv, qkv, qkv, conv_w)


def _natten_tcol(rpb):
    qc = np.arange(GRID_W)
    kc = np.arange(GRID_W)
    cs = np.clip(qc - WIN_W // 2, 0, GRID_W - WIN_W)
    col_ok = (kc[None, :] >= cs[:, None]) & (kc[None, :] < cs[:, None] + WIN_W)
    dx = np.clip(kc[None, :] - qc[:, None] + WIN_W - 1, 0, 2 * WIN_W - 2)
    tcol = jnp.where(col_ok[None, None], rpb.astype(F32)[:, :, dx], NEG)
    return tcol.reshape(NA_HEADS // 2, 2, 2 * WIN_H - 1, GRID_W, GRID_W)


def _natten_row_classes():
    table = []
    for r0 in (0, 2 * NA_QR, GRID_H - NA_QR):
        ws = int(np.clip(r0 - WIN_H // 2, 0, NA_WS_MAX))
        rows = []
        for q in range(NA_QR):
            qr = r0 + q
            rs = int(np.clip(qr - WIN_H // 2, 0, GRID_H - WIN_H))
            rows.append([(ws + k) - qr + WIN_H - 1 if rs <= ws + k < rs + WIN_H else None for k in range(NA_KR)])
        table.append(rows)
    return table


def _attend(qa, kw, vw, kc, vc, bias):
    dn = (((1,), (1,)), ((), ()))
    s2 = lax.dot_general(qa, kc, dn, preferred_element_type=F32)
    m = jnp.max(s2, axis=1, keepdims=True)
    if kw is not None:
        s1 = lax.dot_general(qa, kw, dn, preferred_element_type=F32) + bias
        m = jnp.maximum(m, jnp.max(s1, axis=1, keepdims=True))
        p1 = jnp.exp(s1 - m)
    p2 = jnp.exp(s2 - m)
    den = jnp.sum(p2, axis=1, keepdims=True)
    o = jnp.dot(p2.astype(BF16), vc, preferred_element_type=F32)
    if kw is not None:
        den = den + jnp.sum(p1, axis=1, keepdims=True)
        o = o + jnp.dot(p1.astype(BF16), vw, preferred_element_type=F32)
    return o / den


NA_SUB = 4


def _natten_kernel(q_ref, k_ref, v_ref, kc_ref, vc_ref, tcol_ref, o_ref, bias_ref):
    n_rb = GRID_H // NA_QR

    @pl.when((pl.program_id(1) == 0) & (pl.program_id(2) == 0))
    def _():
        neg = jnp.full((GRID_W, GRID_W), NEG, F32)
        for cls, rows in enumerate(_natten_row_classes()):
            for a in range(2):
                for qr, dys in enumerate(rows):
                    blocks = [neg if dy is None else tcol_ref[a, dy] for dy in dys]
                    bias_ref[cls, a, qr * GRID_W:(qr + 1) * GRID_W, :] = jnp.concatenate(blocks, axis=1)

    lane = lax.broadcasted_iota(jnp.int32, (NA_NQ, 128), 1)
    for sb in range(NA_SUB):
        rb = pl.program_id(2) * NA_SUB + sb
        ws = jnp.clip(rb * NA_QR - WIN_H // 2, 0, NA_WS_MAX)
        start = pl.multiple_of(ws * GRID_W, GRID_W)
        kw = k_ref[pl.ds(start, NA_NK), :]
        vw = v_ref[pl.ds(start, NA_NK), :]
        cls = jnp.where(rb == 0, 0, jnp.where(rb == n_rb - 1, 2, 1))
        q = q_ref[sb * NA_NQ:(sb + 1) * NA_NQ, :] * (NA_DH ** -0.5)
        zero = jnp.zeros_like(q)
        o0 = _attend(jnp.where(lane < NA_DH, q, zero), kw, vw, kc_ref[...], vc_ref[...], bias_ref[cls, 0])
        o1 = _attend(jnp.where(lane >= NA_DH, q, zero), kw, vw, kc_ref[...], vc_ref[...], bias_ref[cls, 1])
        o_ref[sb * NA_NQ:(sb + 1) * NA_NQ, :] = jnp.where(lane < NA_DH, o0, o1).astype(o_ref.dtype)


def _natten(qkv, tcol, nb):
    n_steps = GRID_H // NA_QR // NA_SUB
    qcol, kcol, vcol = 3 * CONV_CH // 128, 3 * CONV_CH // 128 + 4, 3 * CONV_CH // 128 + 8
    ctx_blk0 = nb * L // CTX
    return pl.pallas_call(
        _natten_kernel,
        grid=(NA_HEADS // 2, nb, n_steps),
        in_specs=[
            pl.BlockSpec((NA_SUB * NA_NQ, 128), lambda h, b, r: (b * n_steps + r, qcol + h)),
            pl.BlockSpec((L, 128), lambda h, b, r: (b, kcol + h)),
            pl.BlockSpec((L, 128), lambda h, b, r: (b, vcol + h)),
            pl.BlockSpec((CTX, 128), lambda h, b, r: (ctx_blk0 + b, kcol + h)),
            pl.BlockSpec((CTX, 128), lambda h, b, r: (ctx_blk0 + b, vcol + h)),
            pl.BlockSpec((None, 2, 2 * WIN_H - 1, GRID_W, GRID_W), lambda h, b, r: (h, 0, 0, 0, 0)),
        ],
        out_specs=pl.BlockSpec((NA_SUB * NA_NQ, 128), lambda h, b, r: (b * n_steps + r, h)),
        out_shape=jax.ShapeDtypeStruct((nb * L, NA_HEADS * NA_DH), BF16),
        scratch_shapes=[pltpu.VMEM((3, 2, NA_NQ, NA_NK), F32)],
        compiler_params=_cp(("arbitrary", "arbitrary", "arbitrary")),
        name="natten",
    )(qkv, qkv, qkv, qkv, qkv, tcol)


def _ctx_attn_kernel(q_ref, kc_ref, vc_ref, o_ref):
    q = q_ref[...] * (NA_DH ** -0.5)
    lane = lax.broadcasted_iota(jnp.int32, q.shape, 1)
    zero = jnp.zeros_like(q)
    o0 = _attend(jnp.where(lane < NA_DH, q, zero), None, None, kc_ref[...], vc_ref[...], None)
    o1 = _attend(jnp.where(lane >= NA_DH, q, zero), None, None, kc_ref[...], vc_ref[...], None)
    o_ref[...] = jnp.where(lane < NA_DH, o0, o1).astype(o_ref.dtype)


def _ctx_attn(qkv, nb):
    qcol, kcol, vcol = 3 * CONV_CH // 128, 3 * CONV_CH // 128 + 4, 3 * CONV_CH // 128 + 8
    ctx_blk0 = nb * L // CTX
    return pl.pallas_call(
        _ctx_attn_kernel,
        grid=(nb, NA_HEADS // 2),
        in_specs=[
            pl.BlockSpec((CTX, 128), lambda b, h: (ctx_blk0 + b, qcol + h)),
            pl.BlockSpec((CTX, 128), lambda b, h: (ctx_blk0 + b, kcol + h)),
            pl.BlockSpec((CTX, 128), lambda b, h: (ctx_blk0 + b, vcol + h)),
        ],
        out_specs=pl.BlockSpec((CTX, 128), lambda b, h: (b, h)),
        out_shape=jax.ShapeDtypeStruct((nb * CTX, NA_HEADS * NA_DH), BF16),
        compiler_params=_cp(("arbitrary", "arbitrary")),
        name="ctx_attn",
    )(qkv, qkv, qkv)


def _retention_kernel(lg_ref, q_ref, k_ref, v_ref, g_ref, kc_ref, vc_ref, o_ref, y_scr, stf_scr, stb_scr):
    h = pl.program_id(1)
    n_chunks = L // RET_C
    pos_r = lax.broadcasted_iota(jnp.int32, (RET_C, 1), 0).astype(F32)
    rel = pos_r - lax.broadcasted_iota(jnp.int32, (1, RET_C), 1).astype(F32)
    dn_t = (((1,), (1,)), ((), ()))
    dn_k = (((0,), (0,)), ((), ()))

    def decays(lg, fwd):
        if fwd:
            d_intra = jnp.where(rel >= 0, jnp.exp(lg * jnp.maximum(rel, 0.0)), 0.0)
            d_q = jnp.exp(lg * (pos_r + 1.0))
            d_k = jnp.exp(lg * (RET_C - 1.0 - pos_r))
        else:
            d_intra = jnp.where(rel <= 0, jnp.exp(lg * jnp.maximum(-rel, 0.0)), 0.0)
            d_q = jnp.exp(lg * (RET_C - pos_r))
            d_k = jnp.exp(lg * pos_r)
        d_c = jnp.exp(jnp.full((1, 1), RET_C, F32) * lg)
        return d_intra, d_q, d_k, d_c

    def ctx_state(st_scr, d_k, d_c, order):
        st_scr[...] = jnp.zeros_like(st_scr)
        for jc in order:
            kd = (kc_ref[jc * RET_C:(jc + 1) * RET_C, :].astype(F32) * d_k).astype(BF16)
            vc = vc_ref[jc * RET_C:(jc + 1) * RET_C, :]
            st_scr[...] = st_scr[...] * d_c + lax.dot_general(kd, vc, dn_k, preferred_element_type=F32)

    def chunk_out(st_scr, c, d_intra, d_q, d_k, d_c):
        r0 = pl.multiple_of(c * RET_C, RET_C)
        q = q_ref[pl.ds(r0, RET_C), :]
        k = k_ref[pl.ds(r0, RET_C), :]
        v = v_ref[pl.ds(r0, RET_C), :]
        s = lax.dot_general(q, k, dn_t, preferred_element_type=F32) * d_intra
        st = st_scr[...]
        o = (jnp.dot(s.astype(BF16), v, preferred_element_type=F32)
             + jnp.dot(q, st.astype(BF16), preferred_element_type=F32) * d_q)
        kd = (k.astype(F32) * d_k).astype(BF16)
        st_scr[...] = st * d_c + lax.dot_general(kd, v, dn_k, preferred_element_type=F32)
        return r0, o

    def finish(r0, y):
        y = y * lax.rsqrt(jnp.mean(y * y, axis=-1, keepdims=True) + EPS)
        g = g_ref[pl.ds(r0, RET_C), :].astype(F32)
        o_ref[pl.ds(r0, RET_C), :] = (_silu(g) * y).astype(o_ref.dtype)

    dec_f = decays(lg_ref[0, h], True)
    dec_b = decays(lg_ref[1, h], False)
    ctx_state(stf_scr, dec_f[2], dec_f[3], range(CTX // RET_C))
    ctx_state(stb_scr, dec_b[2], dec_b[3], range(CTX // RET_C - 1, -1, -1))

    def first_half(i, carry):
        r0, o = chunk_out(stf_scr, i, *dec_f)
        y_scr[pl.ds(r0, RET_C), :] = o
        r1, o = chunk_out(stb_scr, n_chunks - 1 - i, *dec_b)
        y_scr[pl.ds(r1, RET_C), :] = o
        return carry

    lax.fori_loop(0, n_chunks // 2, first_half, 0)

    def second_half(i, carry):
        r0, o = chunk_out(stf_scr, i, *dec_f)
        finish(r0, y_scr[pl.ds(r0, RET_C), :] + o)
        r1, o = chunk_out(stb_scr, n_chunks - 1 - i, *dec_b)
        finish(r1, y_scr[pl.ds(r1, RET_C), :] + o)
        return carry

    lax.fori_loop(n_chunks // 2, n_chunks, second_half, 0)


def _retention(qkvg, lg, nb):
    ctx_blk0 = nb * L // CTX
    kq = RET_HEADS
    vq = 2 * RET_HEADS * RET_DK // RET_DV
    gq = vq + RET_HEADS
    return pl.pallas_call(
        _retention_kernel,
        grid=(nb, RET_HEADS),
        in_specs=[
            pl.BlockSpec(memory_space=pltpu.SMEM),
            pl.BlockSpec((L, RET_DK), lambda b, h: (b, h)),
            pl.BlockSpec((L, RET_DK), lambda b, h: (b, kq + h)),
            pl.BlockSpec((L, RET_DV), lambda b, h: (b, vq + h)),
            pl.BlockSpec((L, RET_DV), lambda b, h: (b, gq + h)),
            pl.BlockSpec((CTX, RET_DK), lambda b, h: (ctx_blk0 + b, kq + h)),
            pl.BlockSpec((CTX, RET_DV), lambda b, h: (ctx_blk0 + b, vq + h)),
        ],
        out_specs=pl.BlockSpec((L, RET_DV), lambda b, h: (b, h)),
        out_shape=jax.ShapeDtypeStruct((nb * L, RET_HEADS * RET_DV), BF16),
        scratch_shapes=[pltpu.VMEM((L, RET_DV), F32), pltpu.VMEM((RET_DK, RET_DV), F32),
                        pltpu.VMEM((RET_DK, RET_DV), F32)],
        compiler_params=_cp(("arbitrary", "arbitrary")),
        name="retention",
    )(lg, qkvg, qkvg, qkvg, qkvg, qkvg, qkvg)


def _outproj_kernel(*refs, src_lens, lat_tiles):
    i = pl.program_id(0)
    srcs = []
    for n in src_lens:
        srcs.append(refs[:n])
        refs = refs[n:]
    n_act = len(src_lens) - 1
    w_refs, refs = refs[:n_act], refs[n_act:]
    (g1_ref, sh2_ref, sc2_ref, n2_ref, wr_ref, br_ref, tri_ref,
     x1_ref, h2x_ref, rt_ref, cnt_ref, carry_scr) = refs

    @pl.when(i == 0)
    def _():
        carry_scr[...] = jnp.zeros_like(carry_scr)

    coef = n2_ref[...] * (1.0 + sc2_ref[...])
    rs = tri_ref.shape[0]
    for sbk in range(x1_ref.shape[0] // rs):
        rows = slice(sbk * rs, (sbk + 1) * rs)
        y = None
        for a_src, w_ref in zip(srcs[:n_act], w_refs):
            t = jnp.dot(_src_load(a_src, i, lat_tiles, rows), w_ref[...], preferred_element_type=F32)
            y = t if y is None else y + t
        x1 = _src_load(srcs[n_act], i, lat_tiles, rows) + g1_ref[...] * y
        x1_ref[rows, :] = x1
        ms = jnp.mean(x1 * x1, axis=-1, keepdims=True)
        h2 = x1 * lax.rsqrt(ms + EPS) * coef + sh2_ref[...]
        h2x_ref[rows, :D] = h2
        h_hi = h2.astype(BF16)
        h_lo = (h2 - h_hi.astype(F32)).astype(BF16)
        p = jnp.dot(h_hi, wr_ref[...], preferred_element_type=F32)
        q = jnp.dot(h_lo, wr_ref[:, :128], preferred_element_type=F32)
        route = _route(p[:, :128] + p[:, 128:] + q + br_ref[...], tri_ref, carry_scr)
        h2x_ref[rows, D:] = route
        rt_ref[:, rows] = route.T[8:16, :]
    cnt_ref[...] = carry_scr[...]


def _outproj(acts, ws, xsrc, g1, sh2, sc2, n2g, wr2, br, rows, n_lat_rows, n_mod, tm):
    lat_tiles = n_lat_rows // tm
    rs = tm
    tri = jnp.asarray(np.tril(np.ones((rs, rs), np.float32), -1), BF16)
    mod_idx = lambda i: (jnp.minimum((i * tm) // L, n_mod), 0, 0)
    in_specs, args = [], []
    for src in list(acts) + [xsrc]:
        in_specs += _src_specs(src, tm, lat_tiles)
        args += list(src)
    in_specs += [pl.BlockSpec(w.shape, lambda i: (0, 0)) for w in ws]
    in_specs += [pl.BlockSpec((None, 1, D), mod_idx),
                 pl.BlockSpec((None, 1, D), mod_idx),
                 pl.BlockSpec((None, 1, D), mod_idx),
                 pl.BlockSpec((1, D), lambda i: (0, 0)),
                 pl.BlockSpec((D, 256), lambda i: (0, 0)),
                 pl.BlockSpec((1, 128), lambda i: (0, 0)),
                 pl.BlockSpec((rs, rs), lambda i: (0, 0))]
    args += list(ws) + [g1, sh2, sc2, n2g.reshape(1, D), wr2, br, tri]
    return pl.pallas_call(
        functools.partial(_outproj_kernel, src_lens=tuple(len(s) for s in list(acts) + [xsrc]),
                          lat_tiles=lat_tiles),
        grid=(rows // tm,),
        in_specs=in_specs,
        out_specs=[pl.BlockSpec((tm, D), lambda i: (i, 0)),
                   pl.BlockSpec((tm, DX), lambda i: (i, 0)),
                   pl.BlockSpec((None, 8, tm), lambda i: (i, 0, 0)),
                   pl.BlockSpec((8, 128), lambda i: (0, 0))],
        out_shape=[jax.ShapeDtypeStruct((rows, D), F32),
                   jax.ShapeDtypeStruct((rows, DX), F32),
                   jax.ShapeDtypeStruct((rows // tm, 8, tm), F32),
                   jax.ShapeDtypeStruct((8, 128), F32)],
        scratch_shapes=[pltpu.VMEM((8, 128), F32)],
        compiler_params=_cp(("arbitrary",)),
        name="outproj_router",
    )(*args)


def _route(lg, tri_ref, carry_scr):
    lane = lax.broadcasted_iota(jnp.int32, lg.shape, 1)
    lane_f = lane.astype(F32)
    big = jnp.float32(1e9)

    gmask = lane < N_GROUPS
    gl = jnp.where(gmask, lg, NEG)
    gm = jnp.max(gl, axis=1, keepdims=True)
    gidx = jnp.min(jnp.where(gl == gm, lane_f, big), axis=1, keepdims=True)
    g_val = 1.0 / jnp.sum(jnp.where(gmask, jnp.exp(gl - gm), 0.0), axis=1, keepdims=True)

    base = 8.0 + 8.0 * gidx
    emask = (lane_f >= base) & (lane_f < base + EPG)
    el = jnp.where(emask, lg, NEG)
    m1 = jnp.max(el, axis=1, keepdims=True)
    i1 = jnp.min(jnp.where(el == m1, lane_f, big), axis=1, keepdims=True)
    el2 = jnp.where(lane_f == i1, NEG, el)
    m2 = jnp.max(el2, axis=1, keepdims=True)
    i2 = jnp.min(jnp.where(el2 == m2, lane_f, big), axis=1, keepdims=True)
    r = jnp.exp(m2 - m1)
    w1 = g_val / (1.0 + r)
    w2 = w1 * r
    gates = jnp.where(lane_f == i1 - base, w1, jnp.where(lane_f == i2 - base, w2, 0.0))

    onehot = jnp.where(lane_f == gidx, 1.0, 0.0)
    before = jnp.dot(tri_ref[...], onehot.astype(BF16), preferred_element_type=F32)
    carry = carry_scr[0:1, :]
    rank = jnp.sum(jnp.where(lane_f == gidx, before + carry, 0.0), axis=1, keepdims=True)
    carry_scr[0:1, :] = carry + jnp.sum(onehot, axis=0, keepdims=True)

    return jnp.where(lane < EPG, gates, jnp.where(lane == 8, gidx, jnp.where(lane == 9, rank, 0.0)))


def _slot_kernel(rt_ref, st_ref, o_ref):
    g = rt_ref[0:1, :]
    slot = rt_ref[1:2, :]
    st = st_ref[...]
    for k in range(N_GROUPS):
        slot = slot + jnp.where(g == k, st[:, k:k + 1], 0.0)
    o_ref[...] = slot.astype(jnp.int32)


def _slots(route_t, starts_row):
    n, _, st = route_t.shape
    return pl.pallas_call(
        _slot_kernel,
        grid=(n,),
        in_specs=[pl.BlockSpec((None, 8, st), lambda i: (i, 0, 0)),
                  pl.BlockSpec((1, 128), lambda i: (0, 0))],
        out_specs=pl.BlockSpec((None, 1, st), lambda i: (i, 0, 0)),
        out_shape=jax.ShapeDtypeStruct((n, 1, st), jnp.int32),
        compiler_params=_cp(("arbitrary",)),
        name="moe_slots",
    )(route_t, starts_row)


def _slot_spec(slot3, tile, ahead=0):
    n, _, st = slot3.shape
    per = st // tile
    last = n * per - 1

    def idx(i, *_):
        t = jnp.minimum(i + ahead, last)
        return (t // per, 0, t % per)

    return pl.BlockSpec((None, 1, tile), idx, memory_space=pltpu.SMEM)


def _dispatch_kernel(slot_ref, pad_ref, h2x_ref, xs_ref, zx_scr, sem):
    i = pl.program_id(0)
    n_pad = pad_ref.shape[1]

    @pl.when(i == 0)
    def _():
        zx_scr[...] = jnp.zeros_like(zx_scr)

        def zbody(r, carry):
            pltpu.make_async_copy(zx_scr.at[pl.ds(0, 1)], xs_ref.at[pl.ds(pad_ref[0, r], 1)], sem.at[0]).start()
            return carry

        lax.fori_loop(0, n_pad, zbody, 0)
        pltpu.make_async_copy(xs_ref.at[pl.ds(0, n_pad)], xs_ref.at[pl.ds(0, n_pad)], sem.at[0]).wait()

    def body(j, carry):
        for u in range(2):
            r = 2 * j + u
            pltpu.make_async_copy(h2x_ref.at[pl.ds(r, 1)], xs_ref.at[pl.ds(slot_ref[0, r], 1)],
                                  sem.at[0]).start(priority=u)
        return carry

    td = h2x_ref.shape[0]
    lax.fori_loop(0, td // 2, body, 0)
    pltpu.make_async_copy(h2x_ref, xs_ref.at[pl.ds(0, td)], sem.at[0]).wait()


def _dispatch(slot3, pad_slots, h2x, n_sorted):
    rows = h2x.shape[0]
    n_pad = pad_slots.shape[0]
    td = slot3.shape[2]
    return pl.pallas_call(
        _dispatch_kernel,
        grid=(rows // td,),
        in_specs=[_slot_spec(slot3, td),
                  pl.BlockSpec((1, n_pad), lambda i: (0, 0), memory_space=pltpu.SMEM),
                  pl.BlockSpec((td, DX), lambda i: (i, 0))],
        out_specs=pl.BlockSpec(memory_space=pl.ANY),
        out_shape=jax.ShapeDtypeStruct((n_sorted, DX), F32),
        scratch_shapes=[pltpu.VMEM((8, DX), F32), pltpu.SemaphoreType.DMA((1,))],
        compiler_params=_cp(("arbitrary",)),
        name="moe_dispatch",
    )(slot3, pad_slots.reshape(1, n_pad), h2x)


def _moe_kernel(tg_ref, na_ref, x_ref, w1_ref, w3_ref, w2_ref, o_ref, hid_scr, w1b_scr, w3b_scr, w2b_scr):
    i = pl.program_id(0)

    @pl.when((i == 0) | (tg_ref[i] != tg_ref[jnp.maximum(i - 1, 0)]))
    def _():
        for e in range(EPG):
            cols = slice(e * EH, (e + 1) * EH)
            w1b_scr[:, cols] = w1_ref[:, cols].astype(BF16)
            w3b_scr[:, cols] = w3_ref[:, cols].astype(BF16)
            w2b_scr[cols, :] = w2_ref[cols, :].astype(BF16)

    @pl.when(i < na_ref[0])
    def _():
        x = x_ref[:, :D].astype(BF16)
        gt = x_ref[:, D:]
        for e in range(EPG):
            cols = slice(e * EH, (e + 1) * EH)
            h1 = jnp.dot(x, w1b_scr[:, cols], preferred_element_type=F32)
            h3 = jnp.dot(x, w3b_scr[:, cols], preferred_element_type=F32)
            hid_scr[:, cols] = (_silu(h1) * h3 * gt[:, e:e + 1]).astype(BF16)
        o_ref[...] = jnp.dot(hid_scr[...], w2b_scr[...], preferred_element_type=F32)

    @pl.when(i >= na_ref[0])
    def _():
        o_ref[...] = jnp.zeros_like(o_ref)


def _moe(tile_group, n_active, xs, w1, w3, w2, layer, n_tiles):
    x_idx = lambda i, tg, na: (jnp.minimum(i, na[0] - 1), 0)
    out_idx = lambda i, tg, na: (i, 0)
    w_idx = lambda i, tg, na: (layer, tg[i], 0, 0)
    once = pl.Buffered(1)
    grid_spec = pltpu.PrefetchScalarGridSpec(
        num_scalar_prefetch=2,
        grid=(n_tiles,),
        in_specs=[pl.BlockSpec((TS, DX), x_idx),
                  pl.BlockSpec((None, None, D, GH), w_idx, pipeline_mode=once),
                  pl.BlockSpec((None, None, D, GH), w_idx, pipeline_mode=once),
                  pl.BlockSpec((None, None, GH, D), w_idx, pipeline_mode=once)],
        out_specs=pl.BlockSpec((TS, D), out_idx),
        scratch_shapes=[pltpu.VMEM((TS, GH), BF16), pltpu.VMEM((D, GH), BF16), pltpu.VMEM((D, GH), BF16),
                        pltpu.VMEM((GH, D), BF16)],
    )
    return pl.pallas_call(
        _moe_kernel,
        grid_spec=grid_spec,
        out_shape=jax.ShapeDtypeStruct((xs.shape[0], D), F32),
        compiler_params=_cp(("arbitrary",)),
        name="moe_experts",
    )(tile_group, n_active, xs, w1, w3, w2)


def _combine_kernel(*refs, final, n_steps):
    if final:
        slot_ref, slot_next_ref, x1_ref, g2_ref, fg_ref, ys_ref, o_ref, buf, sem = refs
    else:
        slot_ref, slot_next_ref, x1_ref, g2_ref, ys_ref, o_ref, buf, sem = refs
    i = pl.program_id(0)

    def issue(sref, b):
        def body(j, carry):
            for u in range(2):
                r = 2 * j + u
                pltpu.make_async_copy(ys_ref.at[pl.ds(sref[0, r], 1)], buf.at[b, pl.ds(r, 1)],
                                      sem.at[b]).start(priority=u)
            return carry

        lax.fori_loop(0, TD // 2, body, 0)

    @pl.when(i == 0)
    def _():
        issue(slot_ref, 0)

    @pl.when(i + 1 < n_steps)
    def _():
        issue(slot_next_ref, (i + 1) % 2)

    b = i % 2
    pltpu.make_async_copy(ys_ref.at[pl.ds(0, TD)], buf.at[b], sem.at[b]).wait()

    x2 = x1_ref[...] + g2_ref[...] * buf[b]
    if final:
        ms = jnp.mean(x2 * x2, axis=-1, keepdims=True)
        x2 = x2 * lax.rsqrt(ms + EPS) * fg_ref[...]
    o_ref[...] = x2


def _combine(slot3, x1, g2, ys, n_mod, final_g=None):
    rows = x1.shape[0]
    n_steps = rows // TD
    final = final_g is not None
    mod_idx = lambda i: (jnp.minimum((i * TD) // L, n_mod), 0, 0)
    in_specs = [_slot_spec(slot3, TD), _slot_spec(slot3, TD, ahead=1),
                pl.BlockSpec((TD, D), lambda i: (i, 0)),
                pl.BlockSpec((None, 1, D), mod_idx)]
    args = [slot3, slot3, x1, g2]
    if final:
        in_specs.append(pl.BlockSpec((1, D), lambda i: (0, 0)))
        args.append(final_g.reshape(1, D))
    in_specs.append(pl.BlockSpec(memory_space=pl.ANY))
    args.append(ys)
    return pl.pallas_call(
        functools.partial(_combine_kernel, final=final, n_steps=n_steps),
        grid=(n_steps,),
        in_specs=in_specs,
        out_specs=pl.BlockSpec((TD, D), lambda i: (i, 0)),
        out_shape=jax.ShapeDtypeStruct((rows, D), F32),
        scratch_shapes=[pltpu.VMEM((2, TD, D), F32), pltpu.SemaphoreType.DMA((2,))],
        compiler_params=_cp(("arbitrary",)),
        name="moe_combine_final" if final else "moe_combine",
    )(*args)


def _sort_plan(counts_row, rows):
    counts = counts_row[:N_GROUPS].astype(jnp.int32)
    padded = ((counts + TS - 1) // TS) * TS
    ends = jnp.cumsum(padded)
    starts = ends - padded
    starts_row = jnp.zeros((1, 128), F32).at[0, :N_GROUPS].set(starts.astype(F32))
    n_tiles = rows // TS + N_GROUPS
    n_active = (ends[-1] // TS).reshape(1)
    tile_start = jnp.arange(n_tiles, dtype=jnp.int32) * TS
    tile_group = jnp.minimum(jnp.sum(tile_start[:, None] >= ends[None, :], axis=1), N_GROUPS - 1).astype(jnp.int32)
    r = jnp.arange(TS, dtype=jnp.int32)
    pad_real = ((starts + counts)[:, None] + r[None, :]).reshape(-1)
    is_pad = pad_real < jnp.repeat(ends, TS)
    tail = ends[-1] + jnp.cumsum(jnp.logical_not(is_pad).astype(jnp.int32)) - 1
    pad_slots = jnp.where(is_pad, pad_real, tail)
    return starts_row, pad_slots, tile_group, n_active, n_tiles, n_tiles * TS


def _moe_sorted(h2x, route_t, cnt, w1, w3, w2, layer):
    rows = h2x.shape[0]
    starts_row, pad_slots, tile_group, n_active, n_tiles, n_sorted = _sort_plan(cnt[0], rows)
    slot3 = _slots(route_t, starts_row)
    xs = _dispatch(slot3, pad_slots, h2x, n_sorted)
    ys = _moe(tile_group, n_active, xs, w1, w3, w2, layer, n_tiles)
    return slot3, ys


def _router_weights(gw, gb, ew, eb):
    wr = jnp.zeros((D, 128), F32).at[:, :N_GROUPS].set(gw).at[:, 8:8 + N_GROUPS * EPG].set(ew)
    br = jnp.zeros((1, 128), F32).at[0, :N_GROUPS].set(gb).at[0, 8:8 + N_GROUPS * EPG].set(eb)
    hi = wr.astype(BF16)
    lo = (wr - hi.astype(F32)).astype(BF16)
    return jnp.concatenate([hi, lo], axis=1), br


def kernel(x, c, ctx, c_ctx, ada_w, ada_b, norm1_g, norm2_g, w_in_e, conv_w, na_rpb, w_out_e, w_in_o,
           ret_decay_f, ret_decay_b, w_out_o, router_gw, router_gb, router_ew, router_eb,
           moe_w1, moe_w3, moe_w2, final_g):
    nb = x.shape[0]
    assert x.shape[1:] == (L, D) and ctx.shape[1:] == (CTX, D) and nb < 16 and ada_w.shape[0] == 2
    t_lat = nb * L
    t_ctx = nb * CTX
    t_all = t_lat + t_ctx
    tm = 1024 if t_ctx % 1024 == 0 else 512

    xsrc = (x.reshape(t_lat, D), ctx.reshape(t_ctx, D))
    c16 = jnp.zeros((16, D), F32).at[:nb].set(c).at[nb].set(c_ctx)
    mod = _ada(c16, ada_w, ada_b).reshape(2, 16, 6, 1, D)
    mods = [[mod[l, :, k] for k in range(6)] for l in range(2)]

    sh1, sc1, g1, sh2, sc2, g2 = mods[0]
    qkv = _inproj(xsrc, norm1_g[0], sh1, sc1, w_in_e[0].astype(BF16), t_lat, nb, tm)
    y_conv = (_conv(qkv, conv_w[0], L, nb, 0), _conv(qkv, conv_w[0], CTX, nb, t_lat // CTX))
    y_attn = (_natten(qkv, _natten_tcol(na_rpb[0]), nb), _ctx_attn(qkv, nb))
    w_out = w_out_e[0].astype(BF16)
    wr2, br = _router_weights(router_gw[0], router_gb[0], router_ew[0], router_eb[0])
    x1, h2x, route_t, cnt = _outproj([y_conv, y_attn], [w_out[:CONV_CH], w_out[CONV_CH:]], xsrc, g1, sh2, sc2,
                                     norm2_g[0], wr2, br, t_all, t_lat, nb, tm)
    slot3, ys = _moe_sorted(h2x, route_t, cnt, moe_w1, moe_w3, moe_w2, 0)

    sh1, sc1, g1n, sh2n, sc2n, g2n = mods[1]
    qkvg, xa = _inproj_comb(slot3, ys, x1, g2, norm1_g[1], sh1, sc1, w_in_o[0].astype(BF16), t_lat, nb, tm,
                            _rope_tables(tm))
    g1, sh2, sc2, g2 = g1n, sh2n, sc2n, g2n
    lg = jnp.stack([jnp.log1p(-jnp.exp(ret_decay_f[0].astype(F32))),
                    jnp.log1p(-jnp.exp(ret_decay_b[0].astype(F32)))])
    y_ret = _retention(qkvg, lg, nb)
    wr2, br = _router_weights(router_gw[1], router_gb[1], router_ew[1], router_eb[1])
    x1, h2x, route_t, cnt = _outproj([(y_ret,)], [w_out_o[0].astype(BF16)], (xa,), g1, sh2, sc2, norm2_g[1],
                                     wr2, br, t_lat, t_lat, nb, tm)
    slot3, ys = _moe_sorted(h2x, route_t, cnt, moe_w1, moe_w3, moe_w2, 1)
    out = _combine(slot3, x1, g2, ys, nb, final_g)
    return out.reshape(nb, L, D)
```

```python
import functools
import math

import numpy as np
import jax
import jax.numpy as jnp
from jax import lax
from jax.experimental import pallas as pl
from jax.experimental.pallas import tpu as pltpu

F32 = jnp.float32
BF16 = jnp.bfloat16
HIGHEST = lax.Precision.HIGHEST

D = 1024
L = 4096
GRID_W = 64
GRID_H = L // GRID_W
CTX = 256
EPS = 1e-6

CONV_CH = 512
NA_HEADS = 8
NA_DH = 64
WIN_H = 8
WIN_W = 16
EVEN_IN = 3072

RET_HEADS = 4
RET_DK = 256
RET_DV = 512
RET_C = 256
ODD_IN = 6144
ROPE_BASE = 10000.0

N_GROUPS = 4
EPG = 8
EH = 256
GH = EPG * EH

NEG = -1e30

NA_QR = 4
NA_KR = NA_QR + WIN_H - 1
NA_NQ = NA_QR * GRID_W
NA_NK = NA_KR * GRID_W
NA_WS_MAX = GRID_H - NA_KR

DX = D + 128
TS = 512
TD = 512

VMEM_LIMIT = 56 * 1024 * 1024


def _cp(sem, vmem=VMEM_LIMIT):
    return pltpu.CompilerParams(dimension_semantics=sem, vmem_limit_bytes=vmem)


def _silu(v):
    return v * (1.0 / (1.0 + jnp.exp(-v)))


def _src_specs(src, tm, lat_tiles):
    tail = tuple(src[0].shape[1:])
    zeros = (0,) * len(tail)
    if len(src) == 1:
        return [pl.BlockSpec((tm,) + tail, lambda i, *_: (i,) + zeros)]
    return [pl.BlockSpec((tm,) + tail, lambda i, *_: (jnp.minimum(i, lat_tiles - 1),) + zeros),
            pl.BlockSpec((tm,) + tail, lambda i, *_: (jnp.maximum(i - lat_tiles, 0),) + zeros)]


def _src_load(refs, i, lat_tiles, rows=slice(None)):
    if len(refs) == 1:
        return refs[0][rows, :]
    return jnp.where(i < lat_tiles, refs[0][rows, :], refs[1][rows, :])


def _ada_kernel(c_ref, w_ref, b_ref, o_ref):
    s = _silu(c_ref[...])
    o_ref[...] = jnp.dot(s, w_ref[...], precision=HIGHEST, preferred_element_type=F32) + b_ref[...]


def _ada(c16, ada_w, ada_b):
    depth = ada_w.shape[0]
    tn = 1536
    return pl.pallas_call(
        _ada_kernel,
        grid=(depth, 6 * D // tn),
        in_specs=[
            pl.BlockSpec((16, D), lambda l, j: (0, 0)),
            pl.BlockSpec((None, D, tn), lambda l, j: (l, 0, j)),
            pl.BlockSpec((None, 1, tn), lambda l, j: (l, 0, j)),
        ],
        out_specs=pl.BlockSpec((None, 16, tn), lambda l, j: (l, 0, j)),
        out_shape=jax.ShapeDtypeStruct((depth, 16, 6 * D), F32),
        compiler_params=_cp(("arbitrary", "arbitrary")),
        name="ada_mod",
    )(c16, ada_w, ada_b.reshape(depth, 1, 6 * D))


def _inproj_kernel(*refs, rope, tn, n_src, lat_tiles):
    x_refs, refs = refs[:n_src], refs[n_src:]
    if rope:
        g_ref, sh_ref, sc_ref, w_ref, cos_ref, sin_ref, o_ref, h_scr = refs
    else:
        g_ref, sh_ref, sc_ref, w_ref, o_ref, h_scr = refs
    i = pl.program_id(0)
    j = pl.program_id(1)

    if not rope and tn == w_ref.shape[1] == o_ref.shape[1]:
        coef = g_ref[...] * (1.0 + sc_ref[...])
        rs = 256
        for sbk in range(o_ref.shape[0] // rs):
            rows = slice(sbk * rs, (sbk + 1) * rs)
            x = _src_load(x_refs, i, lat_tiles, rows)
            ms = jnp.mean(x * x, axis=-1, keepdims=True)
            h = (x * lax.rsqrt(ms + EPS) * coef + sh_ref[...]).astype(BF16)
            o_ref[rows, :] = jnp.dot(h, w_ref[...], preferred_element_type=F32).astype(o_ref.dtype)
        return

    @pl.when(j == 0)
    def _():
        x = _src_load(x_refs, i, lat_tiles)
        ms = jnp.mean(x * x, axis=-1, keepdims=True)
        y = x * lax.rsqrt(ms + EPS) * g_ref[...]
        h_scr[...] = (y * (1.0 + sc_ref[...]) + sh_ref[...]).astype(BF16)

    acc = jnp.dot(h_scr[...], w_ref[...], preferred_element_type=F32)
    if not rope:
        o_ref[...] = acc.astype(o_ref.dtype)
        return

    @pl.when(j < 2)
    def _():
        cos = cos_ref[...]
        sin = sin_ref[...]
        scale = jnp.where(j == 1, RET_DK ** -0.5, 1.0).astype(F32)
        pieces = []
        for cblk in range(tn // 128):
            t = acc[:, cblk * 128:(cblk + 1) * 128]
            half = (cblk % 2) * 128
            r = pltpu.roll(t, 64, axis=1)
            pieces.append(t * cos[:, half:half + 128] + r * sin[:, half:half + 128])
        o_ref[...] = (jnp.concatenate(pieces, axis=1) * scale).astype(o_ref.dtype)

    @pl.when(j >= 2)
    def _():
        o_ref[...] = acc.astype(o_ref.dtype)


def _inproj(xsrc, gain, sh, sc, w, n_lat_rows, n_mod, tm, rope_tables=None):
    rows = sum(a.shape[0] for a in xsrc)
    n = w.shape[1]
    tn = 1024 if rope_tables is not None else n
    lat_tiles = n_lat_rows // tm
    mod_idx = lambda i, j: (jnp.minimum((i * tm) // L, n_mod), 0, 0)
    in_specs = _src_specs(xsrc, tm, lat_tiles) + [
        pl.BlockSpec((1, D), lambda i, j: (0, 0)),
        pl.BlockSpec((None, 1, D), mod_idx),
        pl.BlockSpec((None, 1, D), mod_idx),
        pl.BlockSpec((D, tn), lambda i, j: (0, j)),
    ]
    args = list(xsrc) + [gain.reshape(1, D), sh, sc, w]
    rope = rope_tables is not None
    if rope:
        per_seq = L // tm
        tab_idx = lambda i, j: (jnp.where(i < lat_tiles, i % per_seq, per_seq), 0)
        in_specs += [pl.BlockSpec((tm, RET_DK), tab_idx), pl.BlockSpec((tm, RET_DK), tab_idx)]
        args += list(rope_tables)
    return pl.pallas_call(
        functools.partial(_inproj_kernel, rope=rope, tn=tn, n_src=len(xsrc), lat_tiles=lat_tiles),
        grid=(rows // tm, n // tn),
        in_specs=in_specs,
        out_specs=pl.BlockSpec((tm, tn), lambda i, j: (i, j)),
        out_shape=jax.ShapeDtypeStruct((rows, n), BF16),
        scratch_shapes=[pltpu.VMEM((tm, D), BF16)],
        compiler_params=_cp(("arbitrary", "arbitrary")),
        name="inproj_rope" if rope else "inproj",
    )(*args)


def _inproj_comb_kernel(slot_ref, slotn_ref, x1_ref, g2_ref, g_ref, sh_ref, sc_ref, w_ref, cos_ref, sin_ref, ys_ref,
                        o_ref, xa_ref, h_scr, buf, sem, *, tn, tm, n_tiles, n_j, ch):
    i = pl.program_id(0)
    j = pl.program_id(1)
    rows_buf = ch * n_j

    def start_row(sref, b, r, u):
        src = sref[0, jnp.minimum(r, tm - 1)]
        pltpu.make_async_copy(ys_ref.at[pl.ds(src, 1)], buf.at[b, pl.ds(r, 1)], sem.at[b]).start(priority=u)

    @pl.when(j == 0)
    def _():
        @pl.when(i == 0)
        def _():
            def body(jj, carry):
                for u in range(2):
                    start_row(slot_ref, 0, 2 * jj + u, u)
                return carry

            lax.fori_loop(0, rows_buf // 2, body, 0)

        b = i % 2
        pltpu.make_async_copy(ys_ref.at[pl.ds(0, rows_buf)], buf.at[b], sem.at[b]).wait()
        x = x1_ref[...] + g2_ref[...] * buf[b, :tm, :]
        xa_ref[...] = x
        ms = jnp.mean(x * x, axis=-1, keepdims=True)
        coef = g_ref[...] * (1.0 + sc_ref[...])
        h_scr[...] = (x * lax.rsqrt(ms + EPS) * coef + sh_ref[...]).astype(BF16)

    bn = (i + 1) % 2
    for u in range(ch):
        start_row(slotn_ref, bn, j * ch + u, u % 2)

    acc = jnp.dot(h_scr[...], w_ref[...], preferred_element_type=F32)

    qk_tiles = 2 * RET_HEADS * RET_DK // tn

    @pl.when(j < qk_tiles)
    def _():
        cos = cos_ref[...]
        sin = sin_ref[...]
        pieces = []
        for cblk in range(tn // 128):
            t = acc[:, cblk * 128:(cblk + 1) * 128]
            half = (cblk % 2) * 128
            r = pltpu.roll(t, 64, axis=1)
            is_k = j * tn + cblk * 128 >= RET_HEADS * RET_DK
            scale = jnp.where(is_k, RET_DK ** -0.5, 1.0).astype(F32)
            pieces.append((t * cos[:, half:half + 128] + r * sin[:, half:half + 128]) * scale)
        o_ref[...] = jnp.concatenate(pieces, axis=1).astype(o_ref.dtype)

    @pl.when(j >= qk_tiles)
    def _():
        o_ref[...] = acc.astype(o_ref.dtype)

    @pl.when((i == n_tiles - 1) & (j == n_j - 1))
    def _():
        pltpu.make_async_copy(ys_ref.at[pl.ds(0, rows_buf)], buf.at[bn], sem.at[bn]).wait()


def _inproj_comb(slot3, ys, x1, g2, gain, sh, sc, w, n_lat_rows, n_mod, tm, rope_tables):
    rows = x1.shape[0]
    n = w.shape[1]
    tn = 2048
    n_tiles = rows // tm
    lat_tiles = n_lat_rows // tm
    per_seq = L // tm
    mod_idx = lambda i, j: (jnp.minimum((i * tm) // L, n_mod), 0, 0)
    tab_idx = lambda i, j: (jnp.where(i < lat_tiles, i % per_seq, per_seq), 0)
    n_j = n // tn
    ch = -(-tm // n_j)
    while (ch * n_j) % 8:
        ch += 1
    in_specs = [_slot_spec(slot3, tm), _slot_spec(slot3, tm, ahead=1),
                pl.BlockSpec((tm, D), lambda i, j: (i, 0)),
                pl.BlockSpec((None, 1, D), mod_idx),
                pl.BlockSpec((1, D), lambda i, j: (0, 0)),
                pl.BlockSpec((None, 1, D), mod_idx),
                pl.BlockSpec((None, 1, D), mod_idx),
                pl.BlockSpec((D, tn), lambda i, j: (0, j)),
                pl.BlockSpec((tm, RET_DK), tab_idx),
                pl.BlockSpec((tm, RET_DK), tab_idx),
                pl.BlockSpec(memory_space=pl.ANY)]
    return pl.pallas_call(
        functools.partial(_inproj_comb_kernel, tn=tn, tm=tm, n_tiles=n_tiles, n_j=n_j, ch=ch),
        grid=(n_tiles, n_j),
        in_specs=in_specs,
        out_specs=[pl.BlockSpec((tm, tn), lambda i, j: (i, j)),
                   pl.BlockSpec((tm, D), lambda i, j: (i, 0))],
        out_shape=[jax.ShapeDtypeStruct((rows, n), BF16),
                   jax.ShapeDtypeStruct((rows, D), F32)],
        scratch_shapes=[pltpu.VMEM((tm, D), BF16), pltpu.VMEM((2, ch * n_j, D), F32),
                        pltpu.SemaphoreType.DMA((2,))],
        compiler_params=_cp(("arbitrary", "arbitrary")),
        name="inproj_rope_combine",
    )(slot3, slot3, x1, g2, gain.reshape(1, D), sh, sc, w, rope_tables[0], rope_tables[1], ys)


def _rope_tables(tm):
    pos = np.arange(L)
    half = RET_DK // 4
    freqs = ROPE_BASE ** (-np.arange(half, dtype=np.float64) / half)
    ang_r = (pos // GRID_W)[:, None] * freqs[None, :]
    ang_c = (pos % GRID_W)[:, None] * freqs[None, :]
    cos = np.concatenate([np.cos(ang_r), np.cos(ang_r), np.cos(ang_c), np.cos(ang_c)], axis=1)
    sin = np.concatenate([-np.sin(ang_r), np.sin(ang_r), -np.sin(ang_c), np.sin(ang_c)], axis=1)
    cos = np.concatenate([cos, np.ones((tm, RET_DK))], axis=0)
    sin = np.concatenate([sin, np.zeros((tm, RET_DK))], axis=0)
    return jnp.asarray(cos, F32), jnp.asarray(sin, F32)


def _conv_kernel(ab_ref, ac_ref, ax_ref, w_ref, o_ref, *, seq):
    u = ac_ref[...].astype(F32) * ax_ref[...].astype(F32)
    row = lax.broadcasted_iota(jnp.int32, u.shape, 0)
    up = jnp.where(row == 0, 0.0, pltpu.roll(u, 1, axis=0))
    un = jnp.where(row == seq - 1, 0.0, pltpu.roll(u, seq - 1, axis=0))
    w = w_ref[...]
    y = ab_ref[...].astype(F32) * (up * w[0:1, :] + u * w[1:2, :] + un * w[2:3, :])
    o_ref[...] = y.astype(o_ref.dtype)


def _conv(qkv, conv_w, seq, n_seq, row_blk0):
    nb = CONV_CH // 128
    return pl.pallas_call(
        functools.partial(_conv_kernel, seq=seq),
        grid=(n_seq, nb),
        in_specs=[
            pl.BlockSpec((seq, 128), lambda b, j: (row_blk0 + b, j)),
            pl.BlockSpec((seq, 128), lambda b, j: (row_blk0 + b, nb + j)),
            pl.BlockSpec((seq, 128), lambda b, j: (row_blk0 + b, 2 * nb + j)),
            pl.BlockSpec((3, 128), lambda b, j: (0, j)),
        ],
        out_specs=pl.BlockSpec((seq, 128), lambda b, j: (b, j)),
        out_shape=jax.ShapeDtypeStruct((n_seq * seq, CONV_CH), BF16),
        compiler_params=_cp(("arbitrary", "arbitrary")),
        name="short_conv",
    )(qkv, qkv, qkv, conv_w)


def _natten_tcol(rpb):
    qc = np.arange(GRID_W)
    kc = np.arange(GRID_W)
    cs = np.clip(qc - WIN_W // 2, 0, GRID_W - WIN_W)
    col_ok = (kc[None, :] >= cs[:, None]) & (kc[None, :] < cs[:, None] + WIN_W)
    dx = np.clip(kc[None, :] - qc[:, None] + WIN_W - 1, 0, 2 * WIN_W - 2)
    tcol = jnp.where(col_ok[None, None], rpb.astype(F32)[:, :, dx], NEG)
    return tcol.reshape(NA_HEADS // 2, 2, 2 * WIN_H - 1, GRID_W, GRID_W)


def _natten_row_classes():
    table = []
    for r0 in (0, 2 * NA_QR, GRID_H - NA_QR):
        ws = int(np.clip(r0 - WIN_H // 2, 0, NA_WS_MAX))
        rows = []
        for q in range(NA_QR):
            qr = r0 + q
            rs = int(np.clip(qr - WIN_H // 2, 0, GRID_H - WIN_H))
            rows.append([(ws + k) - qr + WIN_H - 1 if rs <= ws + k < rs + WIN_H else None for k in range(NA_KR)])
        table.append(rows)
    return table


def _attend(qa, kw, vw, kc, vc, bias):
    dn = (((1,), (1,)), ((), ()))
    s2 = lax.dot_general(qa, kc, dn, preferred_element_type=F32)
    m = jnp.max(s2, axis=1, keepdims=True)
    if kw is not None:
        s1 = lax.dot_general(qa, kw, dn, preferred_element_type=F32) + bias
        m = jnp.maximum(m, jnp.max(s1, axis=1, keepdims=True))
        p1 = jnp.exp(s1 - m)
    p2 = jnp.exp(s2 - m)
    den = jnp.sum(p2, axis=1, keepdims=True)
    o = jnp.dot(p2.astype(BF16), vc, preferred_element_type=F32)
    if kw is not None:
        den = den + jnp.sum(p1, axis=1, keepdims=True)
        o = o + jnp.dot(p1.astype(BF16), vw, preferred_element_type=F32)
    return o / den


NA_SUB = 8


def _natten_kernel(q_ref, k_ref, v_ref, kc_ref, vc_ref, tcol_ref, o_ref, bias_ref):
    n_rb = GRID_H // NA_QR

    @pl.when((pl.program_id(1) == 0) & (pl.program_id(2) == 0))
    def _():
        neg = jnp.full((GRID_W, GRID_W), NEG, F32)
        for cls, rows in enumerate(_natten_row_classes()):
            for a in range(2):
                for qr, dys in enumerate(rows):
                    blocks = [neg if dy is None else tcol_ref[a, dy] for dy in dys]
                    bias_ref[cls, a, qr * GRID_W:(qr + 1) * GRID_W, :] = jnp.concatenate(blocks, axis=1)

    lane = lax.broadcasted_iota(jnp.int32, (NA_NQ, 128), 1)
    for sb in range(NA_SUB):
        rb = pl.program_id(2) * NA_SUB + sb
        ws = jnp.clip(rb * NA_QR - WIN_H // 2, 0, NA_WS_MAX)
        start = pl.multiple_of(ws * GRID_W, GRID_W)
        kw = k_ref[pl.ds(start, NA_NK), :]
        vw = v_ref[pl.ds(start, NA_NK), :]
        cls = jnp.where(rb == 0, 0, jnp.where(rb == n_rb - 1, 2, 1))
        q = q_ref[sb * NA_NQ:(sb + 1) * NA_NQ, :] * (NA_DH ** -0.5)
        zero = jnp.zeros_like(q)
        o0 = _attend(jnp.where(lane < NA_DH, q, zero), kw, vw, kc_ref[...], vc_ref[...], bias_ref[cls, 0])
        o1 = _attend(jnp.where(lane >= NA_DH, q, zero), kw, vw, kc_ref[...], vc_ref[...], bias_ref[cls, 1])
        o_ref[sb * NA_NQ:(sb + 1) * NA_NQ, :] = jnp.where(lane < NA_DH, o0, o1).astype(o_ref.dtype)


def _natten(qkv, tcol, nb):
    n_steps = GRID_H // NA_QR // NA_SUB
    qcol, kcol, vcol = 3 * CONV_CH // 128, 3 * CONV_CH // 128 + 4, 3 * CONV_CH // 128 + 8
    ctx_blk0 = nb * L // CTX
    return pl.pallas_call(
        _natten_kernel,
        grid=(NA_HEADS // 2, nb, n_steps),
        in_specs=[
            pl.BlockSpec((NA_SUB * NA_NQ, 128), lambda h, b, r: (b * n_steps + r, qcol + h)),
            pl.BlockSpec((L, 128), lambda h, b, r: (b, kcol + h)),
            pl.BlockSpec((L, 128), lambda h, b, r: (b, vcol + h)),
            pl.BlockSpec((CTX, 128), lambda h, b, r: (ctx_blk0 + b, kcol + h)),
            pl.BlockSpec((CTX, 128), lambda h, b, r: (ctx_blk0 + b, vcol + h)),
            pl.BlockSpec((None, 2, 2 * WIN_H - 1, GRID_W, GRID_W), lambda h, b, r: (h, 0, 0, 0, 0)),
        ],
        out_specs=pl.BlockSpec((NA_SUB * NA_NQ, 128), lambda h, b, r: (b * n_steps + r, h)),
        out_shape=jax.ShapeDtypeStruct((nb * L, NA_HEADS * NA_DH), BF16),
        scratch_shapes=[pltpu.VMEM((3, 2, NA_NQ, NA_NK), F32)],
        compiler_params=_cp(("arbitrary", "arbitrary", "arbitrary")),
        name="natten",
    )(qkv, qkv, qkv, qkv, qkv, tcol)


def _ctx_attn_kernel(q_ref, kc_ref, vc_ref, o_ref):
    q = q_ref[...] * (NA_DH ** -0.5)
    lane = lax.broadcasted_iota(jnp.int32, q.shape, 1)
    zero = jnp.zeros_like(q)
    o0 = _attend(jnp.where(lane < NA_DH, q, zero), None, None, kc_ref[...], vc_ref[...], None)
    o1 = _attend(jnp.where(lane >= NA_DH, q, zero), None, None, kc_ref[...], vc_ref[...], None)
    o_ref[...] = jnp.where(lane < NA_DH, o0, o1).astype(o_ref.dtype)


def _ctx_attn(qkv, nb):
    qcol, kcol, vcol = 3 * CONV_CH // 128, 3 * CONV_CH // 128 + 4, 3 * CONV_CH // 128 + 8
    ctx_blk0 = nb * L // CTX
    return pl.pallas_call(
        _ctx_attn_kernel,
        grid=(nb, NA_HEADS // 2),
        in_specs=[
            pl.BlockSpec((CTX, 128), lambda b, h: (ctx_blk0 + b, qcol + h)),
            pl.BlockSpec((CTX, 128), lambda b, h: (ctx_blk0 + b, kcol + h)),
            pl.BlockSpec((CTX, 128), lambda b, h: (ctx_blk0 + b, vcol + h)),
        ],
        out_specs=pl.BlockSpec((CTX, 128), lambda b, h: (b, h)),
        out_shape=jax.ShapeDtypeStruct((nb * CTX, NA_HEADS * NA_DH), BF16),
        compiler_params=_cp(("arbitrary", "arbitrary")),
        name="ctx_attn",
    )(qkv, qkv, qkv)


def _retention_kernel(lg_ref, q_ref, k_ref, v_ref, g_ref, kc_ref, vc_ref, o_ref, y_scr, stf_scr, stb_scr):
    h = pl.program_id(1)
    n_chunks = L // RET_C
    pos_r = lax.broadcasted_iota(jnp.int32, (RET_C, 1), 0).astype(F32)
    rel = pos_r - lax.broadcasted_iota(jnp.int32, (1, RET_C), 1).astype(F32)
    dn_t = (((1,), (1,)), ((), ()))
    dn_k = (((0,), (0,)), ((), ()))

    def decays(lg, fwd):
        if fwd:
            d_intra = jnp.where(rel >= 0, jnp.exp(lg * jnp.maximum(rel, 0.0)), 0.0)
            d_q = jnp.exp(lg * (pos_r + 1.0))
            d_k = jnp.exp(lg * (RET_C - 1.0 - pos_r))
        else:
            d_intra = jnp.where(rel <= 0, jnp.exp(lg * jnp.maximum(-rel, 0.0)), 0.0)
            d_q = jnp.exp(lg * (RET_C - pos_r))
            d_k = jnp.exp(lg * pos_r)
        d_c = jnp.exp(jnp.full((1, 1), RET_C, F32) * lg)
        return d_intra, d_q, d_k, d_c

    def ctx_state(st_scr, d_k, d_c, order):
        st_scr[...] = jnp.zeros_like(st_scr)
        for jc in order:
            kd = (kc_ref[jc * RET_C:(jc + 1) * RET_C, :].astype(F32) * d_k).astype(BF16)
            vc = vc_ref[jc * RET_C:(jc + 1) * RET_C, :]
            st_scr[...] = st_scr[...] * d_c + lax.dot_general(kd, vc, dn_k, preferred_element_type=F32)

    def chunk_out(st_scr, c, d_intra, d_q, d_k, d_c):
        r0 = pl.multiple_of(c * RET_C, RET_C)
        q = q_ref[pl.ds(r0, RET_C), :]
        k = k_ref[pl.ds(r0, RET_C), :]
        v = v_ref[pl.ds(r0, RET_C), :]
        s = lax.dot_general(q, k, dn_t, preferred_element_type=F32) * d_intra
        st = st_scr[...]
        o = (jnp.dot(s.astype(BF16), v, preferred_element_type=F32)
             + jnp.dot(q, st.astype(BF16), preferred_element_type=F32) * d_q)
        kd = (k.astype(F32) * d_k).astype(BF16)
        st_scr[...] = st * d_c + lax.dot_general(kd, v, dn_k, preferred_element_type=F32)
        return r0, o

    def finish(r0, y):
        y = y * lax.rsqrt(jnp.mean(y * y, axis=-1, keepdims=True) + EPS)
        g = g_ref[pl.ds(r0, RET_C), :].astype(F32)
        o_ref[pl.ds(r0, RET_C), :] = (_silu(g) * y).astype(o_ref.dtype)

    dec_f = decays(lg_ref[0, h], True)
    dec_b = decays(lg_ref[1, h], False)
    ctx_state(stf_scr, dec_f[2], dec_f[3], range(CTX // RET_C))
    ctx_state(stb_scr, dec_b[2], dec_b[3], range(CTX // RET_C - 1, -1, -1))

    def first_half(i, carry):
        r0, o = chunk_out(stf_scr, i, *dec_f)
        y_scr[pl.ds(r0, RET_C), :] = o
        r1, o = chunk_out(stb_scr, n_chunks - 1 - i, *dec_b)
        y_scr[pl.ds(r1, RET_C), :] = o
        return carry

    lax.fori_loop(0, n_chunks // 2, first_half, 0)

    def second_half(i, carry):
        r0, o = chunk_out(stf_scr, i, *dec_f)
        finish(r0, y_scr[pl.ds(r0, RET_C), :] + o)
        r1, o = chunk_out(stb_scr, n_chunks - 1 - i, *dec_b)
        finish(r1, y_scr[pl.ds(r1, RET_C), :] + o)
        return carry

    lax.fori_loop(n_chunks // 2, n_chunks, second_half, 0)


def _retention(qkvg, lg, nb):
    ctx_blk0 = nb * L // CTX
    kq = RET_HEADS
    vq = 2 * RET_HEADS * RET_DK // RET_DV
    gq = vq + RET_HEADS
    return pl.pallas_call(
        _retention_kernel,
        grid=(nb, RET_HEADS),
        in_specs=[
            pl.BlockSpec(memory_space=pltpu.SMEM),
            pl.BlockSpec((L, RET_DK), lambda b, h: (b, h)),
            pl.BlockSpec((L, RET_DK), lambda b, h: (b, kq + h)),
            pl.BlockSpec((L, RET_DV), lambda b, h: (b, vq + h)),
            pl.BlockSpec((L, RET_DV), lambda b, h: (b, gq + h)),
            pl.BlockSpec((CTX, RET_DK), lambda b, h: (ctx_blk0 + b, kq + h)),
            pl.BlockSpec((CTX, RET_DV), lambda b, h: (ctx_blk0 + b, vq + h)),
        ],
        out_specs=pl.BlockSpec((L, RET_DV), lambda b, h: (b, h)),
        out_shape=jax.ShapeDtypeStruct((nb * L, RET_HEADS * RET_DV), BF16),
        scratch_shapes=[pltpu.VMEM((L, RET_DV), F32), pltpu.VMEM((RET_DK, RET_DV), F32),
                        pltpu.VMEM((RET_DK, RET_DV), F32)],
        compiler_params=_cp(("arbitrary", "arbitrary")),
        name="retention",
    )(lg, qkvg, qkvg, qkvg, qkvg, qkvg, qkvg)


def _outproj_kernel(*refs, src_lens, lat_tiles):
    i = pl.program_id(0)
    srcs = []
    for n in src_lens:
        srcs.append(refs[:n])
        refs = refs[n:]
    n_act = len(src_lens) - 1
    w_refs, refs = refs[:n_act], refs[n_act:]
    (g1_ref, sh2_ref, sc2_ref, n2_ref, wr_ref, br_ref, tri_ref,
     x1_ref, h2x_ref, rt_ref, cnt_ref, carry_scr) = refs

    @pl.when(i == 0)
    def _():
        carry_scr[...] = jnp.zeros_like(carry_scr)

    coef = n2_ref[...] * (1.0 + sc2_ref[...])
    rs = tri_ref.shape[0]
    for sbk in range(x1_ref.shape[0] // rs):
        rows = slice(sbk * rs, (sbk + 1) * rs)
        y = None
        for a_src, w_ref in zip(srcs[:n_act], w_refs):
            t = jnp.dot(_src_load(a_src, i, lat_tiles, rows), w_ref[...], preferred_element_type=F32)
            y = t if y is None else y + t
        x1 = _src_load(srcs[n_act], i, lat_tiles, rows) + g1_ref[...] * y
        x1_ref[rows, :] = x1
        ms = jnp.mean(x1 * x1, axis=-1, keepdims=True)
        h2 = x1 * lax.rsqrt(ms + EPS) * coef + sh2_ref[...]
        h2x_ref[rows, :D] = h2
        h_hi = h2.astype(BF16)
        h_lo = (h2 - h_hi.astype(F32)).astype(BF16)
        p = jnp.dot(h_hi, wr_ref[...], preferred_element_type=F32)
        q = jnp.dot(h_lo, wr_ref[:, :128], preferred_element_type=F32)
        route = _route(p[:, :128] + p[:, 128:] + q + br_ref[...], tri_ref, carry_scr)
        h2x_ref[rows, D:] = route
        rt_ref[:, rows] = route.T[8:16, :]
    cnt_ref[...] = carry_scr[...]


def _outproj(acts, ws, xsrc, g1, sh2, sc2, n2g, wr2, br, rows, n_lat_rows, n_mod, tm):
    lat_tiles = n_lat_rows // tm
    rs = tm
    tri = jnp.asarray(np.tril(np.ones((rs, rs), np.float32), -1), BF16)
    mod_idx = lambda i: (jnp.minimum((i * tm) // L, n_mod), 0, 0)
    in_specs, args = [], []
    for src in list(acts) + [xsrc]:
        in_specs += _src_specs(src, tm, lat_tiles)
        args += list(src)
    in_specs += [pl.BlockSpec(w.shape, lambda i: (0, 0)) for w in ws]
    in_specs += [pl.BlockSpec((None, 1, D), mod_idx),
                 pl.BlockSpec((None, 1, D), mod_idx),
                 pl.BlockSpec((None, 1, D), mod_idx),
                 pl.BlockSpec((1, D), lambda i: (0, 0)),
                 pl.BlockSpec((D, 256), lambda i: (0, 0)),
                 pl.BlockSpec((1, 128), lambda i: (0, 0)),
                 pl.BlockSpec((rs, rs), lambda i: (0, 0))]
    args += list(ws) + [g1, sh2, sc2, n2g.reshape(1, D), wr2, br, tri]
    return pl.pallas_call(
        functools.partial(_outproj_kernel, src_lens=tuple(len(s) for s in list(acts) + [xsrc]),
                          lat_tiles=lat_tiles),
        grid=(rows // tm,),
        in_specs=in_specs,
        out_specs=[pl.BlockSpec((tm, D), lambda i: (i, 0)),
                   pl.BlockSpec((tm, DX), lambda i: (i, 0)),
                   pl.BlockSpec((None, 8, tm), lambda i: (i, 0, 0)),
                   pl.BlockSpec((8, 128), lambda i: (0, 0))],
        out_shape=[jax.ShapeDtypeStruct((rows, D), F32),
                   jax.ShapeDtypeStruct((rows, DX), F32),
                   jax.ShapeDtypeStruct((rows // tm, 8, tm), F32),
                   jax.ShapeDtypeStruct((8, 128), F32)],
        scratch_shapes=[pltpu.VMEM((8, 128), F32)],
        compiler_params=_cp(("arbitrary",)),
        name="outproj_router",
    )(*args)


def _route(lg, tri_ref, carry_scr):
    lane = lax.broadcasted_iota(jnp.int32, lg.shape, 1)
    lane_f = lane.astype(F32)
    big = jnp.float32(1e9)

    gmask = lane < N_GROUPS
    gl = jnp.where(gmask, lg, NEG)
    gm = jnp.max(gl, axis=1, keepdims=True)
    gidx = jnp.min(jnp.where(gl == gm, lane_f, big), axis=1, keepdims=True)
    g_val = 1.0 / jnp.sum(jnp.where(gmask, jnp.exp(gl - gm), 0.0), axis=1, keepdims=True)

    base = 8.0 + 8.0 * gidx
    emask = (lane_f >= base) & (lane_f < base + EPG)
    el = jnp.where(emask, lg, NEG)
    m1 = jnp.max(el, axis=1, keepdims=True)
    i1 = jnp.min(jnp.where(el == m1, lane_f, big), axis=1, keepdims=True)
    el2 = jnp.where(lane_f == i1, NEG, el)
    m2 = jnp.max(el2, axis=1, keepdims=True)
    i2 = jnp.min(jnp.where(el2 == m2, lane_f, big), axis=1, keepdims=True)
    r = jnp.exp(m2 - m1)
    w1 = g_val / (1.0 + r)
    w2 = w1 * r
    gates = jnp.where(lane_f == i1 - base, w1, jnp.where(lane_f == i2 - base, w2, 0.0))

    onehot = jnp.where(lane_f == gidx, 1.0, 0.0)
    before = jnp.dot(tri_ref[...], onehot.astype(BF16), preferred_element_type=F32)
    carry = carry_scr[0:1, :]
    rank = jnp.sum(jnp.where(lane_f == gidx, before + carry, 0.0), axis=1, keepdims=True)
    carry_scr[0:1, :] = carry + jnp.sum(onehot, axis=0, keepdims=True)

    return jnp.where(lane < EPG, gates, jnp.where(lane == 8, gidx, jnp.where(lane == 9, rank, 0.0)))


def _slot_kernel(rt_ref, st_ref, o_ref):
    g = rt_ref[0:1, :]
    slot = rt_ref[1:2, :]
    st = st_ref[...]
    for k in range(N_GROUPS):
        slot = slot + jnp.where(g == k, st[:, k:k + 1], 0.0)
    o_ref[...] = slot.astype(jnp.int32)


def _slots(route_t, starts_row):
    n, _, st = route_t.shape
    return pl.pallas_call(
        _slot_kernel,
        grid=(n,),
        in_specs=[pl.BlockSpec((None, 8, st), lambda i: (i, 0, 0)),
                  pl.BlockSpec((1, 128), lambda i: (0, 0))],
        out_specs=pl.BlockSpec((None, 1, st), lambda i: (i, 0, 0)),
        out_shape=jax.ShapeDtypeStruct((n, 1, st), jnp.int32),
        compiler_params=_cp(("arbitrary",)),
        name="moe_slots",
    )(route_t, starts_row)


def _slot_spec(slot3, tile, ahead=0):
    n, _, st = slot3.shape
    per = st // tile
    last = n * per - 1

    def idx(i, *_):
        t = jnp.minimum(i + ahead, last)
        return (t // per, 0, t % per)

    return pl.BlockSpec((None, 1, tile), idx, memory_space=pltpu.SMEM)


def _dispatch_kernel(slot_ref, pad_ref, h2x_ref, xs_ref, zx_scr, sem):
    i = pl.program_id(0)
    n_pad = pad_ref.shape[1]

    @pl.when(i == 0)
    def _():
        zx_scr[...] = jnp.zeros_like(zx_scr)

        def zbody(r, carry):
            pltpu.make_async_copy(zx_scr.at[pl.ds(0, 1)], xs_ref.at[pl.ds(pad_ref[0, r], 1)], sem.at[0]).start()
            return carry

        lax.fori_loop(0, n_pad, zbody, 0)
        pltpu.make_async_copy(xs_ref.at[pl.ds(0, n_pad)], xs_ref.at[pl.ds(0, n_pad)], sem.at[0]).wait()

    def body(j, carry):
        for u in range(2):
            r = 2 * j + u
            pltpu.make_async_copy(h2x_ref.at[pl.ds(r, 1)], xs_ref.at[pl.ds(slot_ref[0, r], 1)],
                                  sem.at[0]).start(priority=u)
        return carry

    td = h2x_ref.shape[0]
    lax.fori_loop(0, td // 2, body, 0)
    pltpu.make_async_copy(h2x_ref, xs_ref.at[pl.ds(0, td)], sem.at[0]).wait()


def _dispatch(slot3, pad_slots, h2x, n_sorted):
    rows = h2x.shape[0]
    n_pad = pad_slots.shape[0]
    td = slot3.shape[2]
    return pl.pallas_call(
        _dispatch_kernel,
        grid=(rows // td,),
        in_specs=[_slot_spec(slot3, td),
                  pl.BlockSpec((1, n_pad), lambda i: (0, 0), memory_space=pltpu.SMEM),
                  pl.BlockSpec((td, DX), lambda i: (i, 0))],
        out_specs=pl.BlockSpec(memory_space=pl.ANY),
        out_shape=jax.ShapeDtypeStruct((n_sorted, DX), F32),
        scratch_shapes=[pltpu.VMEM((8, DX), F32), pltpu.SemaphoreType.DMA((1,))],
        compiler_params=_cp(("arbitrary",)),
        name="moe_dispatch",
    )(slot3, pad_slots.reshape(1, n_pad), h2x)


def _moe_kernel(tg_ref, na_ref, x_ref, w1_ref, w3_ref, w2_ref, o_ref, hid_scr, w1b_scr, w3b_scr, w2b_scr):
    i = pl.program_id(0)

    @pl.when((i == 0) | (tg_ref[i] != tg_ref[jnp.maximum(i - 1, 0)]))
    def _():
        for e in range(EPG):
            cols = slice(e * EH, (e + 1) * EH)
            w1b_scr[:, cols] = w1_ref[:, cols].astype(BF16)
            w3b_scr[:, cols] = w3_ref[:, cols].astype(BF16)
            w2b_scr[cols, :] = w2_ref[cols, :].astype(BF16)

    @pl.when(i < na_ref[0])
    def _():
        x = x_ref[:, :D].astype(BF16)
        gt = x_ref[:, D:]
        for e in range(EPG):
            cols = slice(e * EH, (e + 1) * EH)
            h1 = jnp.dot(x, w1b_scr[:, cols], preferred_element_type=F32)
            h3 = jnp.dot(x, w3b_scr[:, cols], preferred_element_type=F32)
            hid_scr[:, cols] = (_silu(h1) * h3 * gt[:, e:e + 1]).astype(BF16)
        o_ref[...] = jnp.dot(hid_scr[...], w2b_scr[...], preferred_element_type=F32)

    @pl.when(i >= na_ref[0])
    def _():
        o_ref[...] = jnp.zeros_like(o_ref)


def _moe(tile_group, n_active, xs, w1, w3, w2, layer, n_tiles):
    x_idx = lambda i, tg, na: (jnp.minimum(i, na[0] - 1), 0)
    out_idx = lambda i, tg, na: (i, 0)
    w_idx = lambda i, tg, na: (layer, tg[i], 0, 0)
    once = pl.Buffered(1)
    grid_spec = pltpu.PrefetchScalarGridSpec(
        num_scalar_prefetch=2,
        grid=(n_tiles,),
        in_specs=[pl.BlockSpec((TS, DX), x_idx),
                  pl.BlockSpec((None, None, D, GH), w_idx, pipeline_mode=once),
                  pl.BlockSpec((None, None, D, GH), w_idx, pipeline_mode=once),
                  pl.BlockSpec((None, None, GH, D), w_idx, pipeline_mode=once)],
        out_specs=pl.BlockSpec((TS, D), out_idx),
        scratch_shapes=[pltpu.VMEM((TS, GH), BF16), pltpu.VMEM((D, GH), BF16), pltpu.VMEM((D, GH), BF16),
                        pltpu.VMEM((GH, D), BF16)],
    )
    return pl.pallas_call(
        _moe_kernel,
        grid_spec=grid_spec,
        out_shape=jax.ShapeDtypeStruct((xs.shape[0], D), F32),
        compiler_params=_cp(("arbitrary",)),
        name="moe_experts",
    )(tile_group, n_active, xs, w1, w3, w2)


def _combine_kernel(*refs, final, n_steps):
    if final:
        slot_ref, slot_next_ref, x1_ref, g2_ref, fg_ref, ys_ref, o_ref, buf, sem = refs
    else:
        slot_ref, slot_next_ref, x1_ref, g2_ref, ys_ref, o_ref, buf, sem = refs
    i = pl.program_id(0)

    def start_row(sref, b, r, u):
        pltpu.make_async_copy(ys_ref.at[pl.ds(sref[0, r], 1)], buf.at[b, pl.ds(r, 1)], sem.at[b]).start(priority=u)

    @pl.when(i == 0)
    def _():
        def body(j, carry):
            for u in range(2):
                start_row(slot_ref, 0, 2 * j + u, u)
            return carry

        lax.fori_loop(0, TD // 2, body, 0)

    b = i % 2
    bn = (i + 1) % 2
    pltpu.make_async_copy(ys_ref.at[pl.ds(0, TD)], buf.at[b], sem.at[b]).wait()

    for r in range(TD):
        start_row(slot_next_ref, bn, r, r % 2)

    x2 = x1_ref[...] + g2_ref[...] * buf[b]
    if final:
        ms = jnp.mean(x2 * x2, axis=-1, keepdims=True)
        x2 = x2 * lax.rsqrt(ms + EPS) * fg_ref[...]
    o_ref[...] = x2

    @pl.when(i == n_steps - 1)
    def _():
        pltpu.make_async_copy(ys_ref.at[pl.ds(0, TD)], buf.at[bn], sem.at[bn]).wait()


def _combine(slot3, x1, g2, ys, n_mod, final_g=None):
    rows = x1.shape[0]
    n_steps = rows // TD
    final = final_g is not None
    mod_idx = lambda i: (jnp.minimum((i * TD) // L, n_mod), 0, 0)
    in_specs = [_slot_spec(slot3, TD), _slot_spec(slot3, TD, ahead=1),
                pl.BlockSpec((TD, D), lambda i: (i, 0)),
                pl.BlockSpec((None, 1, D), mod_idx)]
    args = [slot3, slot3, x1, g2]
    if final:
        in_specs.append(pl.BlockSpec((1, D), lambda i: (0, 0)))
        args.append(final_g.reshape(1, D))
    in_specs.append(pl.BlockSpec(memory_space=pl.ANY))
    args.append(ys)
    return pl.pallas_call(
        functools.partial(_combine_kernel, final=final, n_steps=n_steps),
        grid=(n_steps,),
        in_specs=in_specs,
        out_specs=pl.BlockSpec((TD, D), lambda i: (i, 0)),
        out_shape=jax.ShapeDtypeStruct((rows, D), F32),
        scratch_shapes=[pltpu.VMEM((2, TD, D), F32), pltpu.SemaphoreType.DMA((2,))],
        compiler_params=_cp(("arbitrary",)),
        name="moe_combine_final" if final else "moe_combine",
    )(*args)


def _sort_plan(counts_row, rows):
    counts = counts_row[:N_GROUPS].astype(jnp.int32)
    padded = ((counts + TS - 1) // TS) * TS
    ends = jnp.cumsum(padded)
    starts = ends - padded
    starts_row = jnp.zeros((1, 128), F32).at[0, :N_GROUPS].set(starts.astype(F32))
    n_tiles = rows // TS + N_GROUPS
    n_active = (ends[-1] // TS).reshape(1)
    tile_start = jnp.arange(n_tiles, dtype=jnp.int32) * TS
    tile_group = jnp.minimum(jnp.sum(tile_start[:, None] >= ends[None, :], axis=1), N_GROUPS - 1).astype(jnp.int32)
    r = jnp.arange(TS, dtype=jnp.int32)
    pad_real = ((starts + counts)[:, None] + r[None, :]).reshape(-1)
    is_pad = pad_real < jnp.repeat(ends, TS)
    tail = ends[-1] + jnp.cumsum(jnp.logical_not(is_pad).astype(jnp.int32)) - 1
    pad_slots = jnp.where(is_pad, pad_real, tail)
    return starts_row, pad_slots, tile_group, n_active, n_tiles, n_tiles * TS


def _moe_sorted(h2x, route_t, cnt, w1, w3, w2, layer):
    rows = h2x.shape[0]
    starts_row, pad_slots, tile_group, n_active, n_tiles, n_sorted = _sort_plan(cnt[0], rows)
    slot3 = _slots(route_t, starts_row)
    xs = _dispatch(slot3, pad_slots, h2x, n_sorted)
    ys = _moe(tile_group, n_active, xs, w1, w3, w2, layer, n_tiles)
    return slot3, ys


def _router_weights(gw, gb, ew, eb):
    wr = jnp.zeros((D, 128), F32).at[:, :N_GROUPS].set(gw).at[:, 8:8 + N_GROUPS * EPG].set(ew)
    br = jnp.zeros((1, 128), F32).at[0, :N_GROUPS].set(gb).at[0, 8:8 + N_GROUPS * EPG].set(eb)
    hi = wr.astype(BF16)
    lo = (wr - hi.astype(F32)).astype(BF16)
    return jnp.concatenate([hi, lo], axis=1), br


def kernel(x, c, ctx, c_ctx, ada_w, ada_b, norm1_g, norm2_g, w_in_e, conv_w, na_rpb, w_out_e, w_in_o,
           ret_decay_f, ret_decay_b, w_out_o, router_gw, router_gb, router_ew, router_eb,
           moe_w1, moe_w3, moe_w2, final_g):
    nb = x.shape[0]
    assert x.shape[1:] == (L, D) and ctx.shape[1:] == (CTX, D) and nb < 16 and ada_w.shape[0] == 2
    t_lat = nb * L
    t_ctx = nb * CTX
    t_all = t_lat + t_ctx
    tm = 1024 if t_ctx % 1024 == 0 else 512

    xsrc = (x.reshape(t_lat, D), ctx.reshape(t_ctx, D))
    c16 = jnp.zeros((16, D), F32).at[:nb].set(c).at[nb].set(c_ctx)
    mod = _ada(c16, ada_w, ada_b).reshape(2, 16, 6, 1, D)
    mods = [[mod[l, :, k] for k in range(6)] for l in range(2)]

    sh1, sc1, g1, sh2, sc2, g2 = mods[0]
    qkv = _inproj(xsrc, norm1_g[0], sh1, sc1, w_in_e[0].astype(BF16), t_lat, nb, tm)
    y_conv = (_conv(qkv, conv_w[0], L, nb, 0), _conv(qkv, conv_w[0], CTX, nb, t_lat // CTX))
    y_attn = (_natten(qkv, _natten_tcol(na_rpb[0]), nb), _ctx_attn(qkv, nb))
    w_out = w_out_e[0].astype(BF16)
    wr2, br = _router_weights(router_gw[0], router_gb[0], router_ew[0], router_eb[0])
    x1, h2x, route_t, cnt = _outproj([y_conv, y_attn], [w_out[:CONV_CH], w_out[CONV_CH:]], xsrc, g1, sh2, sc2,
                                     norm2_g[0], wr2, br, t_all, t_lat, nb, tm)
    slot3, ys = _moe_sorted(h2x, route_t, cnt, moe_w1, moe_w3, moe_w2, 0)

    sh1, sc1, g1n, sh2n, sc2n, g2n = mods[1]
    qkvg, xa = _inproj_comb(slot3, ys, x1, g2, norm1_g[1], sh1, sc1, w_in_o[0].astype(BF16), t_lat, nb, tm,
                            _rope_tables(tm))
    g1, sh2, sc2, g2 = g1n, sh2n, sc2n, g2n
    lg = jnp.stack([jnp.log1p(-jnp.exp(ret_decay_f[0].astype(F32))),
                    jnp.log1p(-jnp.exp(ret_decay_b[0].astype(F32)))])
    y_ret = _retention(qkvg, lg, nb)
    wr2, br = _router_weights(router_gw[1], router_gb[1], router_ew[1], router_eb[1])
    x1, h2x, route_t, cnt = _outproj([(y_ret,)], [w_out_o[0].astype(BF16)], (xa,), g1, sh2, sc2, norm2_g[1],
                                     wr2, br, t_lat, t_lat, nb, tm)
    slot3, ys = _moe_sorted(h2x, route_t, cnt, moe_w1, moe_w3, moe_w2, 1)
    out = _combine(slot3, x1, g2, ys, nb, final_g)
    return out.reshape(nb, L, D)
```

```python
import functools
import math

import numpy as np
import jax
import jax.numpy as jnp
from jax import lax
from jax.experimental import pallas as pl
from jax.experimental.pallas import tpu as pltpu

F32 = jnp.float32
BF16 = jnp.bfloat16
HIGHEST = lax.Precision.HIGHEST

D = 1024
L = 4096
GRID_W = 64
GRID_H = L // GRID_W
CTX = 256
EPS = 1e-6

CONV_CH = 512
NA_HEADS = 8
NA_DH = 64
WIN_H = 8
WIN_W = 16
EVEN_IN = 3072

RET_HEADS = 4
RET_DK = 256
RET_DV = 512
RET_C = 256
ODD_IN = 6144
ROPE_BASE = 10000.0

N_GROUPS = 4
EPG = 8
EH = 256
GH = EPG * EH

NEG = -1e30

NA_QR = 4
NA_KR = NA_QR + WIN_H - 1
NA_NQ = NA_QR * GRID_W
NA_NK = NA_KR * GRID_W
NA_WS_MAX = GRID_H - NA_KR

DX = D + 128
TS = 512
TD = 512

VMEM_LIMIT = 56 * 1024 * 1024


def _cp(sem, vmem=VMEM_LIMIT):
    return pltpu.CompilerParams(dimension_semantics=sem, vmem_limit_bytes=vmem)


def _silu(v):
    return v * (1.0 / (1.0 + jnp.exp(-v)))


def _src_specs(src, tm, lat_tiles):
    tail = tuple(src[0].shape[1:])
    zeros = (0,) * len(tail)
    if len(src) == 1:
        return [pl.BlockSpec((tm,) + tail, lambda i, *_: (i,) + zeros)]
    return [pl.BlockSpec((tm,) + tail, lambda i, *_: (jnp.minimum(i, lat_tiles - 1),) + zeros),
            pl.BlockSpec((tm,) + tail, lambda i, *_: (jnp.maximum(i - lat_tiles, 0),) + zeros)]


def _src_load(refs, i, lat_tiles, rows=slice(None)):
    if len(refs) == 1:
        return refs[0][rows, :]
    return jnp.where(i < lat_tiles, refs[0][rows, :], refs[1][rows, :])


def _ada_kernel(c_ref, w_ref, b_ref, o_ref):
    s = _silu(c_ref[...])
    o_ref[...] = jnp.dot(s, w_ref[...], precision=HIGHEST, preferred_element_type=F32) + b_ref[...]


def _ada(c16, ada_w, ada_b):
    depth = ada_w.shape[0]
    tn = 1536
    return pl.pallas_call(
        _ada_kernel,
        grid=(depth, 6 * D // tn),
        in_specs=[
            pl.BlockSpec((16, D), lambda l, j: (0, 0)),
            pl.BlockSpec((None, D, tn), lambda l, j: (l, 0, j)),
            pl.BlockSpec((None, 1, tn), lambda l, j: (l, 0, j)),
        ],
        out_specs=pl.BlockSpec((None, 16, tn), lambda l, j: (l, 0, j)),
        out_shape=jax.ShapeDtypeStruct((depth, 16, 6 * D), F32),
        compiler_params=_cp(("arbitrary", "arbitrary")),
        name="ada_mod",
    )(c16, ada_w, ada_b.reshape(depth, 1, 6 * D))


def _inproj_kernel(*refs, rope, tn, n_src, lat_tiles):
    x_refs, refs = refs[:n_src], refs[n_src:]
    if rope:
        g_ref, sh_ref, sc_ref, w_ref, cos_ref, sin_ref, o_ref, h_scr = refs
    else:
        g_ref, sh_ref, sc_ref, w_ref, o_ref, h_scr = refs
    i = pl.program_id(0)
    j = pl.program_id(1)

    if not rope and tn == w_ref.shape[1] == o_ref.shape[1]:
        coef = g_ref[...] * (1.0 + sc_ref[...])
        rs = 256
        for sbk in range(o_ref.shape[0] // rs):
            rows = slice(sbk * rs, (sbk + 1) * rs)
            x = _src_load(x_refs, i, lat_tiles, rows)
            ms = jnp.mean(x * x, axis=-1, keepdims=True)
            h = (x * lax.rsqrt(ms + EPS) * coef + sh_ref[...]).astype(BF16)
            o_ref[rows, :] = jnp.dot(h, w_ref[...], preferred_element_type=F32).astype(o_ref.dtype)
        return

    @pl.when(j == 0)
    def _():
        x = _src_load(x_refs, i, lat_tiles)
        ms = jnp.mean(x * x, axis=-1, keepdims=True)
        y = x * lax.rsqrt(ms + EPS) * g_ref[...]
        h_scr[...] = (y * (1.0 + sc_ref[...]) + sh_ref[...]).astype(BF16)

    acc = jnp.dot(h_scr[...], w_ref[...], preferred_element_type=F32)
    if not rope:
        o_ref[...] = acc.astype(o_ref.dtype)
        return

    @pl.when(j < 2)
    def _():
        cos = cos_ref[...]
        sin = sin_ref[...]
        scale = jnp.where(j == 1, RET_DK ** -0.5, 1.0).astype(F32)
        pieces = []
        for cblk in range(tn // 128):
            t = acc[:, cblk * 128:(cblk + 1) * 128]
            half = (cblk % 2) * 128
            r = pltpu.roll(t, 64, axis=1)
            pieces.append(t * cos[:, half:half + 128] + r * sin[:, half:half + 128])
        o_ref[...] = (jnp.concatenate(pieces, axis=1) * scale).astype(o_ref.dtype)

    @pl.when(j >= 2)
    def _():
        o_ref[...] = acc.astype(o_ref.dtype)


def _inproj(xsrc, gain, sh, sc, w, n_lat_rows, n_mod, tm, rope_tables=None):
    rows = sum(a.shape[0] for a in xsrc)
    n = w.shape[1]
    tn = 1024 if rope_tables is not None else n
    lat_tiles = n_lat_rows // tm
    mod_idx = lambda i, j: (jnp.minimum((i * tm) // L, n_mod), 0, 0)
    in_specs = _src_specs(xsrc, tm, lat_tiles) + [
        pl.BlockSpec((1, D), lambda i, j: (0, 0)),
        pl.BlockSpec((None, 1, D), mod_idx),
        pl.BlockSpec((None, 1, D), mod_idx),
        pl.BlockSpec((D, tn), lambda i, j: (0, j)),
    ]
    args = list(xsrc) + [gain.reshape(1, D), sh, sc, w]
    rope = rope_tables is not None
    if rope:
        per_seq = L // tm
        tab_idx = lambda i, j: (jnp.where(i < lat_tiles, i % per_seq, per_seq), 0)
        in_specs += [pl.BlockSpec((tm, RET_DK), tab_idx), pl.BlockSpec((tm, RET_DK), tab_idx)]
        args += list(rope_tables)
    return pl.pallas_call(
        functools.partial(_inproj_kernel, rope=rope, tn=tn, n_src=len(xsrc), lat_tiles=lat_tiles),
        grid=(rows // tm, n // tn),
        in_specs=in_specs,
        out_specs=pl.BlockSpec((tm, tn), lambda i, j: (i, j)),
        out_shape=jax.ShapeDtypeStruct((rows, n), BF16),
        scratch_shapes=[pltpu.VMEM((tm, D), BF16)],
        compiler_params=_cp(("arbitrary", "arbitrary")),
        name="inproj_rope" if rope else "inproj",
    )(*args)


def _inproj_comb_kernel(slot_ref, slotn_ref, x1_ref, g2_ref, g_ref, sh_ref, sc_ref, w_ref, cos_ref, sin_ref, ys_ref,
                        o_ref, xa_ref, h_scr, buf, sem, *, tn, tm, n_tiles, n_j, ch):
    i = pl.program_id(0)
    j = pl.program_id(1)
    rows_buf = ch * n_j

    def start_row(sref, b, r, u):
        src = sref[0, jnp.minimum(r, tm - 1)]
        pltpu.make_async_copy(ys_ref.at[pl.ds(src, 1)], buf.at[b, pl.ds(r, 1)], sem.at[b]).start(priority=u)

    @pl.when(j == 0)
    def _():
        @pl.when(i == 0)
        def _():
            def body(jj, carry):
                for u in range(2):
                    start_row(slot_ref, 0, 2 * jj + u, u)
                return carry

            lax.fori_loop(0, rows_buf // 2, body, 0)

        b = i % 2
        pltpu.make_async_copy(ys_ref.at[pl.ds(0, rows_buf)], buf.at[b], sem.at[b]).wait()
        x = x1_ref[...] + g2_ref[...] * buf[b, :tm, :]
        xa_ref[...] = x
        ms = jnp.mean(x * x, axis=-1, keepdims=True)
        coef = g_ref[...] * (1.0 + sc_ref[...])
        h_scr[...] = (x * lax.rsqrt(ms + EPS) * coef + sh_ref[...]).astype(BF16)

    bn = (i + 1) % 2
    for u in range(ch):
        start_row(slotn_ref, bn, j * ch + u, u % 2)

    acc = jnp.dot(h_scr[...], w_ref[...], preferred_element_type=F32)

    qk_tiles = 2 * RET_HEADS * RET_DK // tn

    @pl.when(j < qk_tiles)
    def _():
        cos = cos_ref[...]
        sin = sin_ref[...]
        pieces = []
        for cblk in range(tn // 128):
            t = acc[:, cblk * 128:(cblk + 1) * 128]
            half = (cblk % 2) * 128
            r = pltpu.roll(t, 64, axis=1)
            is_k = j * tn + cblk * 128 >= RET_HEADS * RET_DK
            scale = jnp.where(is_k, RET_DK ** -0.5, 1.0).astype(F32)
            pieces.append((t * cos[:, half:half + 128] + r * sin[:, half:half + 128]) * scale)
        o_ref[...] = jnp.concatenate(pieces, axis=1).astype(o_ref.dtype)

    @pl.when(j >= qk_tiles)
    def _():
        o_ref[...] = acc.astype(o_ref.dtype)

    @pl.when((i == n_tiles - 1) & (j == n_j - 1))
    def _():
        pltpu.make_async_copy(ys_ref.at[pl.ds(0, rows_buf)], buf.at[bn], sem.at[bn]).wait()


def _inproj_comb(slot3, ys, x1, g2, gain, sh, sc, w, n_lat_rows, n_mod, tm, rope_tables):
    rows = x1.shape[0]
    n = w.shape[1]
    tn = 2048
    n_tiles = rows // tm
    lat_tiles = n_lat_rows // tm
    per_seq = L // tm
    mod_idx = lambda i, j: (jnp.minimum((i * tm) // L, n_mod), 0, 0)
    tab_idx = lambda i, j: (jnp.where(i < lat_tiles, i % per_seq, per_seq), 0)
    n_j = n // tn
    ch = -(-tm // n_j)
    while (ch * n_j) % 8:
        ch += 1
    in_specs = [_slot_spec(slot3, tm), _slot_spec(slot3, tm, ahead=1),
                pl.BlockSpec((tm, D), lambda i, j: (i, 0)),
                pl.BlockSpec((None, 1, D), mod_idx),
                pl.BlockSpec((1, D), lambda i, j: (0, 0)),
                pl.BlockSpec((None, 1, D), mod_idx),
                pl.BlockSpec((None, 1, D), mod_idx),
                pl.BlockSpec((D, tn), lambda i, j: (0, j)),
                pl.BlockSpec((tm, RET_DK), tab_idx),
                pl.BlockSpec((tm, RET_DK), tab_idx),
                pl.BlockSpec(memory_space=pl.ANY)]
    return pl.pallas_call(
        functools.partial(_inproj_comb_kernel, tn=tn, tm=tm, n_tiles=n_tiles, n_j=n_j, ch=ch),
        grid=(n_tiles, n_j),
        in_specs=in_specs,
        out_specs=[pl.BlockSpec((tm, tn), lambda i, j: (i, j)),
                   pl.BlockSpec((tm, D), lambda i, j: (i, 0))],
        out_shape=[jax.ShapeDtypeStruct((rows, n), BF16),
                   jax.ShapeDtypeStruct((rows, D), F32)],
        scratch_shapes=[pltpu.VMEM((tm, D), BF16), pltpu.VMEM((2, ch * n_j, D), F32),
                        pltpu.SemaphoreType.DMA((2,))],
        compiler_params=_cp(("arbitrary", "arbitrary")),
        name="inproj_rope_combine",
    )(slot3, slot3, x1, g2, gain.reshape(1, D), sh, sc, w, rope_tables[0], rope_tables[1], ys)


def _rope_tables(tm):
    pos = np.arange(L)
    half = RET_DK // 4
    freqs = ROPE_BASE ** (-np.arange(half, dtype=np.float64) / half)
    ang_r = (pos // GRID_W)[:, None] * freqs[None, :]
    ang_c = (pos % GRID_W)[:, None] * freqs[None, :]
    cos = np.concatenate([np.cos(ang_r), np.cos(ang_r), np.cos(ang_c), np.cos(ang_c)], axis=1)
    sin = np.concatenate([-np.sin(ang_r), np.sin(ang_r), -np.sin(ang_c), np.sin(ang_c)], axis=1)
    cos = np.concatenate([cos, np.ones((tm, RET_DK))], axis=0)
    sin = np.concatenate([sin, np.zeros((tm, RET_DK))], axis=0)
    return jnp.asarray(cos, F32), jnp.asarray(sin, F32)


def _conv_kernel(ab_ref, ac_ref, ax_ref, w_ref, o_ref, *, seq):
    u = ac_ref[...].astype(F32) * ax_ref[...].astype(F32)
    row = lax.broadcasted_iota(jnp.int32, u.shape, 0)
    up = jnp.where(row == 0, 0.0, pltpu.roll(u, 1, axis=0))
    un = jnp.where(row == seq - 1, 0.0, pltpu.roll(u, seq - 1, axis=0))
    w = w_ref[...]
    y = ab_ref[...].astype(F32) * (up * w[0:1, :] + u * w[1:2, :] + un * w[2:3, :])
    o_ref[...] = y.astype(o_ref.dtype)


def _conv(qkv, conv_w, seq, n_seq, row_blk0):
    nb = CONV_CH // 128
    return pl.pallas_call(
        functools.partial(_conv_kernel, seq=seq),
        grid=(n_seq, nb),
        in_specs=[
            pl.BlockSpec((seq, 128), lambda b, j: (row_blk0 + b, j)),
            pl.BlockSpec((seq, 128), lambda b, j: (row_blk0 + b, nb + j)),
            pl.BlockSpec((seq, 128), lambda b, j: (row_blk0 + b, 2 * nb + j)),
            pl.BlockSpec((3, 128), lambda b, j: (0, j)),
        ],
        out_specs=pl.BlockSpec((seq, 128), lambda b, j: (b, j)),
        out_shape=jax.ShapeDtypeStruct((n_seq * seq, CONV_CH), BF16),
        compiler_params=_cp(("arbitrary", "arbitrary")),
        name="short_conv",
    )(qkv, qkv, qkv, conv_w)


def _natten_tcol(rpb):
    qc = np.arange(GRID_W)
    kc = np.arange(GRID_W)
    cs = np.clip(qc - WIN_W // 2, 0, GRID_W - WIN_W)
    col_ok = (kc[None, :] >= cs[:, None]) & (kc[None, :] < cs[:, None] + WIN_W)
    dx = np.clip(kc[None, :] - qc[:, None] + WIN_W - 1, 0, 2 * WIN_W - 2)
    tcol = jnp.where(col_ok[None, None], rpb.astype(F32)[:, :, dx], NEG)
    return tcol.reshape(NA_HEADS // 2, 2, 2 * WIN_H - 1, GRID_W, GRID_W)


def _natten_row_classes():
    table = []
    for r0 in (0, 2 * NA_QR, GRID_H - NA_QR):
        ws = int(np.clip(r0 - WIN_H // 2, 0, NA_WS_MAX))
        rows = []
        for q in range(NA_QR):
            qr = r0 + q
            rs = int(np.clip(qr - WIN_H // 2, 0, GRID_H - WIN_H))
            rows.append([(ws + k) - qr + WIN_H - 1 if rs <= ws + k < rs + WIN_H else None for k in range(NA_KR)])
        table.append(rows)
    return table


def _attend(qa, kw, vw, kc, vc, bias):
    dn = (((1,), (1,)), ((), ()))
    s2 = lax.dot_general(qa, kc, dn, preferred_element_type=F32)
    m = jnp.max(s2, axis=1, keepdims=True)
    if kw is not None:
        s1 = lax.dot_general(qa, kw, dn, preferred_element_type=F32) + bias
        m = jnp.maximum(m, jnp.max(s1, axis=1, keepdims=True))
        p1 = jnp.exp(s1 - m)
    p2 = jnp.exp(s2 - m)
    den = jnp.sum(p2, axis=1, keepdims=True)
    o = jnp.dot(p2.astype(BF16), vc, preferred_element_type=F32)
    if kw is not None:
        den = den + jnp.sum(p1, axis=1, keepdims=True)
        o = o + jnp.dot(p1.astype(BF16), vw, preferred_element_type=F32)
    return o / den


NA_SUB = 8


def _natten_kernel(q_ref, k_ref, v_ref, kc_ref, vc_ref, tcol_ref, o_ref, bias_ref):
    n_rb = GRID_H // NA_QR

    @pl.when((pl.program_id(1) == 0) & (pl.program_id(2) == 0))
    def _():
        neg = jnp.full((GRID_W, GRID_W), NEG, F32)
        for cls, rows in enumerate(_natten_row_classes()):
            for a in range(2):
                for qr, dys in enumerate(rows):
                    blocks = [neg if dy is None else tcol_ref[a, dy] for dy in dys]
                    bias_ref[cls, a, qr * GRID_W:(qr + 1) * GRID_W, :] = jnp.concatenate(blocks, axis=1)

    lane = lax.broadcasted_iota(jnp.int32, (NA_NQ, 128), 1)
    for sb in range(NA_SUB):
        rb = pl.program_id(2) * NA_SUB + sb
        ws = jnp.clip(rb * NA_QR - WIN_H // 2, 0, NA_WS_MAX)
        start = pl.multiple_of(ws * GRID_W, GRID_W)
        kw = k_ref[pl.ds(start, NA_NK), :]
        vw = v_ref[pl.ds(start, NA_NK), :]
        cls = jnp.where(rb == 0, 0, jnp.where(rb == n_rb - 1, 2, 1))
        q = q_ref[sb * NA_NQ:(sb + 1) * NA_NQ, :] * (NA_DH ** -0.5)
        zero = jnp.zeros_like(q)
        o0 = _attend(jnp.where(lane < NA_DH, q, zero), kw, vw, kc_ref[...], vc_ref[...], bias_ref[cls, 0])
        o1 = _attend(jnp.where(lane >= NA_DH, q, zero), kw, vw, kc_ref[...], vc_ref[...], bias_ref[cls, 1])
        o_ref[sb * NA_NQ:(sb + 1) * NA_NQ, :] = jnp.where(lane < NA_DH, o0, o1).astype(o_ref.dtype)


def _natten(qkv, tcol, nb):
    n_steps = GRID_H // NA_QR // NA_SUB
    qcol, kcol, vcol = 3 * CONV_CH // 128, 3 * CONV_CH // 128 + 4, 3 * CONV_CH // 128 + 8
    ctx_blk0 = nb * L // CTX
    return pl.pallas_call(
        _natten_kernel,
        grid=(NA_HEADS // 2, nb, n_steps),
        in_specs=[
            pl.BlockSpec((NA_SUB * NA_NQ, 128), lambda h, b, r: (b * n_steps + r, qcol + h)),
            pl.BlockSpec((L, 128), lambda h, b, r: (b, kcol + h)),
            pl.BlockSpec((L, 128), lambda h, b, r: (b, vcol + h)),
            pl.BlockSpec((CTX, 128), lambda h, b, r: (ctx_blk0 + b, kcol + h)),
            pl.BlockSpec((CTX, 128), lambda h, b, r: (ctx_blk0 + b, vcol + h)),
            pl.BlockSpec((None, 2, 2 * WIN_H - 1, GRID_W, GRID_W), lambda h, b, r: (h, 0, 0, 0, 0)),
        ],
        out_specs=pl.BlockSpec((NA_SUB * NA_NQ, 128), lambda h, b, r: (b * n_steps + r, h)),
        out_shape=jax.ShapeDtypeStruct((nb * L, NA_HEADS * NA_DH), BF16),
        scratch_shapes=[pltpu.VMEM((3, 2, NA_NQ, NA_NK), F32)],
        compiler_params=_cp(("arbitrary", "arbitrary", "arbitrary")),
        name="natten",
    )(qkv, qkv, qkv, qkv, qkv, tcol)


def _ctx_attn_kernel(q_ref, kc_ref, vc_ref, o_ref):
    q = q_ref[...] * (NA_DH ** -0.5)
    lane = lax.broadcasted_iota(jnp.int32, q.shape, 1)
    zero = jnp.zeros_like(q)
    o0 = _attend(jnp.where(lane < NA_DH, q, zero), None, None, kc_ref[...], vc_ref[...], None)
    o1 = _attend(jnp.where(lane >= NA_DH, q, zero), None, None, kc_ref[...], vc_ref[...], None)
    o_ref[...] = jnp.where(lane < NA_DH, o0, o1).astype(o_ref.dtype)


def _ctx_attn(qkv, nb):
    qcol, kcol, vcol = 3 * CONV_CH // 128, 3 * CONV_CH // 128 + 4, 3 * CONV_CH // 128 + 8
    ctx_blk0 = nb * L // CTX
    return pl.pallas_call(
        _ctx_attn_kernel,
        grid=(nb, NA_HEADS // 2),
        in_specs=[
            pl.BlockSpec((CTX, 128), lambda b, h: (ctx_blk0 + b, qcol + h)),
            pl.BlockSpec((CTX, 128), lambda b, h: (ctx_blk0 + b, kcol + h)),
            pl.BlockSpec((CTX, 128), lambda b, h: (ctx_blk0 + b, vcol + h)),
        ],
        out_specs=pl.BlockSpec((CTX, 128), lambda b, h: (b, h)),
        out_shape=jax.ShapeDtypeStruct((nb * CTX, NA_HEADS * NA_DH), BF16),
        compiler_params=_cp(("arbitrary", "arbitrary")),
        name="ctx_attn",
    )(qkv, qkv, qkv)


def _retention_kernel(lg_ref, q_ref, k_ref, v_ref, g_ref, kc_ref, vc_ref, o_ref, y_scr, stf_scr, stb_scr):
    h = pl.program_id(1)
    n_chunks = L // RET_C
    pos_r = lax.broadcasted_iota(jnp.int32, (RET_C, 1), 0).astype(F32)
    rel = pos_r - lax.broadcasted_iota(jnp.int32, (1, RET_C), 1).astype(F32)
    dn_t = (((1,), (1,)), ((), ()))
    dn_k = (((0,), (0,)), ((), ()))

    def decays(lg, fwd):
        if fwd:
            d_intra = jnp.where(rel >= 0, jnp.exp(lg * jnp.maximum(rel, 0.0)), 0.0)
            d_q = jnp.exp(lg * (pos_r + 1.0))
            d_k = jnp.exp(lg * (RET_C - 1.0 - pos_r))
        else:
            d_intra = jnp.where(rel <= 0, jnp.exp(lg * jnp.maximum(-rel, 0.0)), 0.0)
            d_q = jnp.exp(lg * (RET_C - pos_r))
            d_k = jnp.exp(lg * pos_r)
        d_c = jnp.exp(jnp.full((1, 1), RET_C, F32) * lg)
        return d_intra, d_q, d_k, d_c

    def ctx_state(st_scr, d_k, d_c, order):
        st_scr[...] = jnp.zeros_like(st_scr)
        for jc in order:
            kd = (kc_ref[jc * RET_C:(jc + 1) * RET_C, :].astype(F32) * d_k).astype(BF16)
            vc = vc_ref[jc * RET_C:(jc + 1) * RET_C, :]
            st_scr[...] = st_scr[...] * d_c + lax.dot_general(kd, vc, dn_k, preferred_element_type=F32)

    def chunk_out(st_scr, c, d_intra, d_q, d_k, d_c):
        r0 = pl.multiple_of(c * RET_C, RET_C)
        q = q_ref[pl.ds(r0, RET_C), :]
        k = k_ref[pl.ds(r0, RET_C), :]
        v = v_ref[pl.ds(r0, RET_C), :]
        s = lax.dot_general(q, k, dn_t, preferred_element_type=F32) * d_intra
        st = st_scr[...]
        o = (jnp.dot(s.astype(BF16), v, preferred_element_type=F32)
             + jnp.dot(q, st.astype(BF16), preferred_element_type=F32) * d_q)
        kd = (k.astype(F32) * d_k).astype(BF16)
        st_scr[...] = st * d_c + lax.dot_general(kd, v, dn_k, preferred_element_type=F32)
        return r0, o

    def finish(r0, y):
        y = y * lax.rsqrt(jnp.mean(y * y, axis=-1, keepdims=True) + EPS)
        g = g_ref[pl.ds(r0, RET_C), :].astype(F32)
        o_ref[pl.ds(r0, RET_C), :] = (_silu(g) * y).astype(o_ref.dtype)

    dec_f = decays(lg_ref[0, h], True)
    dec_b = decays(lg_ref[1, h], False)
    ctx_state(stf_scr, dec_f[2], dec_f[3], range(CTX // RET_C))
    ctx_state(stb_scr, dec_b[2], dec_b[3], range(CTX // RET_C - 1, -1, -1))

    def first_half(i, carry):
        r0, o = chunk_out(stf_scr, i, *dec_f)
        y_scr[pl.ds(r0, RET_C), :] = o
        r1, o = chunk_out(stb_scr, n_chunks - 1 - i, *dec_b)
        y_scr[pl.ds(r1, RET_C), :] = o
        return carry

    lax.fori_loop(0, n_chunks // 2, first_half, 0)

    def second_half(i, carry):
        r0, o = chunk_out(stf_scr, i, *dec_f)
        finish(r0, y_scr[pl.ds(r0, RET_C), :] + o)
        r1, o = chunk_out(stb_scr, n_chunks - 1 - i, *dec_b)
        finish(r1, y_scr[pl.ds(r1, RET_C), :] + o)
        return carry

    lax.fori_loop(n_chunks // 2, n_chunks, second_half, 0)


def _retention(qkvg, lg, nb):
    ctx_blk0 = nb * L // CTX
    kq = RET_HEADS
    vq = 2 * RET_HEADS * RET_DK // RET_DV
    gq = vq + RET_HEADS
    return pl.pallas_call(
        _retention_kernel,
        grid=(nb, RET_HEADS),
        in_specs=[
            pl.BlockSpec(memory_space=pltpu.SMEM),
            pl.BlockSpec((L, RET_DK), lambda b, h: (b, h)),
            pl.BlockSpec((L, RET_DK), lambda b, h: (b, kq + h)),
            pl.BlockSpec((L, RET_DV), lambda b, h: (b, vq + h)),
            pl.BlockSpec((L, RET_DV), lambda b, h: (b, gq + h)),
            pl.BlockSpec((CTX, RET_DK), lambda b, h: (ctx_blk0 + b, kq + h)),
            pl.BlockSpec((CTX, RET_DV), lambda b, h: (ctx_blk0 + b, vq + h)),
        ],
        out_specs=pl.BlockSpec((L, RET_DV), lambda b, h: (b, h)),
        out_shape=jax.ShapeDtypeStruct((nb * L, RET_HEADS * RET_DV), BF16),
        scratch_shapes=[pltpu.VMEM((L, RET_DV), F32), pltpu.VMEM((RET_DK, RET_DV), F32),
                        pltpu.VMEM((RET_DK, RET_DV), F32)],
        compiler_params=_cp(("arbitrary", "arbitrary")),
        name="retention",
    )(lg, qkvg, qkvg, qkvg, qkvg, qkvg, qkvg)


def _outproj_kernel(*refs, src_lens, lat_tiles):
    i = pl.program_id(0)
    srcs = []
    for n in src_lens:
        srcs.append(refs[:n])
        refs = refs[n:]
    n_act = len(src_lens) - 1
    w_refs, refs = refs[:n_act], refs[n_act:]
    (g1_ref, sh2_ref, sc2_ref, n2_ref, wr_ref, br_ref, tri_ref,
     x1_ref, h2x_ref, rt_ref, cnt_ref, carry_scr) = refs

    @pl.when(i == 0)
    def _():
        carry_scr[...] = jnp.zeros_like(carry_scr)

    coef = n2_ref[...] * (1.0 + sc2_ref[...])
    rs = tri_ref.shape[0]
    for sbk in range(x1_ref.shape[0] // rs):
        rows = slice(sbk * rs, (sbk + 1) * rs)
        y = None
        for a_src, w_ref in zip(srcs[:n_act], w_refs):
            t = jnp.dot(_src_load(a_src, i, lat_tiles, rows), w_ref[...], preferred_element_type=F32)
            y = t if y is None else y + t
        x1 = _src_load(srcs[n_act], i, lat_tiles, rows) + g1_ref[...] * y
        x1_ref[rows, :] = x1
        ms = jnp.mean(x1 * x1, axis=-1, keepdims=True)
        h2 = x1 * lax.rsqrt(ms + EPS) * coef + sh2_ref[...]
        h2x_ref[rows, :D] = h2
        h_hi = h2.astype(BF16)
        h_lo = (h2 - h_hi.astype(F32)).astype(BF16)
        p = jnp.dot(h_hi, wr_ref[...], preferred_element_type=F32)
        q = jnp.dot(h_lo, wr_ref[:, :128], preferred_element_type=F32)
        route = _route(p[:, :128] + p[:, 128:] + q + br_ref[...], tri_ref, carry_scr)
        h2x_ref[rows, D:] = route
        rt_ref[:, rows] = route.T[8:16, :]
    cnt_ref[...] = carry_scr[...]


def _outproj(acts, ws, xsrc, g1, sh2, sc2, n2g, wr2, br, rows, n_lat_rows, n_mod, tm):
    lat_tiles = n_lat_rows // tm
    rs = tm
    tri = jnp.asarray(np.tril(np.ones((rs, rs), np.float32), -1), BF16)
    mod_idx = lambda i: (jnp.minimum((i * tm) // L, n_mod), 0, 0)
    in_specs, args = [], []
    for src in list(acts) + [xsrc]:
        in_specs += _src_specs(src, tm, lat_tiles)
        args += list(src)
    in_specs += [pl.BlockSpec(w.shape, lambda i: (0, 0)) for w in ws]
    in_specs += [pl.BlockSpec((None, 1, D), mod_idx),
                 pl.BlockSpec((None, 1, D), mod_idx),
                 pl.BlockSpec((None, 1, D), mod_idx),
                 pl.BlockSpec((1, D), lambda i: (0, 0)),
                 pl.BlockSpec((D, 256), lambda i: (0, 0)),
                 pl.BlockSpec((1, 128), lambda i: (0, 0)),
                 pl.BlockSpec((rs, rs), lambda i: (0, 0))]
    args += list(ws) + [g1, sh2, sc2, n2g.reshape(1, D), wr2, br, tri]
    return pl.pallas_call(
        functools.partial(_outproj_kernel, src_lens=tuple(len(s) for s in list(acts) + [xsrc]),
                          lat_tiles=lat_tiles),
        grid=(rows // tm,),
        in_specs=in_specs,
        out_specs=[pl.BlockSpec((tm, D), lambda i: (i, 0)),
                   pl.BlockSpec((tm, DX), lambda i: (i, 0)),
                   pl.BlockSpec((None, 8, tm), lambda i: (i, 0, 0)),
                   pl.BlockSpec((8, 128), lambda i: (0, 0))],
        out_shape=[jax.ShapeDtypeStruct((rows, D), F32),
                   jax.ShapeDtypeStruct((rows, DX), F32),
                   jax.ShapeDtypeStruct((rows // tm, 8, tm), F32),
                   jax.ShapeDtypeStruct((8, 128), F32)],
        scratch_shapes=[pltpu.VMEM((8, 128), F32)],
        compiler_params=_cp(("arbitrary",)),
        name="outproj_router",
    )(*args)


def _route(lg, tri_ref, carry_scr):
    lane = lax.broadcasted_iota(jnp.int32, lg.shape, 1)
    lane_f = lane.astype(F32)
    big = jnp.float32(1e9)

    gmask = lane < N_GROUPS
    gl = jnp.where(gmask, lg, NEG)
    gm = jnp.max(gl, axis=1, keepdims=True)
    gidx = jnp.min(jnp.where(gl == gm, lane_f, big), axis=1, keepdims=True)
    g_val = 1.0 / jnp.sum(jnp.where(gmask, jnp.exp(gl - gm), 0.0), axis=1, keepdims=True)

    base = 8.0 + 8.0 * gidx
    emask = (lane_f >= base) & (lane_f < base + EPG)
    el = jnp.where(emask, lg, NEG)
    m1 = jnp.max(el, axis=1, keepdims=True)
    i1 = jnp.min(jnp.where(el == m1, lane_f, big), axis=1, keepdims=True)
    el2 = jnp.where(lane_f == i1, NEG, el)
    m2 = jnp.max(el2, axis=1, keepdims=True)
    i2 = jnp.min(jnp.where(el2 == m2, lane_f, big), axis=1, keepdims=True)
    r = jnp.exp(m2 - m1)
    w1 = g_val / (1.0 + r)
    w2 = w1 * r
    gates = jnp.where(lane_f == i1 - base, w1, jnp.where(lane_f == i2 - base, w2, 0.0))

    onehot = jnp.where(lane_f == gidx, 1.0, 0.0)
    before = jnp.dot(tri_ref[...], onehot.astype(BF16), preferred_element_type=F32)
    carry = carry_scr[0:1, :]
    rank = jnp.sum(jnp.where(lane_f == gidx, before + carry, 0.0), axis=1, keepdims=True)
    carry_scr[0:1, :] = carry + jnp.sum(onehot, axis=0, keepdims=True)

    return jnp.where(lane < EPG, gates, jnp.where(lane == 8, gidx, jnp.where(lane == 9, rank, 0.0)))


def _slot_kernel(rt_ref, st_ref, o_ref):
    g = rt_ref[0:1, :]
    slot = rt_ref[1:2, :]
    st = st_ref[...]
    for k in range(N_GROUPS):
        slot = slot + jnp.where(g == k, st[:, k:k + 1], 0.0)
    o_ref[...] = slot.astype(jnp.int32)


def _slots(route_t, starts_row):
    n, _, st = route_t.shape
    return pl.pallas_call(
        _slot_kernel,
        grid=(n,),
        in_specs=[pl.BlockSpec((None, 8, st), lambda i: (i, 0, 0)),
                  pl.BlockSpec((1, 128), lambda i: (0, 0))],
        out_specs=pl.BlockSpec((None, 1, st), lambda i: (i, 0, 0)),
        out_shape=jax.ShapeDtypeStruct((n, 1, st), jnp.int32),
        compiler_params=_cp(("arbitrary",)),
        name="moe_slots",
    )(route_t, starts_row)


def _slot_spec(slot3, tile, ahead=0):
    n, _, st = slot3.shape
    per = st // tile
    last = n * per - 1

    def idx(i, *_):
        t = jnp.minimum(i + ahead, last)
        return (t // per, 0, t % per)

    return pl.BlockSpec((None, 1, tile), idx, memory_space=pltpu.SMEM)


def _dispatch_kernel(slot_ref, pad_ref, h2x_ref, xs_ref, zx_scr, sem):
    i = pl.program_id(0)
    n_pad = pad_ref.shape[1]

    @pl.when(i == 0)
    def _():
        zx_scr[...] = jnp.zeros_like(zx_scr)

        def zbody(r, carry):
            pltpu.make_async_copy(zx_scr.at[pl.ds(0, 1)], xs_ref.at[pl.ds(pad_ref[0, r], 1)], sem.at[0]).start()
            return carry

        lax.fori_loop(0, n_pad, zbody, 0)
        pltpu.make_async_copy(xs_ref.at[pl.ds(0, n_pad)], xs_ref.at[pl.ds(0, n_pad)], sem.at[0]).wait()

    td = h2x_ref.shape[0]
    for r in range(td):
        pltpu.make_async_copy(h2x_ref.at[pl.ds(r, 1)], xs_ref.at[pl.ds(slot_ref[0, r], 1)],
                              sem.at[0]).start(priority=r % 2)
    pltpu.make_async_copy(h2x_ref, xs_ref.at[pl.ds(0, td)], sem.at[0]).wait()


def _dispatch(slot3, pad_slots, h2x, n_sorted):
    rows = h2x.shape[0]
    n_pad = pad_slots.shape[0]
    td = slot3.shape[2]
    return pl.pallas_call(
        _dispatch_kernel,
        grid=(rows // td,),
        in_specs=[_slot_spec(slot3, td),
                  pl.BlockSpec((1, n_pad), lambda i: (0, 0), memory_space=pltpu.SMEM),
                  pl.BlockSpec((td, DX), lambda i: (i, 0))],
        out_specs=pl.BlockSpec(memory_space=pl.ANY),
        out_shape=jax.ShapeDtypeStruct((n_sorted, DX), F32),
        scratch_shapes=[pltpu.VMEM((8, DX), F32), pltpu.SemaphoreType.DMA((1,))],
        compiler_params=_cp(("arbitrary",)),
        name="moe_dispatch",
    )(slot3, pad_slots.reshape(1, n_pad), h2x)


def _moe_kernel(tg_ref, na_ref, x_ref, w1_ref, w3_ref, w2_ref, o_ref, hid_scr, w1b_scr, w3b_scr, w2b_scr):
    i = pl.program_id(0)

    @pl.when((i == 0) | (tg_ref[i] != tg_ref[jnp.maximum(i - 1, 0)]))
    def _():
        for e in range(EPG):
            cols = slice(e * EH, (e + 1) * EH)
            w1b_scr[:, cols] = w1_ref[:, cols].astype(BF16)
            w3b_scr[:, cols] = w3_ref[:, cols].astype(BF16)
            w2b_scr[cols, :] = w2_ref[cols, :].astype(BF16)

    @pl.when(i < na_ref[0])
    def _():
        x = x_ref[:, :D].astype(BF16)
        gt = x_ref[:, D:]
        for e in range(EPG):
            cols = slice(e * EH, (e + 1) * EH)
            h1 = jnp.dot(x, w1b_scr[:, cols], preferred_element_type=F32)
            h3 = jnp.dot(x, w3b_scr[:, cols], preferred_element_type=F32)
            hid_scr[:, cols] = (_silu(h1) * h3 * gt[:, e:e + 1]).astype(BF16)
        o_ref[...] = jnp.dot(hid_scr[...], w2b_scr[...], preferred_element_type=F32)

    @pl.when(i >= na_ref[0])
    def _():
        o_ref[...] = jnp.zeros_like(o_ref)


def _moe(tile_group, n_active, xs, w1, w3, w2, layer, n_tiles):
    x_idx = lambda i, tg, na: (jnp.minimum(i, na[0] - 1), 0)
    out_idx = lambda i, tg, na: (i, 0)
    w_idx = lambda i, tg, na: (layer, tg[i], 0, 0)
    once = pl.Buffered(1)
    grid_spec = pltpu.PrefetchScalarGridSpec(
        num_scalar_prefetch=2,
        grid=(n_tiles,),
        in_specs=[pl.BlockSpec((TS, DX), x_idx),
                  pl.BlockSpec((None, None, D, GH), w_idx, pipeline_mode=once),
                  pl.BlockSpec((None, None, D, GH), w_idx, pipeline_mode=once),
                  pl.BlockSpec((None, None, GH, D), w_idx, pipeline_mode=once)],
        out_specs=pl.BlockSpec((TS, D), out_idx),
        scratch_shapes=[pltpu.VMEM((TS, GH), BF16), pltpu.VMEM((D, GH), BF16), pltpu.VMEM((D, GH), BF16),
                        pltpu.VMEM((GH, D), BF16)],
    )
    return pl.pallas_call(
        _moe_kernel,
        grid_spec=grid_spec,
        out_shape=jax.ShapeDtypeStruct((xs.shape[0], D), F32),
        compiler_params=_cp(("arbitrary",)),
        name="moe_experts",
    )(tile_group, n_active, xs, w1, w3, w2)


def _combine_kernel(*refs, final, n_steps):
    if final:
        slot_ref, slot_next_ref, x1_ref, g2_ref, fg_ref, ys_ref, o_ref, buf, sem = refs
    else:
        slot_ref, slot_next_ref, x1_ref, g2_ref, ys_ref, o_ref, buf, sem = refs
    i = pl.program_id(0)

    def start_row(sref, b, r, u):
        pltpu.make_async_copy(ys_ref.at[pl.ds(sref[0, r], 1)], buf.at[b, pl.ds(r, 1)], sem.at[b]).start(priority=u)

    @pl.when(i == 0)
    def _():
        def body(j, carry):
            for u in range(2):
                start_row(slot_ref, 0, 2 * j + u, u)
            return carry

        lax.fori_loop(0, TD // 2, body, 0)

    b = i % 2
    bn = (i + 1) % 2
    pltpu.make_async_copy(ys_ref.at[pl.ds(0, TD)], buf.at[b], sem.at[b]).wait()

    for r in range(TD):
        start_row(slot_next_ref, bn, r, r % 2)

    x2 = x1_ref[...] + g2_ref[...] * buf[b]
    if final:
        ms = jnp.mean(x2 * x2, axis=-1, keepdims=True)
        x2 = x2 * lax.rsqrt(ms + EPS) * fg_ref[...]
    o_ref[...] = x2

    @pl.when(i == n_steps - 1)
    def _():
        pltpu.make_async_copy(ys_ref.at[pl.ds(0, TD)], buf.at[bn], sem.at[bn]).wait()


def _combine(slot3, x1, g2, ys, n_mod, final_g=None):
    rows = x1.shape[0]
    n_steps = rows // TD
    final = final_g is not None
    mod_idx = lambda i: (jnp.minimum((i * TD) // L, n_mod), 0, 0)
    in_specs = [_slot_spec(slot3, TD), _slot_spec(slot3, TD, ahead=1),
                pl.BlockSpec((TD, D), lambda i: (i, 0)),
                pl.BlockSpec((None, 1, D), mod_idx)]
    args = [slot3, slot3, x1, g2]
    if final:
        in_specs.append(pl.BlockSpec((1, D), lambda i: (0, 0)))
        args.append(final_g.reshape(1, D))
    in_specs.append(pl.BlockSpec(memory_space=pl.ANY))
    args.append(ys)
    return pl.pallas_call(
        functools.partial(_combine_kernel, final=final, n_steps=n_steps),
        grid=(n_steps,),
        in_specs=in_specs,
        out_specs=pl.BlockSpec((TD, D), lambda i: (i, 0)),
        out_shape=jax.ShapeDtypeStruct((rows, D), F32),
        scratch_shapes=[pltpu.VMEM((2, TD, D), F32), pltpu.SemaphoreType.DMA((2,))],
        compiler_params=_cp(("arbitrary",)),
        name="moe_combine_final" if final else "moe_combine",
    )(*args)


def _sort_plan(counts_row, rows):
    counts = counts_row[:N_GROUPS].astype(jnp.int32)
    padded = ((counts + TS - 1) // TS) * TS
    ends = jnp.cumsum(padded)
    starts = ends - padded
    starts_row = jnp.zeros((1, 128), F32).at[0, :N_GROUPS].set(starts.astype(F32))
    n_tiles = rows // TS + N_GROUPS
    n_active = (ends[-1] // TS).reshape(1)
    tile_start = jnp.arange(n_tiles, dtype=jnp.int32) * TS
    tile_group = jnp.minimum(jnp.sum(tile_start[:, None] >= ends[None, :], axis=1), N_GROUPS - 1).astype(jnp.int32)
    r = jnp.arange(TS, dtype=jnp.int32)
    pad_real = ((starts + counts)[:, None] + r[None, :]).reshape(-1)
    is_pad = pad_real < jnp.repeat(ends, TS)
    tail = ends[-1] + jnp.cumsum(jnp.logical_not(is_pad).astype(jnp.int32)) - 1
    pad_slots = jnp.where(is_pad, pad_real, tail)
    return starts_row, pad_slots, tile_group, n_active, n_tiles, n_tiles * TS


def _moe_sorted(h2x, route_t, cnt, w1, w3, w2, layer):
    rows = h2x.shape[0]
    starts_row, pad_slots, tile_group, n_active, n_tiles, n_sorted = _sort_plan(cnt[0], rows)
    slot3 = _slots(route_t, starts_row)
    xs = _dispatch(slot3, pad_slots, h2x, n_sorted)
    ys = _moe(tile_group, n_active, xs, w1, w3, w2, layer, n_tiles)
    return slot3, ys


def _router_weights(gw, gb, ew, eb):
    wr = jnp.zeros((D, 128), F32).at[:, :N_GROUPS].set(gw).at[:, 8:8 + N_GROUPS * EPG].set(ew)
    br = jnp.zeros((1, 128), F32).at[0, :N_GROUPS].set(gb).at[0, 8:8 + N_GROUPS * EPG].set(eb)
    hi = wr.astype(BF16)
    lo = (wr - hi.astype(F32)).astype(BF16)
    return jnp.concatenate([hi, lo], axis=1), br


def kernel(x, c, ctx, c_ctx, ada_w, ada_b, norm1_g, norm2_g, w_in_e, conv_w, na_rpb, w_out_e, w_in_o,
           ret_decay_f, ret_decay_b, w_out_o, router_gw, router_gb, router_ew, router_eb,
           moe_w1, moe_w3, moe_w2, final_g):
    nb = x.shape[0]
    assert x.shape[1:] == (L, D) and ctx.shape[1:] == (CTX, D) and nb < 16 and ada_w.shape[0] == 2
    t_lat = nb * L
    t_ctx = nb * CTX
    t_all = t_lat + t_ctx
    tm = 1024 if t_ctx % 1024 == 0 else 512

    xsrc = (x.reshape(t_lat, D), ctx.reshape(t_ctx, D))
    c16 = jnp.zeros((16, D), F32).at[:nb].set(c).at[nb].set(c_ctx)
    mod = _ada(c16, ada_w, ada_b).reshape(2, 16, 6, 1, D)
    mods = [[mod[l, :, k] for k in range(6)] for l in range(2)]

    sh1, sc1, g1, sh2, sc2, g2 = mods[0]
    qkv = _inproj(xsrc, norm1_g[0], sh1, sc1, w_in_e[0].astype(BF16), t_lat, nb, tm)
    y_conv = (_conv(qkv, conv_w[0], L, nb, 0), _conv(qkv, conv_w[0], CTX, nb, t_lat // CTX))
    y_attn = (_natten(qkv, _natten_tcol(na_rpb[0]), nb), _ctx_attn(qkv, nb))
    w_out = w_out_e[0].astype(BF16)
    wr2, br = _router_weights(router_gw[0], router_gb[0], router_ew[0], router_eb[0])
    x1, h2x, route_t, cnt = _outproj([y_conv, y_attn], [w_out[:CONV_CH], w_out[CONV_CH:]], xsrc, g1, sh2, sc2,
                                     norm2_g[0], wr2, br, t_all, t_lat, nb, tm)
    slot3, ys = _moe_sorted(h2x, route_t, cnt, moe_w1, moe_w3, moe_w2, 0)

    sh1, sc1, g1n, sh2n, sc2n, g2n = mods[1]
    qkvg, xa = _inproj_comb(slot3, ys, x1, g2, norm1_g[1], sh1, sc1, w_in_o[0].astype(BF16), t_lat, nb, tm,
                            _rope_tables(tm))
    g1, sh2, sc2, g2 = g1n, sh2n, sc2n, g2n
    lg = jnp.stack([jnp.log1p(-jnp.exp(ret_decay_f[0].astype(F32))),
                    jnp.log1p(-jnp.exp(ret_decay_b[0].astype(F32)))])
    y_ret = _retention(qkvg, lg, nb)
    wr2, br = _router_weights(router_gw[1], router_gb[1], router_ew[1], router_eb[1])
    x1, h2x, route_t, cnt = _outproj([(y_ret,)], [w_out_o[0].astype(BF16)], (xa,), g1, sh2, sc2, norm2_g[1],
                                     wr2, br, t_lat, t_lat, nb, tm)
    slot3, ys = _moe_sorted(h2x, route_t, cnt, moe_w1, moe_w3, moe_w2, 1)
    out = _combine(slot3, x1, g2, ys, nb, final_g)
    return out.reshape(nb, L, D)
```

```python
import functools
import math

import numpy as np
import jax
import jax.numpy as jnp
from jax import lax
from jax.experimental import pallas as pl
from jax.experimental.pallas import tpu as pltpu

F32 = jnp.float32
BF16 = jnp.bfloat16
HIGHEST = lax.Precision.HIGHEST

D = 1024
L = 4096
GRID_W = 64
GRID_H = L // GRID_W
CTX = 256
EPS = 1e-6

CONV_CH = 512
NA_HEADS = 8
NA_DH = 64
WIN_H = 8
WIN_W = 16
EVEN_IN = 3072

RET_HEADS = 4
RET_DK = 256
RET_DV = 512
RET_C = 256
ODD_IN = 6144
ROPE_BASE = 10000.0

N_GROUPS = 4
EPG = 8
EH = 256
GH = EPG * EH

NEG = -1e30

NA_QR = 4
NA_KR = NA_QR + WIN_H - 1
NA_NQ = NA_QR * GRID_W
NA_NK = NA_KR * GRID_W
NA_WS_MAX = GRID_H - NA_KR

DX = D + 128
TS = 512
TD = 512

VMEM_LIMIT = 56 * 1024 * 1024


def _cp(sem, vmem=VMEM_LIMIT):
    return pltpu.CompilerParams(dimension_semantics=sem, vmem_limit_bytes=vmem)


def _silu(v):
    return v * (1.0 / (1.0 + jnp.exp(-v)))


def _src_specs(src, tm, lat_tiles):
    tail = tuple(src[0].shape[1:])
    zeros = (0,) * len(tail)
    if len(src) == 1:
        return [pl.BlockSpec((tm,) + tail, lambda i, *_: (i,) + zeros)]
    return [pl.BlockSpec((tm,) + tail, lambda i, *_: (jnp.minimum(i, lat_tiles - 1),) + zeros),
            pl.BlockSpec((tm,) + tail, lambda i, *_: (jnp.maximum(i - lat_tiles, 0),) + zeros)]


def _src_load(refs, i, lat_tiles, rows=slice(None)):
    if len(refs) == 1:
        return refs[0][rows, :]
    return jnp.where(i < lat_tiles, refs[0][rows, :], refs[1][rows, :])


def _ada_kernel(c_ref, w_ref, b_ref, o_ref):
    s = _silu(c_ref[...])
    o_ref[...] = jnp.dot(s, w_ref[...], precision=HIGHEST, preferred_element_type=F32) + b_ref[...]


def _ada(c16, ada_w, ada_b):
    depth = ada_w.shape[0]
    tn = 1536
    return pl.pallas_call(
        _ada_kernel,
        grid=(depth, 6 * D // tn),
        in_specs=[
            pl.BlockSpec((16, D), lambda l, j: (0, 0)),
            pl.BlockSpec((None, D, tn), lambda l, j: (l, 0, j)),
            pl.BlockSpec((None, 1, tn), lambda l, j: (l, 0, j)),
        ],
        out_specs=pl.BlockSpec((None, 16, tn), lambda l, j: (l, 0, j)),
        out_shape=jax.ShapeDtypeStruct((depth, 16, 6 * D), F32),
        compiler_params=_cp(("arbitrary", "arbitrary")),
        name="ada_mod",
    )(c16, ada_w, ada_b.reshape(depth, 1, 6 * D))


def _inproj_kernel(*refs, rope, tn, n_src, lat_tiles):
    x_refs, refs = refs[:n_src], refs[n_src:]
    if rope:
        g_ref, sh_ref, sc_ref, w_ref, cos_ref, sin_ref, o_ref, h_scr = refs
    else:
        g_ref, sh_ref, sc_ref, w_ref, o_ref, h_scr = refs
    i = pl.program_id(0)
    j = pl.program_id(1)

    if not rope and tn == w_ref.shape[1] == o_ref.shape[1]:
        coef = g_ref[...] * (1.0 + sc_ref[...])
        rs = 256
        for sbk in range(o_ref.shape[0] // rs):
            rows = slice(sbk * rs, (sbk + 1) * rs)
            x = _src_load(x_refs, i, lat_tiles, rows)
            ms = jnp.mean(x * x, axis=-1, keepdims=True)
            h = (x * lax.rsqrt(ms + EPS) * coef + sh_ref[...]).astype(BF16)
            o_ref[rows, :] = jnp.dot(h, w_ref[...], preferred_element_type=F32).astype(o_ref.dtype)
        return

    @pl.when(j == 0)
    def _():
        x = _src_load(x_refs, i, lat_tiles)
        ms = jnp.mean(x * x, axis=-1, keepdims=True)
        y = x * lax.rsqrt(ms + EPS) * g_ref[...]
        h_scr[...] = (y * (1.0 + sc_ref[...]) + sh_ref[...]).astype(BF16)

    acc = jnp.dot(h_scr[...], w_ref[...], preferred_element_type=F32)
    if not rope:
        o_ref[...] = acc.astype(o_ref.dtype)
        return

    @pl.when(j < 2)
    def _():
        cos = cos_ref[...]
        sin = sin_ref[...]
        scale = jnp.where(j == 1, RET_DK ** -0.5, 1.0).astype(F32)
        pieces = []
        for cblk in range(tn // 128):
            t = acc[:, cblk * 128:(cblk + 1) * 128]
            half = (cblk % 2) * 128
            r = pltpu.roll(t, 64, axis=1)
            pieces.append(t * cos[:, half:half + 128] + r * sin[:, half:half + 128])
        o_ref[...] = (jnp.concatenate(pieces, axis=1) * scale).astype(o_ref.dtype)

    @pl.when(j >= 2)
    def _():
        o_ref[...] = acc.astype(o_ref.dtype)


def _inproj(xsrc, gain, sh, sc, w, n_lat_rows, n_mod, tm, rope_tables=None):
    rows = sum(a.shape[0] for a in xsrc)
    n = w.shape[1]
    tn = 1024 if rope_tables is not None else n
    lat_tiles = n_lat_rows // tm
    mod_idx = lambda i, j: (jnp.minimum((i * tm) // L, n_mod), 0, 0)
    in_specs = _src_specs(xsrc, tm, lat_tiles) + [
        pl.BlockSpec((1, D), lambda i, j: (0, 0)),
        pl.BlockSpec((None, 1, D), mod_idx),
        pl.BlockSpec((None, 1, D), mod_idx),
        pl.BlockSpec((D, tn), lambda i, j: (0, j)),
    ]
    args = list(xsrc) + [gain.reshape(1, D), sh, sc, w]
    rope = rope_tables is not None
    if rope:
        per_seq = L // tm
        tab_idx = lambda i, j: (jnp.where(i < lat_tiles, i % per_seq, per_seq), 0)
        in_specs += [pl.BlockSpec((tm, RET_DK), tab_idx), pl.BlockSpec((tm, RET_DK), tab_idx)]
        args += list(rope_tables)
    return pl.pallas_call(
        functools.partial(_inproj_kernel, rope=rope, tn=tn, n_src=len(xsrc), lat_tiles=lat_tiles),
        grid=(rows // tm, n // tn),
        in_specs=in_specs,
        out_specs=pl.BlockSpec((tm, tn), lambda i, j: (i, j)),
        out_shape=jax.ShapeDtypeStruct((rows, n), BF16),
        scratch_shapes=[pltpu.VMEM((tm, D), BF16)],
        compiler_params=_cp(("arbitrary", "arbitrary")),
        name="inproj_rope" if rope else "inproj",
    )(*args)


def _inproj_comb_kernel(slot_ref, slotn_ref, x1_ref, g2_ref, g_ref, sh_ref, sc_ref, w_ref, cos_ref, sin_ref, ys_ref,
                        o_ref, xa_ref, h_scr, buf, sem, *, tn, tm, n_tiles, n_j, ch):
    i = pl.program_id(0)
    j = pl.program_id(1)
    rows_buf = ch * n_j

    def start_row(sref, b, r, u):
        src = sref[0, jnp.minimum(r, tm - 1)]
        pltpu.make_async_copy(ys_ref.at[pl.ds(src, 1)], buf.at[b, pl.ds(r, 1)], sem.at[b]).start(priority=u)

    @pl.when(j == 0)
    def _():
        @pl.when(i == 0)
        def _():
            def body(jj, carry):
                for u in range(2):
                    start_row(slot_ref, 0, 2 * jj + u, u)
                return carry

            lax.fori_loop(0, rows_buf // 2, body, 0)

        b = i % 2
        pltpu.make_async_copy(ys_ref.at[pl.ds(0, rows_buf)], buf.at[b], sem.at[b]).wait()
        x = x1_ref[...] + g2_ref[...] * buf[b, :tm, :]
        xa_ref[...] = x
        ms = jnp.mean(x * x, axis=-1, keepdims=True)
        coef = g_ref[...] * (1.0 + sc_ref[...])
        h_scr[...] = (x * lax.rsqrt(ms + EPS) * coef + sh_ref[...]).astype(BF16)

    bn = (i + 1) % 2
    for u in range(ch):
        start_row(slotn_ref, bn, j * ch + u, u % 2)

    acc = jnp.dot(h_scr[...], w_ref[...], preferred_element_type=F32)

    qk_tiles = 2 * RET_HEADS * RET_DK // tn

    @pl.when(j < qk_tiles)
    def _():
        cos = cos_ref[...]
        sin = sin_ref[...]
        pieces = []
        for cblk in range(tn // 128):
            t = acc[:, cblk * 128:(cblk + 1) * 128]
            half = (cblk % 2) * 128
            r = pltpu.roll(t, 64, axis=1)
            is_k = j * tn + cblk * 128 >= RET_HEADS * RET_DK
            scale = jnp.where(is_k, RET_DK ** -0.5, 1.0).astype(F32)
            pieces.append((t * cos[:, half:half + 128] + r * sin[:, half:half + 128]) * scale)
        o_ref[...] = jnp.concatenate(pieces, axis=1).astype(o_ref.dtype)

    @pl.when(j >= qk_tiles)
    def _():
        o_ref[...] = acc.astype(o_ref.dtype)

    @pl.when((i == n_tiles - 1) & (j == n_j - 1))
    def _():
        pltpu.make_async_copy(ys_ref.at[pl.ds(0, rows_buf)], buf.at[bn], sem.at[bn]).wait()


def _inproj_comb(slot3, ys, x1, g2, gain, sh, sc, w, n_lat_rows, n_mod, tm, rope_tables):
    rows = x1.shape[0]
    n = w.shape[1]
    tn = 2048
    n_tiles = rows // tm
    lat_tiles = n_lat_rows // tm
    per_seq = L // tm
    mod_idx = lambda i, j: (jnp.minimum((i * tm) // L, n_mod), 0, 0)
    tab_idx = lambda i, j: (jnp.where(i < lat_tiles, i % per_seq, per_seq), 0)
    n_j = n // tn
    ch = -(-tm // n_j)
    while (ch * n_j) % 8:
        ch += 1
    in_specs = [_slot_spec(slot3, tm), _slot_spec(slot3, tm, ahead=1),
                pl.BlockSpec((tm, D), lambda i, j: (i, 0)),
                pl.BlockSpec((None, 1, D), mod_idx),
                pl.BlockSpec((1, D), lambda i, j: (0, 0)),
                pl.BlockSpec((None, 1, D), mod_idx),
                pl.BlockSpec((None, 1, D), mod_idx),
                pl.BlockSpec((D, tn), lambda i, j: (0, j)),
                pl.BlockSpec((tm, RET_DK), tab_idx),
                pl.BlockSpec((tm, RET_DK), tab_idx),
                pl.BlockSpec(memory_space=pl.ANY)]
    return pl.pallas_call(
        functools.partial(_inproj_comb_kernel, tn=tn, tm=tm, n_tiles=n_tiles, n_j=n_j, ch=ch),
        grid=(n_tiles, n_j),
        in_specs=in_specs,
        out_specs=[pl.BlockSpec((tm, tn), lambda i, j: (i, j)),
                   pl.BlockSpec((tm, D), lambda i, j: (i, 0))],
        out_shape=[jax.ShapeDtypeStruct((rows, n), BF16),
                   jax.ShapeDtypeStruct((rows, D), F32)],
        scratch_shapes=[pltpu.VMEM((tm, D), BF16), pltpu.VMEM((2, ch * n_j, D), F32),
                        pltpu.SemaphoreType.DMA((2,))],
        compiler_params=_cp(("arbitrary", "arbitrary")),
        name="inproj_rope_combine",
    )(slot3, slot3, x1, g2, gain.reshape(1, D), sh, sc, w, rope_tables[0], rope_tables[1], ys)


def _rope_tables(tm):
    pos = np.arange(L)
    half = RET_DK // 4
    freqs = ROPE_BASE ** (-np.arange(half, dtype=np.float64) / half)
    ang_r = (pos // GRID_W)[:, None] * freqs[None, :]
    ang_c = (pos % GRID_W)[:, None] * freqs[None, :]
    cos = np.concatenate([np.cos(ang_r), np.cos(ang_r), np.cos(ang_c), np.cos(ang_c)], axis=1)
    sin = np.concatenate([-np.sin(ang_r), np.sin(ang_r), -np.sin(ang_c), np.sin(ang_c)], axis=1)
    cos = np.concatenate([cos, np.ones((tm, RET_DK))], axis=0)
    sin = np.concatenate([sin, np.zeros((tm, RET_DK))], axis=0)
    return jnp.asarray(cos, F32), jnp.asarray(sin, F32)


def _conv_kernel(ab_ref, ac_ref, ax_ref, w_ref, o_ref, *, seq):
    u = ac_ref[...].astype(F32) * ax_ref[...].astype(F32)
    row = lax.broadcasted_iota(jnp.int32, u.shape, 0)
    up = jnp.where(row == 0, 0.0, pltpu.roll(u, 1, axis=0))
    un = jnp.where(row == seq - 1, 0.0, pltpu.roll(u, seq - 1, axis=0))
    w = w_ref[...]
    y = ab_ref[...].astype(F32) * (up * w[0:1, :] + u * w[1:2, :] + un * w[2:3, :])
    o_ref[...] = y.astype(o_ref.dtype)


def _conv(qkv, conv_w, seq, n_seq, row_blk0):
    nb = CONV_CH // 128
    return pl.pallas_call(
        functools.partial(_conv_kernel, seq=seq),
        grid=(n_seq, nb),
        in_specs=[
            pl.BlockSpec((seq, 128), lambda b, j: (row_blk0 + b, j)),
            pl.BlockSpec((seq, 128), lambda b, j: (row_blk0 + b, nb + j)),
            pl.BlockSpec((seq, 128), lambda b, j: (row_blk0 + b, 2 * nb + j)),
            pl.BlockSpec((3, 128), lambda b, j: (0, j)),
        ],
        out_specs=pl.BlockSpec((seq, 128), lambda b, j: (b, j)),
        out_shape=jax.ShapeDtypeStruct((n_seq * seq, CONV_CH), BF16),
        compiler_params=_cp(("arbitrary", "arbitrary")),
        name="short_conv",
    )(qkv, qkv, qkv, conv_w)


def _natten_tcol(rpb):
    qc = np.arange(GRID_W)
    kc = np.arange(GRID_W)
    cs = np.clip(qc - WIN_W // 2, 0, GRID_W - WIN_W)
    col_ok = (kc[None, :] >= cs[:, None]) & (kc[None, :] < cs[:, None] + WIN_W)
    dx = np.clip(kc[None, :] - qc[:, None] + WIN_W - 1, 0, 2 * WIN_W - 2)
    tcol = jnp.where(col_ok[None, None], rpb.astype(F32)[:, :, dx], NEG)
    return tcol.reshape(NA_HEADS // 2, 2, 2 * WIN_H - 1, GRID_W, GRID_W)


def _natten_row_classes():
    table = []
    for r0 in (0, 2 * NA_QR, GRID_H - NA_QR):
        ws = int(np.clip(r0 - WIN_H // 2, 0, NA_WS_MAX))
        rows = []
        for q in range(NA_QR):
            qr = r0 + q
            rs = int(np.clip(qr - WIN_H // 2, 0, GRID_H - WIN_H))
            rows.append([(ws + k) - qr + WIN_H - 1 if rs <= ws + k < rs + WIN_H else None for k in range(NA_KR)])
        table.append(rows)
    return table


def _attend(qa, kw, vw, kc, vc, bias):
    dn = (((1,), (1,)), ((), ()))
    s2 = lax.dot_general(qa, kc, dn, preferred_element_type=F32)
    m = jnp.max(s2, axis=1, keepdims=True)
    if kw is not None:
        s1 = lax.dot_general(qa, kw, dn, preferred_element_type=F32) + bias
        m = jnp.maximum(m, jnp.max(s1, axis=1, keepdims=True))
        p1 = jnp.exp(s1 - m)
    p2 = jnp.exp(s2 - m)
    den = jnp.sum(p2, axis=1, keepdims=True)
    o = jnp.dot(p2.astype(BF16), vc, preferred_element_type=F32)
    if kw is not None:
        den = den + jnp.sum(p1, axis=1, keepdims=True)
        o = o + jnp.dot(p1.astype(BF16), vw, preferred_element_type=F32)
    return o / den


NA_SUB = 16


def _natten_kernel(q_ref, k_ref, v_ref, kc_ref, vc_ref, tcol_ref, o_ref, bias_ref):
    n_rb = GRID_H // NA_QR

    @pl.when((pl.program_id(1) == 0) & (pl.program_id(2) == 0))
    def _():
        neg = jnp.full((GRID_W, GRID_W), NEG, F32)
        for cls, rows in enumerate(_natten_row_classes()):
            for a in range(2):
                for qr, dys in enumerate(rows):
                    blocks = [neg if dy is None else tcol_ref[a, dy] for dy in dys]
                    bias_ref[cls, a, qr * GRID_W:(qr + 1) * GRID_W, :] = jnp.concatenate(blocks, axis=1)

    lane = lax.broadcasted_iota(jnp.int32, (NA_NQ, 128), 1)
    for sb in range(NA_SUB):
        rb = pl.program_id(2) * NA_SUB + sb
        ws = jnp.clip(rb * NA_QR - WIN_H // 2, 0, NA_WS_MAX)
        start = pl.multiple_of(ws * GRID_W, GRID_W)
        kw = k_ref[pl.ds(start, NA_NK), :]
        vw = v_ref[pl.ds(start, NA_NK), :]
        cls = jnp.where(rb == 0, 0, jnp.where(rb == n_rb - 1, 2, 1))
        q = q_ref[sb * NA_NQ:(sb + 1) * NA_NQ, :] * (NA_DH ** -0.5)
        zero = jnp.zeros_like(q)
        o0 = _attend(jnp.where(lane < NA_DH, q, zero), kw, vw, kc_ref[...], vc_ref[...], bias_ref[cls, 0])
        o1 = _attend(jnp.where(lane >= NA_DH, q, zero), kw, vw, kc_ref[...], vc_ref[...], bias_ref[cls, 1])
        o_ref[sb * NA_NQ:(sb + 1) * NA_NQ, :] = jnp.where(lane < NA_DH, o0, o1).astype(o_ref.dtype)


def _natten(qkv, tcol, nb):
    n_steps = GRID_H // NA_QR // NA_SUB
    qcol, kcol, vcol = 3 * CONV_CH // 128, 3 * CONV_CH // 128 + 4, 3 * CONV_CH // 128 + 8
    ctx_blk0 = nb * L // CTX
    return pl.pallas_call(
        _natten_kernel,
        grid=(NA_HEADS // 2, nb, n_steps),
        in_specs=[
            pl.BlockSpec((NA_SUB * NA_NQ, 128), lambda h, b, r: (b * n_steps + r, qcol + h)),
            pl.BlockSpec((L, 128), lambda h, b, r: (b, kcol + h)),
            pl.BlockSpec((L, 128), lambda h, b, r: (b, vcol + h)),
            pl.BlockSpec((CTX, 128), lambda h, b, r: (ctx_blk0 + b, kcol + h)),
            pl.BlockSpec((CTX, 128), lambda h, b, r: (ctx_blk0 + b, vcol + h)),
            pl.BlockSpec((None, 2, 2 * WIN_H - 1, GRID_W, GRID_W), lambda h, b, r: (h, 0, 0, 0, 0)),
        ],
        out_specs=pl.BlockSpec((NA_SUB * NA_NQ, 128), lambda h, b, r: (b * n_steps + r, h)),
        out_shape=jax.ShapeDtypeStruct((nb * L, NA_HEADS * NA_DH), BF16),
        scratch_shapes=[pltpu.VMEM((3, 2, NA_NQ, NA_NK), F32)],
        compiler_params=_cp(("arbitrary", "arbitrary", "arbitrary")),
        name="natten",
    )(qkv, qkv, qkv, qkv, qkv, tcol)


def _ctx_attn_kernel(q_ref, kc_ref, vc_ref, o_ref):
    q = q_ref[...] * (NA_DH ** -0.5)
    lane = lax.broadcasted_iota(jnp.int32, q.shape, 1)
    zero = jnp.zeros_like(q)
    o0 = _attend(jnp.where(lane < NA_DH, q, zero), None, None, kc_ref[...], vc_ref[...], None)
    o1 = _attend(jnp.where(lane >= NA_DH, q, zero), None, None, kc_ref[...], vc_ref[...], None)
    o_ref[...] = jnp.where(lane < NA_DH, o0, o1).astype(o_ref.dtype)


def _ctx_attn(qkv, nb):
    qcol, kcol, vcol = 3 * CONV_CH // 128, 3 * CONV_CH // 128 + 4, 3 * CONV_CH // 128 + 8
    ctx_blk0 = nb * L // CTX
    return pl.pallas_call(
        _ctx_attn_kernel,
        grid=(nb, NA_HEADS // 2),
        in_specs=[
            pl.BlockSpec((CTX, 128), lambda b, h: (ctx_blk0 + b, qcol + h)),
            pl.BlockSpec((CTX, 128), lambda b, h: (ctx_blk0 + b, kcol + h)),
            pl.BlockSpec((CTX, 128), lambda b, h: (ctx_blk0 + b, vcol + h)),
        ],
        out_specs=pl.BlockSpec((CTX, 128), lambda b, h: (b, h)),
        out_shape=jax.ShapeDtypeStruct((nb * CTX, NA_HEADS * NA_DH), BF16),
        compiler_params=_cp(("arbitrary", "arbitrary")),
        name="ctx_attn",
    )(qkv, qkv, qkv)


def _retention_kernel(lg_ref, q_ref, k_ref, v_ref, g_ref, kc_ref, vc_ref, o_ref, y_scr, stf_scr, stb_scr):
    h = pl.program_id(1)
    n_chunks = L // RET_C
    pos_r = lax.broadcasted_iota(jnp.int32, (RET_C, 1), 0).astype(F32)
    rel = pos_r - lax.broadcasted_iota(jnp.int32, (1, RET_C), 1).astype(F32)
    dn_t = (((1,), (1,)), ((), ()))
    dn_k = (((0,), (0,)), ((), ()))

    def decays(lg, fwd):
        if fwd:
            d_intra = jnp.where(rel >= 0, jnp.exp(lg * jnp.maximum(rel, 0.0)), 0.0)
            d_q = jnp.exp(lg * (pos_r + 1.0))
            d_k = jnp.exp(lg * (RET_C - 1.0 - pos_r))
        else:
            d_intra = jnp.where(rel <= 0, jnp.exp(lg * jnp.maximum(-rel, 0.0)), 0.0)
            d_q = jnp.exp(lg * (RET_C - pos_r))
            d_k = jnp.exp(lg * pos_r)
        d_c = jnp.exp(jnp.full((1, 1), RET_C, F32) * lg)
        return d_intra, d_q, d_k, d_c

    def ctx_state(st_scr, d_k, d_c, order):
        st_scr[...] = jnp.zeros_like(st_scr)
        for jc in order:
            kd = (kc_ref[jc * RET_C:(jc + 1) * RET_C, :].astype(F32) * d_k).astype(BF16)
            vc = vc_ref[jc * RET_C:(jc + 1) * RET_C, :]
            st_scr[...] = st_scr[...] * d_c + lax.dot_general(kd, vc, dn_k, preferred_element_type=F32)

    def chunk_out(st_scr, c, d_intra, d_q, d_k, d_c):
        r0 = pl.multiple_of(c * RET_C, RET_C)
        q = q_ref[pl.ds(r0, RET_C), :]
        k = k_ref[pl.ds(r0, RET_C), :]
        v = v_ref[pl.ds(r0, RET_C), :]
        s = lax.dot_general(q, k, dn_t, preferred_element_type=F32) * d_intra
        st = st_scr[...]
        o = (jnp.dot(s.astype(BF16), v, preferred_element_type=F32)
             + jnp.dot(q, st.astype(BF16), preferred_element_type=F32) * d_q)
        kd = (k.astype(F32) * d_k).astype(BF16)
        st_scr[...] = st * d_c + lax.dot_general(kd, v, dn_k, preferred_element_type=F32)
        return r0, o

    def finish(r0, y):
        y = y * lax.rsqrt(jnp.mean(y * y, axis=-1, keepdims=True) + EPS)
        g = g_ref[pl.ds(r0, RET_C), :].astype(F32)
        o_ref[pl.ds(r0, RET_C), :] = (_silu(g) * y).astype(o_ref.dtype)

    dec_f = decays(lg_ref[0, h], True)
    dec_b = decays(lg_ref[1, h], False)
    ctx_state(stf_scr, dec_f[2], dec_f[3], range(CTX // RET_C))
    ctx_state(stb_scr, dec_b[2], dec_b[3], range(CTX // RET_C - 1, -1, -1))

    def first_half(i, carry):
        r0, o = chunk_out(stf_scr, i, *dec_f)
        y_scr[pl.ds(r0, RET_C), :] = o
        r1, o = chunk_out(stb_scr, n_chunks - 1 - i, *dec_b)
        y_scr[pl.ds(r1, RET_C), :] = o
        return carry

    lax.fori_loop(0, n_chunks // 2, first_half, 0)

    def second_half(i, carry):
        r0, o = chunk_out(stf_scr, i, *dec_f)
        finish(r0, y_scr[pl.ds(r0, RET_C), :] + o)
        r1, o = chunk_out(stb_scr, n_chunks - 1 - i, *dec_b)
        finish(r1, y_scr[pl.ds(r1, RET_C), :] + o)
        return carry

    lax.fori_loop(n_chunks // 2, n_chunks, second_half, 0)


def _retention(qkvg, lg, nb):
    ctx_blk0 = nb * L // CTX
    kq = RET_HEADS
    vq = 2 * RET_HEADS * RET_DK // RET_DV
    gq = vq + RET_HEADS
    return pl.pallas_call(
        _retention_kernel,
        grid=(nb, RET_HEADS),
        in_specs=[
            pl.BlockSpec(memory_space=pltpu.SMEM),
            pl.BlockSpec((L, RET_DK), lambda b, h: (b, h)),
            pl.BlockSpec((L, RET_DK), lambda b, h: (b, kq + h)),
            pl.BlockSpec((L, RET_DV), lambda b, h: (b, vq + h)),
            pl.BlockSpec((L, RET_DV), lambda b, h: (b, gq + h)),
            pl.BlockSpec((CTX, RET_DK), lambda b, h: (ctx_blk0 + b, kq + h)),
            pl.BlockSpec((CTX, RET_DV), lambda b, h: (ctx_blk0 + b, vq + h)),
        ],
        out_specs=pl.BlockSpec((L, RET_DV), lambda b, h: (b, h)),
        out_shape=jax.ShapeDtypeStruct((nb * L, RET_HEADS * RET_DV), BF16),
        scratch_shapes=[pltpu.VMEM((L, RET_DV), F32), pltpu.VMEM((RET_DK, RET_DV), F32),
                        pltpu.VMEM((RET_DK, RET_DV), F32)],
        compiler_params=_cp(("arbitrary", "arbitrary")),
        name="retention",
    )(lg, qkvg, qkvg, qkvg, qkvg, qkvg, qkvg)


def _outproj_kernel(*refs, src_lens, lat_tiles):
    i = pl.program_id(0)
    srcs = []
    for n in src_lens:
        srcs.append(refs[:n])
        refs = refs[n:]
    n_act = len(src_lens) - 1
    w_refs, refs = refs[:n_act], refs[n_act:]
    (g1_ref, sh2_ref, sc2_ref, n2_ref, wr_ref, br_ref, tri_ref,
     x1_ref, h2x_ref, rt_ref, cnt_ref, carry_scr) = refs

    @pl.when(i == 0)
    def _():
        carry_scr[...] = jnp.zeros_like(carry_scr)

    coef = n2_ref[...] * (1.0 + sc2_ref[...])
    rs = tri_ref.shape[0]
    for sbk in range(x1_ref.shape[0] // rs):
        rows = slice(sbk * rs, (sbk + 1) * rs)
        y = None
        for a_src, w_ref in zip(srcs[:n_act], w_refs):
            t = jnp.dot(_src_load(a_src, i, lat_tiles, rows), w_ref[...], preferred_element_type=F32)
            y = t if y is None else y + t
        x1 = _src_load(srcs[n_act], i, lat_tiles, rows) + g1_ref[...] * y
        x1_ref[rows, :] = x1
        ms = jnp.mean(x1 * x1, axis=-1, keepdims=True)
        h2 = x1 * lax.rsqrt(ms + EPS) * coef + sh2_ref[...]
        h2x_ref[rows, :D] = h2
        h_hi = h2.astype(BF16)
        h_lo = (h2 - h_hi.astype(F32)).astype(BF16)
        p = jnp.dot(h_hi, wr_ref[...], preferred_element_type=F32)
        q = jnp.dot(h_lo, wr_ref[:, :128], preferred_element_type=F32)
        route = _route(p[:, :128] + p[:, 128:] + q + br_ref[...], tri_ref, carry_scr)
        h2x_ref[rows, D:] = route
        rt_ref[:, rows] = route.T[8:16, :]
    cnt_ref[...] = carry_scr[...]


def _outproj(acts, ws, xsrc, g1, sh2, sc2, n2g, wr2, br, rows, n_lat_rows, n_mod, tm):
    lat_tiles = n_lat_rows // tm
    rs = tm
    tri = jnp.asarray(np.tril(np.ones((rs, rs), np.float32), -1), BF16)
    mod_idx = lambda i: (jnp.minimum((i * tm) // L, n_mod), 0, 0)
    in_specs, args = [], []
    for src in list(acts) + [xsrc]:
        in_specs += _src_specs(src, tm, lat_tiles)
        args += list(src)
    in_specs += [pl.BlockSpec(w.shape, lambda i: (0, 0)) for w in ws]
    in_specs += [pl.BlockSpec((None, 1, D), mod_idx),
                 pl.BlockSpec((None, 1, D), mod_idx),
                 pl.BlockSpec((None, 1, D), mod_idx),
                 pl.BlockSpec((1, D), lambda i: (0, 0)),
                 pl.BlockSpec((D, 256), lambda i: (0, 0)),
                 pl.BlockSpec((1, 128), lambda i: (0, 0)),
                 pl.BlockSpec((rs, rs), lambda i: (0, 0))]
    args += list(ws) + [g1, sh2, sc2, n2g.reshape(1, D), wr2, br, tri]
    return pl.pallas_call(
        functools.partial(_outproj_kernel, src_lens=tuple(len(s) for s in list(acts) + [xsrc]),
                          lat_tiles=lat_tiles),
        grid=(rows // tm,),
        in_specs=in_specs,
        out_specs=[pl.BlockSpec((tm, D), lambda i: (i, 0)),
                   pl.BlockSpec((tm, DX), lambda i: (i, 0)),
                   pl.BlockSpec((None, 8, tm), lambda i: (i, 0, 0)),
                   pl.BlockSpec((8, 128), lambda i: (0, 0))],
        out_shape=[jax.ShapeDtypeStruct((rows, D), F32),
                   jax.ShapeDtypeStruct((rows, DX), F32),
                   jax.ShapeDtypeStruct((rows // tm, 8, tm), F32),
                   jax.ShapeDtypeStruct((8, 128), F32)],
        scratch_shapes=[pltpu.VMEM((8, 128), F32)],
        compiler_params=_cp(("arbitrary",)),
        name="outproj_router",
    )(*args)


def _route(lg, tri_ref, carry_scr):
    lane = lax.broadcasted_iota(jnp.int32, lg.shape, 1)
    lane_f = lane.astype(F32)
    big = jnp.float32(1e9)

    gmask = lane < N_GROUPS
    gl = jnp.where(gmask, lg, NEG)
    gm = jnp.max(gl, axis=1, keepdims=True)
    gidx = jnp.min(jnp.where(gl == gm, lane_f, big), axis=1, keepdims=True)
    g_val = 1.0 / jnp.sum(jnp.where(gmask, jnp.exp(gl - gm), 0.0), axis=1, keepdims=True)

    base = 8.0 + 8.0 * gidx
    emask = (lane_f >= base) & (lane_f < base + EPG)
    el = jnp.where(emask, lg, NEG)
    m1 = jnp.max(el, axis=1, keepdims=True)
    i1 = jnp.min(jnp.where(el == m1, lane_f, big), axis=1, keepdims=True)
    el2 = jnp.where(lane_f == i1, NEG, el)
    m2 = jnp.max(el2, axis=1, keepdims=True)
    i2 = jnp.min(jnp.where(el2 == m2, lane_f, big), axis=1, keepdims=True)
    r = jnp.exp(m2 - m1)
    w1 = g_val / (1.0 + r)
    w2 = w1 * r
    gates = jnp.where(lane_f == i1 - base, w1, jnp.where(lane_f == i2 - base, w2, 0.0))

    onehot = jnp.where(lane_f == gidx, 1.0, 0.0)
    before = jnp.dot(tri_ref[...], onehot.astype(BF16), preferred_element_type=F32)
    carry = carry_scr[0:1, :]
    rank = jnp.sum(jnp.where(lane_f == gidx, before + carry, 0.0), axis=1, keepdims=True)
    carry_scr[0:1, :] = carry + jnp.sum(onehot, axis=0, keepdims=True)

    return jnp.where(lane < EPG, gates, jnp.where(lane == 8, gidx, jnp.where(lane == 9, rank, 0.0)))


def _slot_kernel(rt_ref, st_ref, o_ref):
    g = rt_ref[0:1, :]
    slot = rt_ref[1:2, :]
    st = st_ref[...]
    for k in range(N_GROUPS):
        slot = slot + jnp.where(g == k, st[:, k:k + 1], 0.0)
    o_ref[...] = slot.astype(jnp.int32)


def _slots(route_t, starts_row):
    n, _, st = route_t.shape
    return pl.pallas_call(
        _slot_kernel,
        grid=(n,),
        in_specs=[pl.BlockSpec((None, 8, st), lambda i: (i, 0, 0)),
                  pl.BlockSpec((1, 128), lambda i: (0, 0))],
        out_specs=pl.BlockSpec((None, 1, st), lambda i: (i, 0, 0)),
        out_shape=jax.ShapeDtypeStruct((n, 1, st), jnp.int32),
        compiler_params=_cp(("arbitrary",)),
        name="moe_slots",
    )(route_t, starts_row)


def _slot_spec(slot3, tile, ahead=0):
    n, _, st = slot3.shape
    per = st // tile
    last = n * per - 1

    def idx(i, *_):
        t = jnp.minimum(i + ahead, last)
        return (t // per, 0, t % per)

    return pl.BlockSpec((None, 1, tile), idx, memory_space=pltpu.SMEM)


def _dispatch_kernel(slot_ref, pad_ref, h2x_ref, xs_ref, zx_scr, sem):
    i = pl.program_id(0)
    n_pad = pad_ref.shape[1]

    @pl.when(i == 0)
    def _():
        zx_scr[...] = jnp.zeros_like(zx_scr)

        for r in range(n_pad):
            pltpu.make_async_copy(zx_scr.at[pl.ds(0, 1)], xs_ref.at[pl.ds(pad_ref[0, r], 1)],
                                  sem.at[0]).start(priority=r % 2)
        pltpu.make_async_copy(xs_ref.at[pl.ds(0, n_pad)], xs_ref.at[pl.ds(0, n_pad)], sem.at[0]).wait()

    td = h2x_ref.shape[0]
    for r in range(td):
        pltpu.make_async_copy(h2x_ref.at[pl.ds(r, 1)], xs_ref.at[pl.ds(slot_ref[0, r], 1)],
                              sem.at[0]).start(priority=r % 2)
    pltpu.make_async_copy(h2x_ref, xs_ref.at[pl.ds(0, td)], sem.at[0]).wait()


def _dispatch(slot3, pad_slots, h2x, n_sorted):
    rows = h2x.shape[0]
    n_pad = pad_slots.shape[0]
    td = slot3.shape[2]
    return pl.pallas_call(
        _dispatch_kernel,
        grid=(rows // td,),
        in_specs=[_slot_spec(slot3, td),
                  pl.BlockSpec((1, n_pad), lambda i: (0, 0), memory_space=pltpu.SMEM),
                  pl.BlockSpec((td, DX), lambda i: (i, 0))],
        out_specs=pl.BlockSpec(memory_space=pl.ANY),
        out_shape=jax.ShapeDtypeStruct((n_sorted, DX), F32),
        scratch_shapes=[pltpu.VMEM((8, DX), F32), pltpu.SemaphoreType.DMA((1,))],
        compiler_params=_cp(("arbitrary",)),
        name="moe_dispatch",
    )(slot3, pad_slots.reshape(1, n_pad), h2x)


def _moe_kernel(tg_ref, na_ref, x_ref, w1_ref, w3_ref, w2_ref, o_ref, hid_scr, w1b_scr, w3b_scr, w2b_scr):
    i = pl.program_id(0)

    @pl.when((i == 0) | (tg_ref[i] != tg_ref[jnp.maximum(i - 1, 0)]))
    def _():
        for e in range(EPG):
            cols = slice(e * EH, (e + 1) * EH)
            w1b_scr[:, cols] = w1_ref[:, cols].astype(BF16)
            w3b_scr[:, cols] = w3_ref[:, cols].astype(BF16)
            w2b_scr[cols, :] = w2_ref[cols, :].astype(BF16)

    @pl.when(i < na_ref[0])
    def _():
        x = x_ref[:, :D].astype(BF16)
        gt = x_ref[:, D:]
        for e in range(EPG):
            cols = slice(e * EH, (e + 1) * EH)
            h1 = jnp.dot(x, w1b_scr[:, cols], preferred_element_type=F32)
            h3 = jnp.dot(x, w3b_scr[:, cols], preferred_element_type=F32)
            hid_scr[:, cols] = (_silu(h1) * h3 * gt[:, e:e + 1]).astype(BF16)
        o_ref[...] = jnp.dot(hid_scr[...], w2b_scr[...], preferred_element_type=F32)

    @pl.when(i >= na_ref[0])
    def _():
        o_ref[...] = jnp.zeros_like(o_ref)


def _moe(tile_group, n_active, xs, w1, w3, w2, layer, n_tiles):
    x_idx = lambda i, tg, na: (jnp.minimum(i, na[0] - 1), 0)
    out_idx = lambda i, tg, na: (i, 0)
    w_idx = lambda i, tg, na: (layer, tg[i], 0, 0)
    once = pl.Buffered(1)
    grid_spec = pltpu.PrefetchScalarGridSpec(
        num_scalar_prefetch=2,
        grid=(n_tiles,),
        in_specs=[pl.BlockSpec((TS, DX), x_idx),
                  pl.BlockSpec((None, None, D, GH), w_idx, pipeline_mode=once),
                  pl.BlockSpec((None, None, D, GH), w_idx, pipeline_mode=once),
                  pl.BlockSpec((None, None, GH, D), w_idx, pipeline_mode=once)],
        out_specs=pl.BlockSpec((TS, D), out_idx),
        scratch_shapes=[pltpu.VMEM((TS, GH), BF16), pltpu.VMEM((D, GH), BF16), pltpu.VMEM((D, GH), BF16),
                        pltpu.VMEM((GH, D), BF16)],
    )
    return pl.pallas_call(
        _moe_kernel,
        grid_spec=grid_spec,
        out_shape=jax.ShapeDtypeStruct((xs.shape[0], D), F32),
        compiler_params=_cp(("arbitrary",)),
        name="moe_experts",
    )(tile_group, n_active, xs, w1, w3, w2)


def _combine_kernel(*refs, final, n_steps):
    if final:
        slot_ref, slot_next_ref, x1_ref, g2_ref, fg_ref, ys_ref, o_ref, buf, sem = refs
    else:
        slot_ref, slot_next_ref, x1_ref, g2_ref, ys_ref, o_ref, buf, sem = refs
    i = pl.program_id(0)

    def start_row(sref, b, r, u):
        pltpu.make_async_copy(ys_ref.at[pl.ds(sref[0, r], 1)], buf.at[b, pl.ds(r, 1)], sem.at[b]).start(priority=u)

    @pl.when(i == 0)
    def _():
        def body(j, carry):
            for u in range(2):
                start_row(slot_ref, 0, 2 * j + u, u)
            return carry

        lax.fori_loop(0, TD // 2, body, 0)

    b = i % 2
    bn = (i + 1) % 2
    pltpu.make_async_copy(ys_ref.at[pl.ds(0, TD)], buf.at[b], sem.at[b]).wait()

    for r in range(TD):
        start_row(slot_next_ref, bn, r, r % 2)

    x2 = x1_ref[...] + g2_ref[...] * buf[b]
    if final:
        ms = jnp.mean(x2 * x2, axis=-1, keepdims=True)
        x2 = x2 * lax.rsqrt(ms + EPS) * fg_ref[...]
    o_ref[...] = x2

    @pl.when(i == n_steps - 1)
    def _():
        pltpu.make_async_copy(ys_ref.at[pl.ds(0, TD)], buf.at[bn], sem.at[bn]).wait()


def _combine(slot3, x1, g2, ys, n_mod, final_g=None):
    rows = x1.shape[0]
    n_steps = rows // TD
    final = final_g is not None
    mod_idx = lambda i: (jnp.minimum((i * TD) // L, n_mod), 0, 0)
    in_specs = [_slot_spec(slot3, TD), _slot_spec(slot3, TD, ahead=1),
                pl.BlockSpec((TD, D), lambda i: (i, 0)),
                pl.BlockSpec((None, 1, D), mod_idx)]
    args = [slot3, slot3, x1, g2]
    if final:
        in_specs.append(pl.BlockSpec((1, D), lambda i: (0, 0)))
        args.append(final_g.reshape(1, D))
    in_specs.append(pl.BlockSpec(memory_space=pl.ANY))
    args.append(ys)
    return pl.pallas_call(
        functools.partial(_combine_kernel, final=final, n_steps=n_steps),
        grid=(n_steps,),
        in_specs=in_specs,
        out_specs=pl.BlockSpec((TD, D), lambda i: (i, 0)),
        out_shape=jax.ShapeDtypeStruct((rows, D), F32),
        scratch_shapes=[pltpu.VMEM((2, TD, D), F32), pltpu.SemaphoreType.DMA((2,))],
        compiler_params=_cp(("arbitrary",)),
        name="moe_combine_final" if final else "moe_combine",
    )(*args)


def _sort_plan(counts_row, rows):
    counts = counts_row[:N_GROUPS].astype(jnp.int32)
    padded = ((counts + TS - 1) // TS) * TS
    ends = jnp.cumsum(padded)
    starts = ends - padded
    starts_row = jnp.zeros((1, 128), F32).at[0, :N_GROUPS].set(starts.astype(F32))
    n_tiles = rows // TS + N_GROUPS
    n_active = (ends[-1] // TS).reshape(1)
    tile_start = jnp.arange(n_tiles, dtype=jnp.int32) * TS
    tile_group = jnp.minimum(jnp.sum(tile_start[:, None] >= ends[None, :], axis=1), N_GROUPS - 1).astype(jnp.int32)
    r = jnp.arange(TS, dtype=jnp.int32)
    pad_real = ((starts + counts)[:, None] + r[None, :]).reshape(-1)
    is_pad = pad_real < jnp.repeat(ends, TS)
    tail = ends[-1] + jnp.cumsum(jnp.logical_not(is_pad).astype(jnp.int32)) - 1
    pad_slots = jnp.where(is_pad, pad_real, tail)
    return starts_row, pad_slots, tile_group, n_active, n_tiles, n_tiles * TS


def _moe_sorted(h2x, route_t, cnt, w1, w3, w2, layer):
    rows = h2x.shape[0]
    starts_row, pad_slots, tile_group, n_active, n_tiles, n_sorted = _sort_plan(cnt[0], rows)
    slot3 = _slots(route_t, starts_row)
    xs = _dispatch(slot3, pad_slots, h2x, n_sorted)
    ys = _moe(tile_group, n_active, xs, w1, w3, w2, layer, n_tiles)
    return slot3, ys


def _router_weights(gw, gb, ew, eb):
    wr = jnp.zeros((D, 128), F32).at[:, :N_GROUPS].set(gw).at[:, 8:8 + N_GROUPS * EPG].set(ew)
    br = jnp.zeros((1, 128), F32).at[0, :N_GROUPS].set(gb).at[0, 8:8 + N_GROUPS * EPG].set(eb)
    hi = wr.astype(BF16)
    lo = (wr - hi.astype(F32)).astype(BF16)
    return jnp.concatenate([hi, lo], axis=1), br


def kernel(x, c, ctx, c_ctx, ada_w, ada_b, norm1_g, norm2_g, w_in_e, conv_w, na_rpb, w_out_e, w_in_o,
           ret_decay_f, ret_decay_b, w_out_o, router_gw, router_gb, router_ew, router_eb,
           moe_w1, moe_w3, moe_w2, final_g):
    nb = x.shape[0]
    assert x.shape[1:] == (L, D) and ctx.shape[1:] == (CTX, D) and nb < 16 and ada_w.shape[0] == 2
    t_lat = nb * L
    t_ctx = nb * CTX
    t_all = t_lat + t_ctx
    tm = 1024 if t_ctx % 1024 == 0 else 512

    xsrc = (x.reshape(t_lat, D), ctx.reshape(t_ctx, D))
    c16 = jnp.zeros((16, D), F32).at[:nb].set(c).at[nb].set(c_ctx)
    mod = _ada(c16, ada_w, ada_b).reshape(2, 16, 6, 1, D)
    mods = [[mod[l, :, k] for k in range(6)] for l in range(2)]

    sh1, sc1, g1, sh2, sc2, g2 = mods[0]
    qkv = _inproj(xsrc, norm1_g[0], sh1, sc1, w_in_e[0].astype(BF16), t_lat, nb, tm)
    y_conv = (_conv(qkv, conv_w[0], L, nb, 0), _conv(qkv, conv_w[0], CTX, nb, t_lat // CTX))
    y_attn = (_natten(qkv, _natten_tcol(na_rpb[0]), nb), _ctx_attn(qkv, nb))
    w_out = w_out_e[0].astype(BF16)
    wr2, br = _router_weights(router_gw[0], router_gb[0], router_ew[0], router_eb[0])
    x1, h2x, route_t, cnt = _outproj([y_conv, y_attn], [w_out[:CONV_CH], w_out[CONV_CH:]], xsrc, g1, sh2, sc2,
                                     norm2_g[0], wr2, br, t_all, t_lat, nb, tm)
    slot3, ys = _moe_sorted(h2x, route_t, cnt, moe_w1, moe_w3, moe_w2, 0)

    sh1, sc1, g1n, sh2n, sc2n, g2n = mods[1]
    qkvg, xa = _inproj_comb(slot3, ys, x1, g2, norm1_g[1], sh1, sc1, w_in_o[0].astype(BF16), t_lat, nb, tm,
                            _rope_tables(tm))
    g1, sh2, sc2, g2 = g1n, sh2n, sc2n, g2n
    lg = jnp.stack([jnp.log1p(-jnp.exp(ret_decay_f[0].astype(F32))),
                    jnp.log1p(-jnp.exp(ret_decay_b[0].astype(F32)))])
    y_ret = _retention(qkvg, lg, nb)
    wr2, br = _router_weights(router_gw[1], router_gb[1], router_ew[1], router_eb[1])
    x1, h2x, route_t, cnt = _outproj([(y_ret,)], [w_out_o[0].astype(BF16)], (xa,), g1, sh2, sc2, norm2_g[1],
                                     wr2, br, t_lat, t_lat, nb, tm)
    slot3, ys = _moe_sorted(h2x, route_t, cnt, moe_w1, moe_w3, moe_w2, 1)
    out = _combine(slot3, x1, g2, ys, nb, final_g)
    return out.reshape(nb, L, D)
```
